```python
import math
import jax, jax.numpy as jnp
from jax import lax
import numpy as np

D_MODEL = 2048
BATCH = 4
SEQ = 2048
DEPTH = 1
DEC_BATCH = 128
DEC_SEQ = 8
PAST_LEN = 8192
PAGE_SIZE = 128

SSM_EXPAND = 2
D_INNER = SSM_EXPAND * D_MODEL
SSM_HEAD_DIM = 64
SSM_HEADS = D_INNER // SSM_HEAD_DIM
SSM_GROUPS = 8
SSM_HPG = SSM_HEADS // SSM_GROUPS
SSM_STATE = 128
CONV_WIDTH = 4
GN = SSM_GROUPS * SSM_STATE
CONV_DIM = D_INNER + 2 * GN
SSD_CHUNK = 128

ATTN_HEAD_DIM = 64
ATTN_Q_HEADS = D_MODEL // ATTN_HEAD_DIM
ATTN_KV_HEADS = ATTN_Q_HEADS // 4
ATTN_REP = ATTN_Q_HEADS // ATTN_KV_HEADS
ATTN_WIDTH = ATTN_Q_HEADS * ATTN_HEAD_DIM
KV_WIDTH = ATTN_KV_HEADS * ATTN_HEAD_DIM
WINDOW = 128

N_BUCKETS = 32
MAX_EXACT = N_BUCKETS // 2
MAX_DISTANCE = WINDOW

RMS_EPS = 1e-6

OFF_Z = 0
OFF_XBC = OFF_Z + D_INNER
OFF_DT = OFF_XBC + CONV_DIM
OFF_Q = OFF_DT + SSM_HEADS
OFF_K = OFF_Q + ATTN_WIDTH
OFF_V = OFF_K + KV_WIDTH
OFF_GA = OFF_V + KV_WIDTH
OFF_MIX_S = OFF_GA + ATTN_WIDTH
OFF_MIX_A = OFF_MIX_S + D_MODEL
IN_COLS = OFF_MIX_A + D_MODEL

kernel_name = 'hybrid_ssd_swa_gated_step'


def _rmsnorm(x, w):
    xf = x.astype(jnp.float32)
    xf = xf * lax.rsqrt(jnp.mean(xf * xf, axis=-1, keepdims=True) + RMS_EPS)
    return (xf * w.astype(jnp.float32)).astype(x.dtype)


def _gated_group_rmsnorm(y, z, w):
    g = y.astype(jnp.float32) * jax.nn.silu(z.astype(jnp.float32))
    shp = g.shape
    g = g.reshape(shp[:-1] + (SSM_GROUPS, shp[-1] // SSM_GROUPS))
    g = g * lax.rsqrt(jnp.mean(g * g, axis=-1, keepdims=True) + RMS_EPS)
    return (g.reshape(shp) * w.astype(jnp.float32)).astype(y.dtype)


def _t5_bucket(dist):
    n = jnp.maximum(dist, 0)
    nf = jnp.maximum(n, 1).astype(jnp.float32)
    large = MAX_EXACT + (jnp.log(nf / MAX_EXACT) / math.log(MAX_DISTANCE / MAX_EXACT)
                         * (N_BUCKETS - MAX_EXACT)).astype(jnp.int32)
    large = jnp.minimum(large, N_BUCKETS - 1)
    return jnp.where(n < MAX_EXACT, n, large)


def _rel_bias(dist, table):
    q, k = dist.shape
    bias = table[_t5_bucket(dist)].astype(jnp.float32)
    return jnp.moveaxis(bias, -1, 0).reshape(ATTN_KV_HEADS, ATTN_REP, q, k)


def _sink_attention(q, k, v, bias, valid, sinks):
    scale = ATTN_HEAD_DIM ** -0.5
    s = jnp.einsum('...qhrd,...khd->...hrqk', q, k).astype(jnp.float32) * scale + bias
    s = jnp.where(valid, s, -jnp.inf)
    sink = sinks.astype(jnp.float32).reshape(ATTN_KV_HEADS, ATTN_REP, 1, 1)
    m = jnp.maximum(jnp.max(s, axis=-1, keepdims=True), sink)
    p = jnp.exp(s - m)
    denom = jnp.sum(p, axis=-1, keepdims=True) + jnp.exp(sink - m)
    return jnp.einsum('...hrqk,...khd->...qhrd', (p / denom).astype(v.dtype), v)


def _banded_window_attention(q, k, v, rel_table, sinks):
    b, l = q.shape[:2]
    nb = l // WINDOW
    qb = q.reshape(b, nb, WINDOW, ATTN_KV_HEADS, ATTN_REP, ATTN_HEAD_DIM)
    kb = k.reshape(b, nb, WINDOW, ATTN_KV_HEADS, ATTN_HEAD_DIM)
    vb = v.reshape(b, nb, WINDOW, ATTN_KV_HEADS, ATTN_HEAD_DIM)

    def with_prev(t):
        prev = jnp.concatenate([jnp.zeros_like(t[:, :1]), t[:, :-1]], axis=1)
        return jnp.concatenate([prev, t], axis=2)

    kk, vv = with_prev(kb), with_prev(vb)
    qi = jnp.arange(WINDOW)[:, None]
    kj = jnp.arange(2 * WINDOW)[None, :]
    dist = qi + WINDOW - kj
    band = (dist >= 0) & (dist <= WINDOW)
    kpos = jnp.arange(nb)[:, None, None] * WINDOW + kj[None] - WINDOW
    valid = (band[None] & (kpos >= 0))[None, :, None, None]
    o = _sink_attention(qb, kk, vv, _rel_bias(dist, rel_table), valid, sinks)
    return o.reshape(b, l, ATTN_KV_HEADS, ATTN_REP, ATTN_HEAD_DIM)


def _window_decode_attention(q, kk, vv, rel_table, sinks):
    l = q.shape[1]
    kl = kk.shape[1]
    w_buf = kl - l
    dist = jnp.arange(l)[:, None] + w_buf - jnp.arange(kl)[None, :]
    valid = (dist >= 0) & (dist <= WINDOW)
    return _sink_attention(q, kk, vv, _rel_bias(dist, rel_table), valid, sinks)


def _causal_conv(xbc, buf, w, b):
    l = xbc.shape[1]
    xp = jnp.concatenate([buf.astype(xbc.dtype), xbc], axis=1)
    y = b
    for i in range(CONV_WIDTH):
        y = y + xp[:, i:i + l] * w[i]
    return jax.nn.silu(y), xp[:, -(CONV_WIDTH - 1):]


def _ssd(xs, dt, A, Bm, Cm, init_state, chunk):
    f32 = jnp.float32
    b, l = xs.shape[:2]
    c = l // chunk
    X = (xs.astype(f32) * dt[..., None]).reshape(b, c, chunk, SSM_GROUPS, SSM_HPG, SSM_HEAD_DIM)
    dA = (dt * A).reshape(b, c, chunk, SSM_GROUPS, SSM_HPG)
    Bc = Bm.astype(f32).reshape(b, c, chunk, SSM_GROUPS, SSM_STATE)
    Cc = Cm.astype(f32).reshape(b, c, chunk, SSM_GROUPS, SSM_STATE)
    Acs = jnp.cumsum(dA, axis=2)
    causal = jnp.tril(jnp.ones((chunk, chunk), bool))[:, :, None, None]
    seg = Acs[:, :, :, None] - Acs[:, :, None, :]
    Lmat = jnp.where(causal, jnp.exp(jnp.where(causal, seg, 0.0)), 0.0)
    CB = jnp.einsum('bclgn,bcsgn->bclsg', Cc, Bc)
    y_diag = jnp.einsum('bclsg,bclsgr,bcsgrp->bclgrp', CB, Lmat, X)
    decay_states = jnp.exp(Acs[:, :, -1:] - Acs)
    states = jnp.einsum('bclgn,bclgr,bclgrp->bcgrpn', Bc, decay_states, X)
    chunk_decay = jnp.exp(Acs[:, :, -1])

    def step(s, inp):
        dec, st = inp
        return s * dec[..., None, None] + st, s

    s0 = init_state.astype(f32).reshape(b, SSM_GROUPS, SSM_HPG, SSM_HEAD_DIM, SSM_STATE)
    final, prev = lax.scan(step, s0, (jnp.moveaxis(chunk_decay, 1, 0), jnp.moveaxis(states, 1, 0)))
    prev = jnp.moveaxis(prev, 0, 1)
    y_off = jnp.einsum('bclgn,bcgrpn,bclgr->bclgrp', Cc, prev, jnp.exp(Acs))
    y = (y_diag + y_off).reshape(b, l, SSM_HEADS, SSM_HEAD_DIM)
    return y.astype(xs.dtype), final.reshape(b, SSM_HEADS, SSM_HEAD_DIM, SSM_STATE).astype(xs.dtype)


def _layer(x, conv_buf, ssm_init, k_buf, v_buf, norm_w, w_in, conv_w, conv_b, dt_bias, a_log, d_skip,
           ssm_norm_w, w_ssm_branch, attn_sinks, w_attn_branch, w_out, rel_table):
    b, l, _ = x.shape
    h = _rmsnorm(x, norm_w)
    u = jnp.einsum('bld,de->ble', h, w_in)
    z = u[..., OFF_Z:OFF_XBC]
    xbc_raw = u[..., OFF_XBC:OFF_DT]
    dt_raw = u[..., OFF_DT:OFF_Q]
    q = u[..., OFF_Q:OFF_K].reshape(b, l, ATTN_KV_HEADS, ATTN_REP, ATTN_HEAD_DIM)
    k = u[..., OFF_K:OFF_V].reshape(b, l, ATTN_KV_HEADS, ATTN_HEAD_DIM)
    v = u[..., OFF_V:OFF_GA].reshape(b, l, ATTN_KV_HEADS, ATTN_HEAD_DIM)
    g_attn = u[..., OFF_GA:OFF_MIX_S]
    mix_s = u[..., OFF_MIX_S:OFF_MIX_A]
    mix_a = u[..., OFF_MIX_A:IN_COLS]

    xbc, new_conv = _causal_conv(xbc_raw, conv_buf, conv_w, conv_b)
    xs = xbc[..., :D_INNER].reshape(b, l, SSM_HEADS, SSM_HEAD_DIM)
    Bm = xbc[..., D_INNER:D_INNER + GN].reshape(b, l, SSM_GROUPS, SSM_STATE)
    Cm = xbc[..., D_INNER + GN:].reshape(b, l, SSM_GROUPS, SSM_STATE)
    dt = jax.nn.softplus(dt_raw.astype(jnp.float32) + dt_bias.astype(jnp.float32))
    A = -jnp.exp(a_log.astype(jnp.float32))
    chunk = SSD_CHUNK if l % SSD_CHUNK == 0 else l
    y, new_ssm = _ssd(xs, dt, A, Bm, Cm, ssm_init, chunk)
    y = y + d_skip[:, None] * xs
    y = _gated_group_rmsnorm(y.reshape(b, l, D_INNER), z, ssm_norm_w)
    p_ssm = jnp.einsum('bli,id->bld', y, w_ssm_branch)

    if k_buf is None:
        o = _banded_window_attention(q, k, v, rel_table, attn_sinks)
        new_k, new_v = k[:, -WINDOW:], v[:, -WINDOW:]
    else:
        kk = jnp.concatenate([k_buf.astype(k.dtype), k], axis=1)
        vv = jnp.concatenate([v_buf.astype(v.dtype), v], axis=1)
        o = _window_decode_attention(q, kk, vv, rel_table, attn_sinks)
        new_k, new_v = kk[:, -k_buf.shape[1]:], vv[:, -v_buf.shape[1]:]
    o = o.reshape(b, l, ATTN_WIDTH) * jax.nn.silu(g_attn)
    p_attn = jnp.einsum('bla,ad->bld', o, w_attn_branch)

    merged = jax.nn.sigmoid(mix_s) * p_ssm + jax.nn.sigmoid(mix_a) * p_attn
    x = x + jnp.einsum('bld,de->ble', merged, w_out)
    return x, new_conv, new_ssm, new_k, new_v


def setup_inputs(seed: int = 0) -> dict:
    key = jax.random.key(seed)
    ks = jax.random.split(key, 20)
    f32 = jnp.float32
    w_buf = min(WINDOW, PAST_LEN)

    def nrm(k, shape, scale):
        return scale * jax.random.normal(k, shape, f32)

    dt0 = jnp.exp(jax.random.uniform(ks[10], (DEPTH, SSM_HEADS), f32, math.log(1e-3), math.log(1e-1)))
    return {
        'x_prompt': nrm(ks[0], (BATCH, SEQ, D_MODEL), 1.0),
        'x_sample': nrm(ks[1], (DEC_BATCH, DEC_SEQ, D_MODEL), 1.0),
        'cache_k': nrm(ks[2], (DEPTH, DEC_BATCH, w_buf, ATTN_KV_HEADS, ATTN_HEAD_DIM), 1.0),
        'cache_v': nrm(ks[3], (DEPTH, DEC_BATCH, w_buf, ATTN_KV_HEADS, ATTN_HEAD_DIM), 1.0),
        'state_conv': nrm(ks[4], (DEPTH, DEC_BATCH, CONV_WIDTH - 1, CONV_DIM), 1.0),
        'state_ssm': nrm(ks[5], (DEPTH, DEC_BATCH, SSM_HEADS, SSM_HEAD_DIM, SSM_STATE), 0.1),
        'norm_w': 1.0 + nrm(ks[6], (DEPTH, D_MODEL), 0.02),
        'w_in': nrm(ks[7], (DEPTH, D_MODEL, IN_COLS), D_MODEL ** -0.5),
        'conv_w': nrm(ks[8], (DEPTH, CONV_WIDTH, CONV_DIM), CONV_WIDTH ** -0.5),
        'conv_b': nrm(ks[9], (DEPTH, CONV_DIM), 0.02),
        'dt_bias': dt0 + jnp.log(-jnp.expm1(-dt0)),
        'a_log': jnp.log(jax.random.uniform(ks[11], (DEPTH, SSM_HEADS), f32, 1.0, 16.0)),
        'd_skip': 1.0 + nrm(ks[12], (DEPTH, SSM_HEADS), 0.1),
        'ssm_norm_w': 1.0 + nrm(ks[13], (DEPTH, D_INNER), 0.02),
        'w_ssm_branch': nrm(ks[14], (DEPTH, D_INNER, D_MODEL), D_INNER ** -0.5),
        'attn_sinks': nrm(ks[15], (DEPTH, ATTN_Q_HEADS), 0.5),
        'w_attn_branch': nrm(ks[16], (DEPTH, ATTN_WIDTH, D_MODEL), ATTN_WIDTH ** -0.5),
        'w_out': nrm(ks[17], (DEPTH, D_MODEL, D_MODEL), D_MODEL ** -0.5),
        'rel_bias': nrm(ks[18], (N_BUCKETS, ATTN_Q_HEADS), 0.5),
        'final_norm_w': 1.0 + nrm(ks[19], (D_MODEL,), 0.02),
    }


def reference(x_prompt, x_sample, cache_k, cache_v, state_conv, state_ssm, norm_w, w_in, conv_w, conv_b,
              dt_bias, a_log, d_skip, ssm_norm_w, w_ssm_branch, attn_sinks, w_attn_branch, w_out,
              rel_bias, final_norm_w):
    xp, xs = x_prompt, x_sample
    kp_l, vp_l, cp_l, sp_l = [], [], [], []
    ks_l, vs_l, cs_l, ss_l = [], [], [], []
    for layer in range(DEPTH):
        weights = (norm_w[layer], w_in[layer], conv_w[layer], conv_b[layer], dt_bias[layer], a_log[layer],
                   d_skip[layer], ssm_norm_w[layer], w_ssm_branch[layer], attn_sinks[layer],
                   w_attn_branch[layer], w_out[layer], rel_bias)
        conv0 = jnp.zeros((xp.shape[0], CONV_WIDTH - 1, CONV_DIM), xp.dtype)
        ssm0 = jnp.zeros((xp.shape[0], SSM_HEADS, SSM_HEAD_DIM, SSM_STATE), xp.dtype)
        xp, c_p, s_p, k_p, v_p = _layer(xp, conv0, ssm0, None, None, *weights)
        xs, c_s, s_s, k_s, v_s = _layer(xs, state_conv[layer], state_ssm[layer], cache_k[layer],
                                        cache_v[layer], *weights)
        kp_l.append(k_p); vp_l.append(v_p); cp_l.append(c_p); sp_l.append(s_p)
        ks_l.append(k_s); vs_l.append(v_s); cs_l.append(c_s); ss_l.append(s_s)
    y_prompt = _rmsnorm(xp, final_norm_w)
    y_sample = _rmsnorm(xs, final_norm_w)
    return (y_prompt, y_sample, jnp.stack(kp_l), jnp.stack(vp_l), jnp.stack(cp_l), jnp.stack(sp_l),
            jnp.stack(ks_l), jnp.stack(vs_l), jnp.stack(cs_l), jnp.stack(ss_l))
```

```python
import functools
import math

import numpy as np
import jax
import jax.numpy as jnp
from jax import lax
from jax.experimental import pallas as pl
from jax.experimental.pallas import tpu as pltpu

F32 = jnp.float32
BF16 = jnp.bfloat16
HIGHEST = lax.Precision.HIGHEST

D_MODEL = 2048
D_INNER = 4096
SSM_HEAD_DIM = 64
SSM_HEADS = 64
SSM_GROUPS = 8
SSM_STATE = 128
CONV_WIDTH = 4
GN = SSM_GROUPS * SSM_STATE
CONV_DIM = D_INNER + 2 * GN
SSD_CHUNK = 128
HEAD_DIM = 64
Q_HEADS = 32
KV_HEADS = 8
REP = Q_HEADS // KV_HEADS
ATTN_WIDTH = Q_HEADS * HEAD_DIM
KV_WIDTH = KV_HEADS * HEAD_DIM
WINDOW = 128
N_BUCKETS = 32
MAX_EXACT = N_BUCKETS // 2
RMS_EPS = 1e-6

_OFF_Z = 0
_OFF_XBC = _OFF_Z + D_INNER
_OFF_DT = _OFF_XBC + CONV_DIM
_OFF_Q = _OFF_DT + SSM_HEADS
_OFF_K = _OFF_Q + ATTN_WIDTH
_OFF_V = _OFF_K + KV_WIDTH
_OFF_GA = _OFF_V + KV_WIDTH
_OFF_MS = _OFF_GA + ATTN_WIDTH
_OFF_MA = _OFF_MS + D_MODEL
_IN_COLS = _OFF_MA + D_MODEL

LANES = 128
SUBLANES = 8
DT_PAD = LANES


class _COL:
    Z = 0
    XS = Z + D_INNER
    BC = XS + D_INNER
    Q = BC + 2 * GN
    GA = Q + ATTN_WIDTH
    MS = GA + ATTN_WIDTH
    MA = MS + D_MODEL
    K = MA + D_MODEL
    V = K + KV_WIDTH
    DT = V + KV_WIDTH
    TOTAL = DT + DT_PAD


IN_TN = 1152
assert _COL.TOTAL % IN_TN == 0
VMEM_LIMIT = 52 * 1024 * 1024


def _cparams(n_grid):
    return pltpu.CompilerParams(dimension_semantics=("arbitrary",) * n_grid, vmem_limit_bytes=VMEM_LIMIT)


def _silu(v):
    return v * (1.0 / (1.0 + jnp.exp(-v)))


def _sigmoid(v):
    return 1.0 / (1.0 + jnp.exp(-v))


def _in_proj_kernel(x_ref, nw_ref, w_ref, o_ref, h_ref):
    @pl.when(pl.program_id(1) == 0)
    def _():
        rows = 256
        for r in range(0, x_ref.shape[0], rows):
            xf = x_ref[r:r + rows, :]
            ms = jnp.mean(xf * xf, axis=-1, keepdims=True)
            h_ref[r:r + rows, :] = (xf * lax.rsqrt(ms + RMS_EPS) * nw_ref[...]).astype(BF16)

    o_ref[...] = jnp.dot(h_ref[...], w_ref[...], preferred_element_type=F32)


def _in_proj(x2d, norm_w, w_perm, tm):
    t = x2d.shape[0]
    return pl.pallas_call(
        _in_proj_kernel,
        grid=(t // tm, _COL.TOTAL // IN_TN),
        in_specs=[pl.BlockSpec((tm, D_MODEL), lambda i, j: (i, 0)),
                  pl.BlockSpec((1, D_MODEL), lambda i, j: (0, 0)),
                  pl.BlockSpec((D_MODEL, IN_TN), lambda i, j: (0, j))],
        out_specs=pl.BlockSpec((tm, IN_TN), lambda i, j: (i, j)),
        out_shape=jax.ShapeDtypeStruct((t, _COL.TOTAL), F32),
        scratch_shapes=[pltpu.VMEM((tm, D_MODEL), BF16)],
        compiler_params=_cparams(2),
        name="in_proj",
    )(x2d, norm_w, w_perm)


def _bucket_matrix():
    qi = np.arange(WINDOW)[:, None]
    kj = np.arange(2 * WINDOW)[None, :]
    dist = qi + WINDOW - kj
    n = np.maximum(dist, 0)
    nf = np.maximum(n, 1).astype(np.float32)
    large = MAX_EXACT + (np.log(nf / MAX_EXACT) / math.log(WINDOW / MAX_EXACT)
                         * (N_BUCKETS - MAX_EXACT)).astype(np.int32)
    large = np.minimum(large, N_BUCKETS - 1)
    bucket = np.where(n < MAX_EXACT, n, large)
    return np.where((dist >= 0) & (dist <= WINDOW), bucket, -1).astype(np.int32)


def _bias_kernel(table_ref, bucket_ref, o_ref):
    h = pl.program_id(0)
    bk = bucket_ref[...]
    acc = jnp.full(bk.shape, -jnp.inf, F32)
    for b in range(N_BUCKETS):
        acc = jnp.where(bk == b, table_ref[b, h], acc)
    o_ref[0] = acc


def _band_bias(rel_bias):
    bucket = jnp.asarray(_bucket_matrix())
    return pl.pallas_call(
        _bias_kernel,
        grid=(Q_HEADS,),
        in_specs=[pl.BlockSpec(memory_space=pltpu.SMEM),
                  pl.BlockSpec((WINDOW, 2 * WINDOW), lambda h: (0, 0))],
        out_specs=pl.BlockSpec((1, WINDOW, 2 * WINDOW), lambda h: (h, 0, 0)),
        out_shape=jax.ShapeDtypeStruct((Q_HEADS, WINDOW, 2 * WINDOW), F32),
        compiler_params=_cparams(1),
        name="band_bias",
    )(rel_bias, bucket)


CONV_STRIP = 512


def _conv_silu(ext_ref, base, rows, cw_ref, cb_ref, xc_ref, out_row):
    for c0 in range(0, CONV_DIM, CONV_STRIP):
        cs = slice(c0, c0 + CONV_STRIP)
        acc = cb_ref[:, cs] + cw_ref[3:4, cs] * ext_ref[base:base + rows, cs]
        for k in range(1, CONV_WIDTH):
            acc = acc + cw_ref[3 - k:4 - k, cs] * ext_ref[base - k:base - k + rows, cs]
        xc_ref[out_row:out_row + rows, cs] = _silu(acc)


def _ssd_core(rows, same_seq, z_ref, dt_ref, dtb_ref, alog_ref, dskip_ref, nw_ref, xc_ref, xd_ref, y_ref,
              yoff_fn, state_fn, g_ref):
    li = lax.broadcasted_iota(jnp.int32, (rows, rows), 0)
    si = lax.broadcasted_iota(jnp.int32, (rows, rows), 1)
    causal = same_seq & (si <= li)
    lane = lax.broadcasted_iota(jnp.int32, (rows, LANES), 1)
    first_half = lane < SSM_HEAD_DIM

    dt = jax.nn.softplus(dt_ref[...] + dtb_ref[...])
    a_neg = -jnp.exp(alog_ref[...])
    d_a = dt * a_neg
    acs = jnp.dot(causal.astype(F32), d_a, precision=HIGHEST, preferred_element_type=F32)
    a_end = jnp.dot(same_seq.astype(F32), d_a, precision=HIGHEST, preferred_element_type=F32)
    acs_t = acs.T
    dt_t = dt.T
    w_in_state = dt * jnp.exp(a_end - acs)
    e_acs = jnp.exp(acs)
    cd_rows = jnp.exp(a_end)

    def pair_expand(m, j):
        a = jnp.broadcast_to(m[:, 2 * j:2 * j + 1], (rows, LANES))
        b = jnp.broadcast_to(m[:, 2 * j + 1:2 * j + 2], (rows, LANES))
        return jnp.where(first_half, a, b)

    pairs_per_group = SSM_HEADS // SSM_GROUPS // 2
    for g in range(SSM_GROUPS):
        b_g = xc_ref[:, D_INNER + g * SSM_STATE:D_INNER + (g + 1) * SSM_STATE].astype(BF16)
        c_g = xc_ref[:, D_INNER + GN + g * SSM_STATE:D_INNER + GN + (g + 1) * SSM_STATE].astype(BF16)
        cb = lax.dot_general(c_g, b_g, (((1,), (1,)), ((), ())), preferred_element_type=F32)
        yoff = yoff_fn(g, c_g)
        for jj in range(pairs_per_group):
            j = g * pairs_per_group + jj
            ps = slice(j * LANES, (j + 1) * LANES)
            xs_pair = xc_ref[:, ps]
            yd = None
            for hh in range(2):
                h = 2 * j + hh
                seg = acs[:, h:h + 1] - acs_t[h:h + 1, :]
                decay = jnp.exp(jnp.where(causal, seg, -jnp.inf))
                m = (cb * decay * dt_t[h:h + 1, :]).astype(BF16)
                x_h = jnp.where(first_half if hh == 0 else ~first_half, xs_pair, 0.0).astype(BF16)
                part = jnp.dot(m, x_h, preferred_element_type=F32)
                yd = part if yd is None else yd + part
            xd_ref[:, ps] = (xs_pair * pair_expand(w_in_state, j)).astype(xd_ref.dtype)
            y_ref[:, ps] = (yd + yoff[:, jj * LANES:(jj + 1) * LANES] * pair_expand(e_acs, j)
                            + dskip_ref[:, ps] * xs_pair)
        state_fn(g, cd_rows)

    group_w = D_INNER // SSM_GROUPS
    for g in range(SSM_GROUPS):
        gs = slice(g * group_w, (g + 1) * group_w)
        gg = y_ref[:, gs] * _silu(z_ref[:, gs])
        ms = jnp.mean(gg * gg, axis=-1, keepdims=True)
        g_ref[:, gs] = (gg * lax.rsqrt(ms + RMS_EPS) * nw_ref[:, gs]).astype(BF16)


def _ssd_prompt_kernel(z_ref, xs_ref, bc_ref, dt_ref, cw_ref, cb_ref, dtb_ref, alog_ref, dskip_ref, nw_ref,
                       g_ref, st_ref, ext_ref, xc_ref, xd_ref, y_ref, stt_ref):
    c = pl.program_id(1)
    rows = SSD_CHUNK

    @pl.when(c == 0)
    def _():
        ext_ref[0:SUBLANES, :] = jnp.zeros((SUBLANES, CONV_DIM), F32)
        stt_ref[...] = jnp.zeros(stt_ref.shape, F32)

    @pl.when(c != 0)
    def _():
        ext_ref[0:SUBLANES, :] = ext_ref[rows:rows + SUBLANES, :]

    ext_ref[SUBLANES:SUBLANES + rows, 0:D_INNER] = xs_ref[...]
    ext_ref[SUBLANES:SUBLANES + rows, D_INNER:CONV_DIM] = bc_ref[...]
    _conv_silu(ext_ref, SUBLANES, rows, cw_ref, cb_ref, xc_ref, 0)

    def yoff_fn(g, c_g):
        return jnp.dot(c_g, stt_ref[:, g * 512:(g + 1) * 512].astype(BF16), preferred_element_type=F32)

    def state_fn(g, cd_rows):
        b_g = xc_ref[:, D_INNER + g * SSM_STATE:D_INNER + (g + 1) * SSM_STATE].astype(BF16)
        xd_g = xd_ref[:, g * 512:(g + 1) * 512]
        new = lax.dot_general(b_g, xd_g, (((0,), (0,)), ((), ())), preferred_element_type=F32)
        lane = lax.broadcasted_iota(jnp.int32, (SSM_STATE, LANES), 1)
        for jj in range(4):
            j = g * 4 + jj
            ps = slice(j * LANES, (j + 1) * LANES)
            cd = jnp.where(lane < SSM_HEAD_DIM,
                           jnp.broadcast_to(cd_rows[0:1, 2 * j:2 * j + 1], (SSM_STATE, LANES)),
                           jnp.broadcast_to(cd_rows[0:1, 2 * j + 1:2 * j + 2], (SSM_STATE, LANES)))
            stt_ref[:, ps] = stt_ref[:, ps] * cd + new[:, jj * LANES:(jj + 1) * LANES]

    same_seq = jnp.full((rows, rows), True)
    _ssd_core(rows, same_seq, z_ref, dt_ref, dtb_ref, alog_ref, dskip_ref, nw_ref, xc_ref, xd_ref, y_ref,
              yoff_fn, state_fn, g_ref)

    @pl.when(c == pl.num_programs(1) - 1)
    def _():
        for j in range(D_INNER // LANES):
            st_ref[0, j * LANES:(j + 1) * LANES, :] = stt_ref[:, j * LANES:(j + 1) * LANES].T


def _ssd_prompt(u, conv_w, conv_b, dt_bias, a_log, dskip_cols, ssm_norm_w, batch, seq):
    nc = seq // SSD_CHUNK
    rows = SSD_CHUNK
    row_blk = lambda b, c: b * nc + c
    full = lambda shape: pl.BlockSpec(shape, lambda b, c: (0,) * len(shape))
    return pl.pallas_call(
        _ssd_prompt_kernel,
        grid=(batch, nc),
        in_specs=[pl.BlockSpec((rows, D_INNER), lambda b, c: (row_blk(b, c), _COL.Z // D_INNER)),
                  pl.BlockSpec((rows, D_INNER), lambda b, c: (row_blk(b, c), _COL.XS // D_INNER)),
                  pl.BlockSpec((rows, 2 * GN), lambda b, c: (row_blk(b, c), _COL.BC // (2 * GN))),
                  pl.BlockSpec((rows, DT_PAD), lambda b, c: (row_blk(b, c), _COL.DT // DT_PAD)),
                  full((CONV_WIDTH, CONV_DIM)), full((1, CONV_DIM)), full((1, DT_PAD)), full((1, DT_PAD)),
                  full((1, D_INNER)), full((1, D_INNER))],
        out_specs=[pl.BlockSpec((rows, D_INNER), lambda b, c: (row_blk(b, c), 0)),
                   pl.BlockSpec((1, D_INNER, SSM_STATE), lambda b, c: (b, 0, 0))],
        out_shape=[jax.ShapeDtypeStruct((batch * seq, D_INNER), BF16),
                   jax.ShapeDtypeStruct((batch, D_INNER, SSM_STATE), F32)],
        scratch_shapes=[pltpu.VMEM((rows + SUBLANES, CONV_DIM), F32),
                        pltpu.VMEM((rows, CONV_DIM), F32),
                        pltpu.VMEM((rows, D_INNER), BF16),
                        pltpu.VMEM((rows, D_INNER), F32),
                        pltpu.VMEM((SSM_STATE, D_INNER), F32)],
        compiler_params=_cparams(2),
        name="ssd_prompt",
    )(u, u, u, u, conv_w, conv_b, dt_bias, a_log, dskip_cols, ssm_norm_w)


SAMPLE_BT = 4


def _ssd_sample_kernel(z_ref, xs_ref, bc_ref, dt_ref, sc_ref, st_ref, cw_ref, cb_ref, dtb_ref, alog_ref, dskip_ref,
                       nw_ref, g_ref, nst_ref, ext_ref, xc_ref, xd_ref, y_ref, yoff_ref, seq_len):
    bt = SAMPLE_BT
    rows = bt * seq_len
    assert seq_len == SUBLANES
    for b in range(bt):
        base = 2 * SUBLANES * b
        ext_ref[base:base + SUBLANES, :] = sc_ref[b * SUBLANES:(b + 1) * SUBLANES, :]
        ext_ref[base + SUBLANES:base + 2 * SUBLANES, 0:D_INNER] = xs_ref[b * seq_len:(b + 1) * seq_len, :]
        ext_ref[base + SUBLANES:base + 2 * SUBLANES, D_INNER:CONV_DIM] = bc_ref[b * seq_len:(b + 1) * seq_len, :]
        _conv_silu(ext_ref, base + SUBLANES, seq_len, cw_ref, cb_ref, xc_ref, b * seq_len)

    for b in range(bt):
        rs = slice(b * seq_len, (b + 1) * seq_len)
        for g in range(SSM_GROUPS):
            c_bg = xc_ref[rs, D_INNER + GN + g * SSM_STATE:D_INNER + GN + (g + 1) * SSM_STATE].astype(BF16)
            s_bg = st_ref[b, g * 512:(g + 1) * 512, :].astype(BF16)
            yoff_ref[rs, g * 512:(g + 1) * 512] = lax.dot_general(
                c_bg, s_bg, (((1,), (1,)), ((), ())), preferred_element_type=F32)

    def yoff_fn(g, c_g):
        return yoff_ref[:, g * 512:(g + 1) * 512]

    def state_fn(g, cd_rows):
        for b in range(bt):
            rs = slice(b * seq_len, (b + 1) * seq_len)
            b_bg = xc_ref[rs, D_INNER + g * SSM_STATE:D_INNER + (g + 1) * SSM_STATE].astype(BF16)
            xd_bg = xd_ref[rs, g * 512:(g + 1) * 512].astype(BF16)
            new = lax.dot_general(xd_bg, b_bg, (((0,), (0,)), ((), ())), preferred_element_type=F32)
            for hh in range(8):
                h = g * 8 + hh
                hs = slice(h * SSM_HEAD_DIM, (h + 1) * SSM_HEAD_DIM)
                cd = jnp.broadcast_to(cd_rows[b * seq_len:b * seq_len + 1, h:h + 1], (SSM_HEAD_DIM, SSM_STATE))
                nst_ref[b, hs, :] = st_ref[b, hs, :] * cd + new[hh * SSM_HEAD_DIM:(hh + 1) * SSM_HEAD_DIM, :]

    li = lax.broadcasted_iota(jnp.int32, (rows, rows), 0)
    si = lax.broadcasted_iota(jnp.int32, (rows, rows), 1)
    same_seq = (li // seq_len) == (si // seq_len)
    _ssd_core(rows, same_seq, z_ref, dt_ref, dtb_ref, alog_ref, dskip_ref, nw_ref, xc_ref, xd_ref, y_ref,
              yoff_fn, state_fn, g_ref)


def _ssd_sample(u, sc8, state, conv_w, conv_b, dt_bias, a_log, dskip_cols, ssm_norm_w, batch, seq):
    bt = SAMPLE_BT
    rows = bt * seq
    full = lambda shape: pl.BlockSpec(shape, lambda i: (0,) * len(shape))
    return pl.pallas_call(
        functools.partial(_ssd_sample_kernel, seq_len=seq),
        grid=(batch // bt,),
        in_specs=[pl.BlockSpec((rows, D_INNER), lambda i: (i, _COL.Z // D_INNER)),
                  pl.BlockSpec((rows, D_INNER), lambda i: (i, _COL.XS // D_INNER)),
                  pl.BlockSpec((rows, 2 * GN), lambda i: (i, _COL.BC // (2 * GN))),
                  pl.BlockSpec((rows, DT_PAD), lambda i: (i, _COL.DT // DT_PAD)),
                  pl.BlockSpec((bt * SUBLANES, CONV_DIM), lambda i: (i, 0)),
                  pl.BlockSpec((bt, D_INNER, SSM_STATE), lambda i: (i, 0, 0)),
                  full((CONV_WIDTH, CONV_DIM)), full((1, CONV_DIM)), full((1, DT_PAD)), full((1, DT_PAD)),
                  full((1, D_INNER)), full((1, D_INNER))],
        out_specs=[pl.BlockSpec((rows, D_INNER), lambda i: (i, 0)),
                   pl.BlockSpec((bt, D_INNER, SSM_STATE), lambda i: (i, 0, 0))],
        out_shape=[jax.ShapeDtypeStruct((batch * seq, D_INNER), BF16),
                   jax.ShapeDtypeStruct((batch, D_INNER, SSM_STATE), F32)],
        scratch_shapes=[pltpu.VMEM((bt * 2 * SUBLANES, CONV_DIM), F32),
                        pltpu.VMEM((rows, CONV_DIM), F32),
                        pltpu.VMEM((rows, D_INNER), F32),
                        pltpu.VMEM((rows, D_INNER), F32),
                        pltpu.VMEM((rows, D_INNER), F32)],
        compiler_params=_cparams(1),
        name="ssd_sample",
    )(u, u, u, u, sc8, state, conv_w, conv_b, dt_bias, a_log, dskip_cols, ssm_norm_w)


ATTN_SCALE = HEAD_DIM ** -0.5


def _qk(q_bf16, k_bf16):
    return lax.dot_general(q_bf16, k_bf16, (((1,), (1,)), ((), ())), preferred_element_type=F32)


def _softmax_sink_pv(parts, sink):
    m = sink
    for s, _ in parts:
        m = jnp.maximum(m, jnp.max(s, axis=-1, keepdims=True))
    denom = jnp.exp(sink - m)
    o = None
    for s, v in parts:
        p = jnp.exp(s - m)
        denom = denom + jnp.sum(p, axis=-1, keepdims=True)
        pv = jnp.dot(p.astype(BF16), v, preferred_element_type=F32)
        o = pv if o is None else o + pv
    return o / denom


def _attn_prompt_kernel(sink_ref, q_ref, kp_ref, ko_ref, vp_ref, vo_ref, ga_ref, bias_ref, o_ref):
    n = pl.program_id(1)
    no_prev = jnp.where(n == 0, -jnp.inf, 0.0)
    for g in range(KV_HEADS):
        ks = slice(g * HEAD_DIM, (g + 1) * HEAD_DIM)
        k_prev, k_own = kp_ref[:, ks].astype(BF16), ko_ref[:, ks].astype(BF16)
        v_prev, v_own = vp_ref[:, ks].astype(BF16), vo_ref[:, ks].astype(BF16)
        for r in range(REP):
            h = g * REP + r
            hs = slice(h * HEAD_DIM, (h + 1) * HEAD_DIM)
            q = q_ref[:, hs].astype(BF16)
            s_prev = _qk(q, k_prev) * ATTN_SCALE + (bias_ref[h, :, 0:WINDOW] + no_prev)
            s_own = _qk(q, k_own) * ATTN_SCALE + bias_ref[h, :, WINDOW:2 * WINDOW]
            sink = jnp.full((WINDOW, 1), sink_ref[h], F32)
            o = _softmax_sink_pv([(s_prev, v_prev), (s_own, v_own)], sink)
            o_ref[:, hs] = (o * _silu(ga_ref[:, hs])).astype(BF16)


def _attn_prompt(u, sinks, bias, batch, seq):
    nb = seq // WINDOW
    row = lambda b, n: b * nb + n
    prev = lambda b, n: jnp.maximum(b * nb + n - 1, 0)
    return pl.pallas_call(
        _attn_prompt_kernel,
        grid=(batch, nb),
        in_specs=[pl.BlockSpec(memory_space=pltpu.SMEM),
                  pl.BlockSpec((WINDOW, ATTN_WIDTH), lambda b, n: (row(b, n), _COL.Q // ATTN_WIDTH)),
                  pl.BlockSpec((WINDOW, KV_WIDTH), lambda b, n: (prev(b, n), _COL.K // KV_WIDTH)),
                  pl.BlockSpec((WINDOW, KV_WIDTH), lambda b, n: (row(b, n), _COL.K // KV_WIDTH)),
                  pl.BlockSpec((WINDOW, KV_WIDTH), lambda b, n: (prev(b, n), _COL.V // KV_WIDTH)),
                  pl.BlockSpec((WINDOW, KV_WIDTH), lambda b, n: (row(b, n), _COL.V // KV_WIDTH)),
                  pl.BlockSpec((WINDOW, ATTN_WIDTH), lambda b, n: (row(b, n), _COL.GA // ATTN_WIDTH)),
                  pl.BlockSpec((Q_HEADS, WINDOW, 2 * WINDOW), lambda b, n: (0, 0, 0))],
        out_specs=pl.BlockSpec((WINDOW, ATTN_WIDTH), lambda b, n: (row(b, n), 0)),
        out_shape=jax.ShapeDtypeStruct((batch * seq, ATTN_WIDTH), BF16),
        compiler_params=_cparams(2),
        name="attn_prompt",
    )(sinks, u, u, u, u, u, u, bias)


def _attn_sample_kernel(sink_ref, q_ref, kn_ref, vn_ref, ga_ref, ck_ref, cv_ref, bc_ref, bn_ref,
                        o_ref, nk_ref, nv_ref, seq_len):
    bt = SAMPLE_BT
    w_buf = ck_ref.shape[1]
    keep = w_buf - seq_len
    for b in range(bt):
        rs = slice(b * seq_len, (b + 1) * seq_len)
        nk_ref[b, 0:keep, :] = ck_ref[b, seq_len:w_buf, :]
        nk_ref[b, keep:w_buf, :] = kn_ref[rs, :]
        nv_ref[b, 0:keep, :] = cv_ref[b, seq_len:w_buf, :]
        nv_ref[b, keep:w_buf, :] = vn_ref[rs, :]
    for g in range(KV_HEADS):
        ks = slice(g * HEAD_DIM, (g + 1) * HEAD_DIM)
        kc = jnp.concatenate([ck_ref[b, :, ks] for b in range(bt)], axis=0).astype(BF16)
        vc = jnp.concatenate([cv_ref[b, :, ks] for b in range(bt)], axis=0).astype(BF16)
        kn = kn_ref[:, ks].astype(BF16)
        vn = vn_ref[:, ks].astype(BF16)
        qs = jnp.concatenate([q_ref[:, (g * REP + r) * HEAD_DIM:(g * REP + r + 1) * HEAD_DIM] for r in range(REP)],
                             axis=0).astype(BF16)
        s_cache = _qk(qs, kc) * ATTN_SCALE + bc_ref[g]
        s_new = _qk(qs, kn) * ATTN_SCALE + bn_ref[g]
        rows = bt * seq_len
        sink = jnp.concatenate([jnp.full((rows, 1), sink_ref[g * REP + r], F32) for r in range(REP)], axis=0)
        o = _softmax_sink_pv([(s_cache, vc), (s_new, vn)], sink)
        for r in range(REP):
            hs = slice((g * REP + r) * HEAD_DIM, (g * REP + r + 1) * HEAD_DIM)
            o_ref[:, hs] = (o[r * rows:(r + 1) * rows, :] * _silu(ga_ref[:, hs])).astype(BF16)


def _decode_bias(band_bias, seq, w_buf):
    bt = SAMPLE_BT
    bd = band_bias[:, :seq, :].reshape(KV_HEADS, REP, 1, seq, 1, 2 * WINDOW)
    same = (jnp.arange(bt)[:, None, None, None] == jnp.arange(bt)[None, None, :, None])
    full = jnp.where(same[None, None], bd, -jnp.inf)
    cache = full[..., WINDOW - w_buf:WINDOW].reshape(KV_HEADS, REP * bt * seq, bt * w_buf)
    new = full[..., WINDOW:WINDOW + seq].reshape(KV_HEADS, REP * bt * seq, bt * seq)
    return cache, new


def _attn_sample(u, sinks, cache_k, cache_v, bias_c, bias_n, batch, seq):
    bt = SAMPLE_BT
    rows = bt * seq
    w_buf = cache_k.shape[1]
    return pl.pallas_call(
        functools.partial(_attn_sample_kernel, seq_len=seq),
        grid=(batch // bt,),
        in_specs=[pl.BlockSpec(memory_space=pltpu.SMEM),
                  pl.BlockSpec((rows, ATTN_WIDTH), lambda i: (i, _COL.Q // ATTN_WIDTH)),
                  pl.BlockSpec((rows, KV_WIDTH), lambda i: (i, _COL.K // KV_WIDTH)),
                  pl.BlockSpec((rows, KV_WIDTH), lambda i: (i, _COL.V // KV_WIDTH)),
                  pl.BlockSpec((rows, ATTN_WIDTH), lambda i: (i, _COL.GA // ATTN_WIDTH)),
                  pl.BlockSpec((bt, w_buf, KV_WIDTH), lambda i: (i, 0, 0)),
                  pl.BlockSpec((bt, w_buf, KV_WIDTH), lambda i: (i, 0, 0)),
                  pl.BlockSpec(bias_c.shape, lambda i: (0, 0, 0)),
                  pl.BlockSpec(bias_n.shape, lambda i: (0, 0, 0))],
        out_specs=[pl.BlockSpec((rows, ATTN_WIDTH), lambda i: (i, 0)),
                   pl.BlockSpec((bt, w_buf, KV_WIDTH), lambda i: (i, 0, 0)),
                   pl.BlockSpec((bt, w_buf, KV_WIDTH), lambda i: (i, 0, 0))],
        out_shape=[jax.ShapeDtypeStruct((batch * seq, ATTN_WIDTH), BF16),
                   jax.ShapeDtypeStruct(cache_k.shape, F32),
                   jax.ShapeDtypeStruct(cache_v.shape, F32)],
        compiler_params=_cparams(1),
        name="attn_sample",
    )(sinks, u, u, u, u, cache_k, cache_v, bias_c, bias_n)


MERGE_TN = 256


def _merge_kernel(g_ref, og_ref, ms_ref, ma_ref, ws_ref, wa_ref, o_ref):
    p_ssm = jnp.dot(g_ref[...], ws_ref[...], preferred_element_type=F32)
    p_attn = jnp.dot(og_ref[...], wa_ref[...], preferred_element_type=F32)
    o_ref[...] = (_sigmoid(ms_ref[...]) * p_ssm + _sigmoid(ma_ref[...]) * p_attn).astype(BF16)


def _merge(g, og, u, w_ssm, w_attn, tm):
    t = g.shape[0]
    tn = MERGE_TN
    return pl.pallas_call(
        _merge_kernel,
        grid=(t // tm, D_MODEL // tn),
        in_specs=[pl.BlockSpec((tm, D_INNER), lambda i, j: (i, 0)),
                  pl.BlockSpec((tm, ATTN_WIDTH), lambda i, j: (i, 0)),
                  pl.BlockSpec((tm, tn), lambda i, j: (i, _COL.MS // tn + j)),
                  pl.BlockSpec((tm, tn), lambda i, j: (i, _COL.MA // tn + j)),
                  pl.BlockSpec((D_INNER, tn), lambda i, j: (0, j)),
                  pl.BlockSpec((ATTN_WIDTH, tn), lambda i, j: (0, j))],
        out_specs=pl.BlockSpec((tm, tn), lambda i, j: (i, j)),
        out_shape=jax.ShapeDtypeStruct((t, D_MODEL), BF16),
        compiler_params=_cparams(2),
        name="merge",
    )(g, og, u, u, w_ssm, w_attn)


def _out_proj_kernel(m_ref, x_ref, wo_ref, fw_ref, o_ref):
    xn = x_ref[...] + jnp.dot(m_ref[...], wo_ref[...], preferred_element_type=F32)
    ms = jnp.mean(xn * xn, axis=-1, keepdims=True)
    o_ref[...] = xn * lax.rsqrt(ms + RMS_EPS) * fw_ref[...]


def _out_proj(m, x2d, w_out, final_w, tm):
    t = m.shape[0]
    return pl.pallas_call(
        _out_proj_kernel,
        grid=(t // tm,),
        in_specs=[pl.BlockSpec((tm, D_MODEL), lambda i: (i, 0)),
                  pl.BlockSpec((tm, D_MODEL), lambda i: (i, 0)),
                  pl.BlockSpec((D_MODEL, D_MODEL), lambda i: (0, 0)),
                  pl.BlockSpec((1, D_MODEL), lambda i: (0, 0))],
        out_specs=pl.BlockSpec((tm, D_MODEL), lambda i: (i, 0)),
        out_shape=jax.ShapeDtypeStruct((t, D_MODEL), F32),
        compiler_params=_cparams(1),
        name="out_proj",
    )(m, x2d, w_out, final_w)


def _regroup_w_in(w):
    parts = [w[:, _OFF_Z:_OFF_DT], w[:, _OFF_Q:_OFF_K], w[:, _OFF_GA:_OFF_MS], w[:, _OFF_MS:_OFF_MA],
             w[:, _OFF_MA:_IN_COLS], w[:, _OFF_K:_OFF_V], w[:, _OFF_V:_OFF_GA], w[:, _OFF_DT:_OFF_Q],
             jnp.zeros((w.shape[0], DT_PAD - SSM_HEADS), w.dtype)]
    return jnp.concatenate(parts, axis=1).astype(BF16)


def _pad_lanes(v):
    return jnp.pad(v.reshape(1, -1), ((0, 0), (0, DT_PAD - v.shape[-1])))


def kernel(x_prompt, x_sample, cache_k, cache_v, state_conv, state_ssm, norm_w, w_in, conv_w, conv_b, dt_bias, a_log,
           d_skip, ssm_norm_w, w_ssm_branch, attn_sinks, w_attn_branch, w_out, rel_bias, final_norm_w):
    assert w_in.shape[0] == 1, "single-layer kernel"
    batch, seq, _ = x_prompt.shape
    dec_batch, dec_seq, _ = x_sample.shape
    w_buf = cache_k.shape[2]
    assert seq % SSD_CHUNK == 0 and seq % WINDOW == 0 and dec_seq == SUBLANES and w_buf == WINDOW
    assert dec_batch % SAMPLE_BT == 0

    w_perm = _regroup_w_in(w_in[0])
    nw = norm_w[0].reshape(1, D_MODEL)
    cw, cb = conv_w[0], conv_b[0].reshape(1, CONV_DIM)
    dtb, alog = _pad_lanes(dt_bias[0]), _pad_lanes(a_log[0])
    dskip_cols = jnp.repeat(d_skip[0], SSM_HEAD_DIM).reshape(1, D_INNER)
    snw = ssm_norm_w[0].reshape(1, D_INNER)
    w_ssm = w_ssm_branch[0].astype(BF16)
    w_attn = w_attn_branch[0].astype(BF16)
    wo = w_out[0].astype(BF16)
    fw = final_norm_w.reshape(1, D_MODEL)
    sinks = attn_sinks[0]

    xp2 = x_prompt.reshape(batch * seq, D_MODEL)
    xs2 = x_sample.reshape(dec_batch * dec_seq, D_MODEL)

    band_bias = _band_bias(rel_bias)
    bias_c, bias_n = _decode_bias(band_bias, dec_seq, w_buf)

    u_p = _in_proj(xp2, nw, w_perm, tm=1024)
    g_p, st_p = _ssd_prompt(u_p, cw, cb, dtb, alog, dskip_cols, snw, batch, seq)
    og_p = _attn_prompt(u_p, sinks, band_bias, batch, seq)
    m_p = _merge(g_p, og_p, u_p, w_ssm, w_attn, tm=1024)
    y_p = _out_proj(m_p, xp2, wo, fw, tm=512)

    u_s = _in_proj(xs2, nw, w_perm, tm=1024)
    sc8 = jnp.pad(state_conv[0], ((0, 0), (SUBLANES - (CONV_WIDTH - 1), 0), (0, 0))).reshape(
        dec_batch * SUBLANES, CONV_DIM)
    g_s, st_s = _ssd_sample(u_s, sc8, state_ssm[0].reshape(dec_batch, D_INNER, SSM_STATE), cw, cb, dtb, alog, dskip_cols, snw, dec_batch, dec_seq)
    og_s, nk_s, nv_s = _attn_sample(
        u_s, sinks, cache_k[0].reshape(dec_batch, w_buf, KV_WIDTH), cache_v[0].reshape(dec_batch, w_buf, KV_WIDTH),
        bias_c, bias_n, dec_batch, dec_seq)
    m_s = _merge(g_s, og_s, u_s, w_ssm, w_attn, tm=1024)
    y_s = _out_proj(m_s, xs2, wo, fw, tm=512)

    u_p3 = u_p.reshape(batch, seq, _COL.TOTAL)
    u_s3 = u_s.reshape(dec_batch, dec_seq, _COL.TOTAL)
    k_p = u_p3[:, seq - WINDOW:, _COL.K:_COL.K + KV_WIDTH].reshape(1, batch, WINDOW, KV_HEADS, HEAD_DIM)
    v_p = u_p3[:, seq - WINDOW:, _COL.V:_COL.V + KV_WIDTH].reshape(1, batch, WINDOW, KV_HEADS, HEAD_DIM)
    conv_p = u_p3[:, seq - (CONV_WIDTH - 1):, _COL.XS:_COL.XS + CONV_DIM][None]
    conv_s = u_s3[:, dec_seq - (CONV_WIDTH - 1):, _COL.XS:_COL.XS + CONV_DIM][None]
    ssm_p = st_p.reshape(1, batch, SSM_HEADS, SSM_HEAD_DIM, SSM_STATE)
    ssm_s = st_s.reshape(1, dec_batch, SSM_HEADS, SSM_HEAD_DIM, SSM_STATE)
    k_s = nk_s.reshape(1, dec_batch, w_buf, KV_HEADS, HEAD_DIM)
    v_s = nv_s.reshape(1, dec_batch, w_buf, KV_HEADS, HEAD_DIM)
    return (y_p.reshape(batch, seq, D_MODEL), y_s.reshape(dec_batch, dec_seq, D_MODEL),
            k_p, v_p, conv_p, ssm_p, k_s, v_s, conv_s, ssm_s)
```

```python
import functools
import math

import numpy as np
import jax
import jax.numpy as jnp
from jax import lax
from jax.experimental import pallas as pl
from jax.experimental.pallas import tpu as pltpu

F32 = jnp.float32
BF16 = jnp.bfloat16
HIGHEST = lax.Precision.HIGHEST

D_MODEL = 2048
D_INNER = 4096
SSM_HEAD_DIM = 64
SSM_HEADS = 64
SSM_GROUPS = 8
SSM_STATE = 128
CONV_WIDTH = 4
GN = SSM_GROUPS * SSM_STATE
CONV_DIM = D_INNER + 2 * GN
SSD_CHUNK = 128
HEAD_DIM = 64
Q_HEADS = 32
KV_HEADS = 8
REP = Q_HEADS // KV_HEADS
ATTN_WIDTH = Q_HEADS * HEAD_DIM
KV_WIDTH = KV_HEADS * HEAD_DIM
WINDOW = 128
N_BUCKETS = 32
MAX_EXACT = N_BUCKETS // 2
RMS_EPS = 1e-6

_OFF_Z = 0
_OFF_XBC = _OFF_Z + D_INNER
_OFF_DT = _OFF_XBC + CONV_DIM
_OFF_Q = _OFF_DT + SSM_HEADS
_OFF_K = _OFF_Q + ATTN_WIDTH
_OFF_V = _OFF_K + KV_WIDTH
_OFF_GA = _OFF_V + KV_WIDTH
_OFF_MS = _OFF_GA + ATTN_WIDTH
_OFF_MA = _OFF_MS + D_MODEL
_IN_COLS = _OFF_MA + D_MODEL

LANES = 128
SUBLANES = 8
DT_PAD = LANES


class _COL:
    Z = 0
    XS = Z + D_INNER
    BC = XS + D_INNER
    Q = BC + 2 * GN
    GA = Q + ATTN_WIDTH
    MS = GA + ATTN_WIDTH
    MA = MS + D_MODEL
    K = MA + D_MODEL
    V = K + KV_WIDTH
    DT = V + KV_WIDTH
    TOTAL = DT + DT_PAD


IN_TN = 1152
assert _COL.TOTAL % IN_TN == 0
VMEM_LIMIT = 52 * 1024 * 1024


def _cparams(n_grid):
    return pltpu.CompilerParams(dimension_semantics=("arbitrary",) * n_grid, vmem_limit_bytes=VMEM_LIMIT)


def _silu(v):
    return v * (1.0 / (1.0 + jnp.exp(-v)))


def _sigmoid(v):
    return 1.0 / (1.0 + jnp.exp(-v))


def _in_proj_kernel(x_ref, nw_ref, w_ref, o_ref, h_ref):
    @pl.when(pl.program_id(1) == 0)
    def _():
        rows = 256
        for r in range(0, x_ref.shape[0], rows):
            xf = x_ref[r:r + rows, :]
            ms = jnp.mean(xf * xf, axis=-1, keepdims=True)
            h_ref[r:r + rows, :] = (xf * lax.rsqrt(ms + RMS_EPS) * nw_ref[...]).astype(BF16)

    o_ref[...] = jnp.dot(h_ref[...], w_ref[...], preferred_element_type=F32)


def _in_proj(x2d, norm_w, w_perm, tm):
    t = x2d.shape[0]
    return pl.pallas_call(
        _in_proj_kernel,
        grid=(t // tm, _COL.TOTAL // IN_TN),
        in_specs=[pl.BlockSpec((tm, D_MODEL), lambda i, j: (i, 0)),
                  pl.BlockSpec((1, D_MODEL), lambda i, j: (0, 0)),
                  pl.BlockSpec((D_MODEL, IN_TN), lambda i, j: (0, j))],
        out_specs=pl.BlockSpec((tm, IN_TN), lambda i, j: (i, j)),
        out_shape=jax.ShapeDtypeStruct((t, _COL.TOTAL), F32),
        scratch_shapes=[pltpu.VMEM((tm, D_MODEL), BF16)],
        compiler_params=_cparams(2),
        name="in_proj",
    )(x2d, norm_w, w_perm)


def _bucket_of_dist(dist):
    n = np.maximum(dist, 0)
    nf = np.maximum(n, 1).astype(np.float32)
    large = MAX_EXACT + (np.log(nf / MAX_EXACT) / math.log(WINDOW / MAX_EXACT)
                         * (N_BUCKETS - MAX_EXACT)).astype(np.int32)
    large = np.minimum(large, N_BUCKETS - 1)
    bucket = np.where(n < MAX_EXACT, n, large)
    return np.where((dist >= 0) & (dist <= WINDOW), bucket, -1).astype(np.int32)


def _bias_of_bucket(bk, table_ref, h):
    acc = jnp.full(bk.shape, -jnp.inf, F32)
    for b in range(N_BUCKETS):
        acc = jnp.where(bk == b, table_ref[b, h], acc)
    return acc


def _band_bias_kernel(table_ref, bucket_ref, o_ref):
    o_ref[0] = _bias_of_bucket(bucket_ref[...], table_ref, pl.program_id(0))


def _band_bias_t(rel_bias):
    kj = np.arange(2 * WINDOW)[:, None]
    qi = np.arange(WINDOW)[None, :]
    bucket = jnp.asarray(_bucket_of_dist(qi + WINDOW - kj))
    return pl.pallas_call(
        _band_bias_kernel,
        grid=(Q_HEADS,),
        in_specs=[pl.BlockSpec(memory_space=pltpu.SMEM),
                  pl.BlockSpec((2 * WINDOW, WINDOW), lambda h: (0, 0))],
        out_specs=pl.BlockSpec((1, 2 * WINDOW, WINDOW), lambda h: (h, 0, 0)),
        out_shape=jax.ShapeDtypeStruct((Q_HEADS, 2 * WINDOW, WINDOW), F32),
        compiler_params=_cparams(1),
        name="band_bias",
    )(rel_bias, bucket)


def _decode_bias_kernel(table_ref, bc_ref, bn_ref, oc_ref, on_ref, rows):
    g = pl.program_id(0)
    for r in range(REP):
        rs = slice(r * rows, (r + 1) * rows)
        oc_ref[0, rs, :] = _bias_of_bucket(bc_ref[...], table_ref, g * REP + r)
        on_ref[0, rs, :] = _bias_of_bucket(bn_ref[...], table_ref, g * REP + r)


def _decode_bias(rel_bias, seq, w_buf):
    bt = SAMPLE_BT
    rows = bt * seq
    b_of_row = np.arange(rows)[:, None] // seq
    l_of_row = np.arange(rows)[:, None] % seq
    bc_col, j_col = np.arange(bt * w_buf)[None, :] // w_buf, np.arange(bt * w_buf)[None, :] % w_buf
    bn_col, l_col = np.arange(rows)[None, :] // seq, np.arange(rows)[None, :] % seq
    bucket_c = np.where(b_of_row == bc_col, _bucket_of_dist(l_of_row + w_buf - j_col), -1).astype(np.int32)
    bucket_n = np.where(b_of_row == bn_col, _bucket_of_dist(l_of_row - l_col), -1).astype(np.int32)
    return pl.pallas_call(
        functools.partial(_decode_bias_kernel, rows=rows),
        grid=(KV_HEADS,),
        in_specs=[pl.BlockSpec(memory_space=pltpu.SMEM),
                  pl.BlockSpec(bucket_c.shape, lambda g: (0, 0)),
                  pl.BlockSpec(bucket_n.shape, lambda g: (0, 0))],
        out_specs=[pl.BlockSpec((1, REP * rows, bt * w_buf), lambda g: (g, 0, 0)),
                   pl.BlockSpec((1, REP * rows, rows), lambda g: (g, 0, 0))],
        out_shape=[jax.ShapeDtypeStruct((KV_HEADS, REP * rows, bt * w_buf), F32),
                   jax.ShapeDtypeStruct((KV_HEADS, REP * rows, rows), F32)],
        compiler_params=_cparams(1),
        name="decode_bias",
    )(rel_bias, jnp.asarray(bucket_c), jnp.asarray(bucket_n))


CONV_STRIP = 512


def _conv_silu(ext_ref, base, rows, cw_ref, cb_ref, xc_ref, out_row):
    for c0 in range(0, CONV_DIM, CONV_STRIP):
        cs = slice(c0, c0 + CONV_STRIP)
        acc = cb_ref[:, cs] + cw_ref[3:4, cs] * ext_ref[base:base + rows, cs]
        for k in range(1, CONV_WIDTH):
            acc = acc + cw_ref[3 - k:4 - k, cs] * ext_ref[base - k:base - k + rows, cs]
        xc_ref[out_row:out_row + rows, cs] = _silu(acc)


def _ssd_core(rows, same_seq, z_ref, dt_ref, dtb_ref, alog_ref, dskip_ref, nw_ref, xc_ref, xd_ref, y_ref,
              yoff_fn, state_fn, g_ref):
    li = lax.broadcasted_iota(jnp.int32, (rows, rows), 0)
    si = lax.broadcasted_iota(jnp.int32, (rows, rows), 1)
    causal = same_seq & (si <= li)
    lane = lax.broadcasted_iota(jnp.int32, (rows, LANES), 1)
    first_half = lane < SSM_HEAD_DIM

    dt = jax.nn.softplus(dt_ref[...] + dtb_ref[...])
    a_neg = -jnp.exp(alog_ref[...])
    d_a = dt * a_neg
    acs = jnp.dot(causal.astype(F32), d_a, precision=HIGHEST, preferred_element_type=F32)
    a_end = jnp.dot(same_seq.astype(F32), d_a, precision=HIGHEST, preferred_element_type=F32)
    acs_t = acs.T
    dt_t = dt.T
    w_in_state = dt * jnp.exp(a_end - acs)
    e_acs = jnp.exp(acs)
    cd_rows = jnp.exp(a_end)

    def pair_expand(m, j):
        a = jnp.broadcast_to(m[:, 2 * j:2 * j + 1], (rows, LANES))
        b = jnp.broadcast_to(m[:, 2 * j + 1:2 * j + 2], (rows, LANES))
        return jnp.where(first_half, a, b)

    pairs_per_group = SSM_HEADS // SSM_GROUPS // 2
    for g in range(SSM_GROUPS):
        b_g = xc_ref[:, D_INNER + g * SSM_STATE:D_INNER + (g + 1) * SSM_STATE].astype(BF16)
        c_g = xc_ref[:, D_INNER + GN + g * SSM_STATE:D_INNER + GN + (g + 1) * SSM_STATE].astype(BF16)
        cb = lax.dot_general(c_g, b_g, (((1,), (1,)), ((), ())), preferred_element_type=F32)
        yoff = yoff_fn(g, c_g)
        for jj in range(pairs_per_group):
            j = g * pairs_per_group + jj
            ps = slice(j * LANES, (j + 1) * LANES)
            xs_pair = xc_ref[:, ps]
            xs_bf16 = xs_pair.astype(BF16)
            parts = []
            for hh in range(2):
                h = 2 * j + hh
                seg = acs[:, h:h + 1] - acs_t[h:h + 1, :]
                decay = jnp.exp(jnp.where(causal, seg, -jnp.inf))
                m = (cb * decay * dt_t[h:h + 1, :]).astype(BF16)
                parts.append(jnp.dot(m, xs_bf16, preferred_element_type=F32))
            yd = jnp.where(first_half, parts[0], parts[1])
            xd_ref[:, ps] = (xs_pair * pair_expand(w_in_state, j)).astype(xd_ref.dtype)
            y_ref[:, ps] = (yd + yoff[:, jj * LANES:(jj + 1) * LANES] * pair_expand(e_acs, j)
                            + dskip_ref[:, ps] * xs_pair)
        state_fn(g, cd_rows)

    group_w = D_INNER // SSM_GROUPS
    for g in range(SSM_GROUPS):
        gs = slice(g * group_w, (g + 1) * group_w)
        gg = y_ref[:, gs] * _silu(z_ref[:, gs])
        ms = jnp.mean(gg * gg, axis=-1, keepdims=True)
        g_ref[:, gs] = (gg * lax.rsqrt(ms + RMS_EPS) * nw_ref[:, gs]).astype(BF16)


def _ssd_prompt_kernel(z_ref, xs_ref, bc_ref, dt_ref, cw_ref, cb_ref, dtb_ref, alog_ref, dskip_ref, nw_ref,
                       g_ref, st_ref, ext_ref, xc_ref, xd_ref, y_ref, stt_ref):
    c = pl.program_id(1)
    rows = SSD_CHUNK

    @pl.when(c == 0)
    def _():
        ext_ref[0:SUBLANES, :] = jnp.zeros((SUBLANES, CONV_DIM), F32)
        stt_ref[...] = jnp.zeros(stt_ref.shape, F32)

    @pl.when(c != 0)
    def _():
        ext_ref[0:SUBLANES, :] = ext_ref[rows:rows + SUBLANES, :]

    ext_ref[SUBLANES:SUBLANES + rows, 0:D_INNER] = xs_ref[...]
    ext_ref[SUBLANES:SUBLANES + rows, D_INNER:CONV_DIM] = bc_ref[...]
    _conv_silu(ext_ref, SUBLANES, rows, cw_ref, cb_ref, xc_ref, 0)

    def yoff_fn(g, c_g):
        return jnp.dot(c_g, stt_ref[:, g * 512:(g + 1) * 512].astype(BF16), preferred_element_type=F32)

    def state_fn(g, cd_rows):
        b_g = xc_ref[:, D_INNER + g * SSM_STATE:D_INNER + (g + 1) * SSM_STATE].astype(BF16)
        xd_g = xd_ref[:, g * 512:(g + 1) * 512]
        new = lax.dot_general(b_g, xd_g, (((0,), (0,)), ((), ())), preferred_element_type=F32)
        lane = lax.broadcasted_iota(jnp.int32, (SSM_STATE, LANES), 1)
        for jj in range(4):
            j = g * 4 + jj
            ps = slice(j * LANES, (j + 1) * LANES)
            cd = jnp.where(lane < SSM_HEAD_DIM,
                           jnp.broadcast_to(cd_rows[0:1, 2 * j:2 * j + 1], (SSM_STATE, LANES)),
                           jnp.broadcast_to(cd_rows[0:1, 2 * j + 1:2 * j + 2], (SSM_STATE, LANES)))
            stt_ref[:, ps] = stt_ref[:, ps] * cd + new[:, jj * LANES:(jj + 1) * LANES]

    same_seq = jnp.full((rows, rows), True)
    _ssd_core(rows, same_seq, z_ref, dt_ref, dtb_ref, alog_ref, dskip_ref, nw_ref, xc_ref, xd_ref, y_ref,
              yoff_fn, state_fn, g_ref)

    @pl.when(c == pl.num_programs(1) - 1)
    def _():
        for j in range(D_INNER // LANES):
            st_ref[0, j * LANES:(j + 1) * LANES, :] = stt_ref[:, j * LANES:(j + 1) * LANES].T


def _ssd_prompt(u, conv_w, conv_b, dt_bias, a_log, dskip_cols, ssm_norm_w, batch, seq):
    nc = seq // SSD_CHUNK
    rows = SSD_CHUNK
    row_blk = lambda b, c: b * nc + c
    full = lambda shape: pl.BlockSpec(shape, lambda b, c: (0,) * len(shape))
    return pl.pallas_call(
        _ssd_prompt_kernel,
        grid=(batch, nc),
        in_specs=[pl.BlockSpec((rows, D_INNER), lambda b, c: (row_blk(b, c), _COL.Z // D_INNER)),
                  pl.BlockSpec((rows, D_INNER), lambda b, c: (row_blk(b, c), _COL.XS // D_INNER)),
                  pl.BlockSpec((rows, 2 * GN), lambda b, c: (row_blk(b, c), _COL.BC // (2 * GN))),
                  pl.BlockSpec((rows, DT_PAD), lambda b, c: (row_blk(b, c), _COL.DT // DT_PAD)),
                  full((CONV_WIDTH, CONV_DIM)), full((1, CONV_DIM)), full((1, DT_PAD)), full((1, DT_PAD)),
                  full((1, D_INNER)), full((1, D_INNER))],
        out_specs=[pl.BlockSpec((rows, D_INNER), lambda b, c: (row_blk(b, c), 0)),
                   pl.BlockSpec((1, D_INNER, SSM_STATE), lambda b, c: (b, 0, 0))],
        out_shape=[jax.ShapeDtypeStruct((batch * seq, D_INNER), BF16),
                   jax.ShapeDtypeStruct((batch, D_INNER, SSM_STATE), F32)],
        scratch_shapes=[pltpu.VMEM((rows + SUBLANES, CONV_DIM), F32),
                        pltpu.VMEM((rows, CONV_DIM), F32),
                        pltpu.VMEM((rows, D_INNER), BF16),
                        pltpu.VMEM((rows, D_INNER), F32),
                        pltpu.VMEM((SSM_STATE, D_INNER), F32)],
        compiler_params=_cparams(2),
        name="ssd_prompt",
    )(u, u, u, u, conv_w, conv_b, dt_bias, a_log, dskip_cols, ssm_norm_w)


SAMPLE_BT = 4


def _ssd_sample_kernel(z_ref, xs_ref, bc_ref, dt_ref, sc_ref, st_ref, cw_ref, cb_ref, dtb_ref, alog_ref, dskip_ref,
                       nw_ref, g_ref, nst_ref, nsc_ref, ext_ref, xc_ref, xd_ref, y_ref, yoff_ref, seq_len):
    bt = SAMPLE_BT
    rows = bt * seq_len
    halo = CONV_WIDTH - 1
    assert seq_len == SUBLANES
    for b in range(bt):
        base = 2 * SUBLANES * b + SUBLANES
        ext_ref[base - halo:base, :] = sc_ref[b]
        ext_ref[base:base + seq_len, 0:D_INNER] = xs_ref[b * seq_len:(b + 1) * seq_len, :]
        ext_ref[base:base + seq_len, D_INNER:CONV_DIM] = bc_ref[b * seq_len:(b + 1) * seq_len, :]
        _conv_silu(ext_ref, base, seq_len, cw_ref, cb_ref, xc_ref, b * seq_len)
        nsc_ref[b] = ext_ref[base + seq_len - halo:base + seq_len, :]

    for b in range(bt):
        rs = slice(b * seq_len, (b + 1) * seq_len)
        for g in range(SSM_GROUPS):
            c_bg = xc_ref[rs, D_INNER + GN + g * SSM_STATE:D_INNER + GN + (g + 1) * SSM_STATE].astype(BF16)
            s_bg = st_ref[b, g * 512:(g + 1) * 512, :].astype(BF16)
            yoff_ref[rs, g * 512:(g + 1) * 512] = lax.dot_general(
                c_bg, s_bg, (((1,), (1,)), ((), ())), preferred_element_type=F32)

    def yoff_fn(g, c_g):
        return yoff_ref[:, g * 512:(g + 1) * 512]

    def state_fn(g, cd_rows):
        for b in range(bt):
            rs = slice(b * seq_len, (b + 1) * seq_len)
            b_bg = xc_ref[rs, D_INNER + g * SSM_STATE:D_INNER + (g + 1) * SSM_STATE].astype(BF16)
            xd_bg = xd_ref[rs, g * 512:(g + 1) * 512].astype(BF16)
            new = lax.dot_general(xd_bg, b_bg, (((0,), (0,)), ((), ())), preferred_element_type=F32)
            for hh in range(8):
                h = g * 8 + hh
                hs = slice(h * SSM_HEAD_DIM, (h + 1) * SSM_HEAD_DIM)
                cd = jnp.broadcast_to(cd_rows[b * seq_len:b * seq_len + 1, h:h + 1], (SSM_HEAD_DIM, SSM_STATE))
                nst_ref[b, hs, :] = st_ref[b, hs, :] * cd + new[hh * SSM_HEAD_DIM:(hh + 1) * SSM_HEAD_DIM, :]

    li = lax.broadcasted_iota(jnp.int32, (rows, rows), 0)
    si = lax.broadcasted_iota(jnp.int32, (rows, rows), 1)
    same_seq = (li // seq_len) == (si // seq_len)
    _ssd_core(rows, same_seq, z_ref, dt_ref, dtb_ref, alog_ref, dskip_ref, nw_ref, xc_ref, xd_ref, y_ref,
              yoff_fn, state_fn, g_ref)


def _ssd_sample(u, state_conv, state, conv_w, conv_b, dt_bias, a_log, dskip_cols, ssm_norm_w, batch, seq):
    bt = SAMPLE_BT
    rows = bt * seq
    halo = CONV_WIDTH - 1
    full = lambda shape: pl.BlockSpec(shape, lambda i: (0,) * len(shape))
    return pl.pallas_call(
        functools.partial(_ssd_sample_kernel, seq_len=seq),
        grid=(batch // bt,),
        in_specs=[pl.BlockSpec((rows, D_INNER), lambda i: (i, _COL.Z // D_INNER)),
                  pl.BlockSpec((rows, D_INNER), lambda i: (i, _COL.XS // D_INNER)),
                  pl.BlockSpec((rows, 2 * GN), lambda i: (i, _COL.BC // (2 * GN))),
                  pl.BlockSpec((rows, DT_PAD), lambda i: (i, _COL.DT // DT_PAD)),
                  pl.BlockSpec((bt, halo, CONV_DIM), lambda i: (i, 0, 0)),
                  pl.BlockSpec((bt, D_INNER, SSM_STATE), lambda i: (i, 0, 0)),
                  full((CONV_WIDTH, CONV_DIM)), full((1, CONV_DIM)), full((1, DT_PAD)), full((1, DT_PAD)),
                  full((1, D_INNER)), full((1, D_INNER))],
        out_specs=[pl.BlockSpec((rows, D_INNER), lambda i: (i, 0)),
                   pl.BlockSpec((bt, D_INNER, SSM_STATE), lambda i: (i, 0, 0)),
                   pl.BlockSpec((bt, halo, CONV_DIM), lambda i: (i, 0, 0))],
        out_shape=[jax.ShapeDtypeStruct((batch * seq, D_INNER), BF16),
                   jax.ShapeDtypeStruct((batch, D_INNER, SSM_STATE), F32),
                   jax.ShapeDtypeStruct((batch, halo, CONV_DIM), F32)],
        scratch_shapes=[pltpu.VMEM((bt * 2 * SUBLANES, CONV_DIM), F32),
                        pltpu.VMEM((rows, CONV_DIM), F32),
                        pltpu.VMEM((rows, D_INNER), F32),
                        pltpu.VMEM((rows, D_INNER), F32),
                        pltpu.VMEM((rows, D_INNER), F32)],
        compiler_params=_cparams(1),
        name="ssd_sample",
    )(u, u, u, u, state_conv, state, conv_w, conv_b, dt_bias, a_log, dskip_cols, ssm_norm_w)


ATTN_SCALE = HEAD_DIM ** -0.5


def _qk(q_bf16, k_bf16):
    return lax.dot_general(q_bf16, k_bf16, (((1,), (1,)), ((), ())), preferred_element_type=F32)


def _softmax_sink_pv(parts, sink):
    m = sink
    for s, _ in parts:
        m = jnp.maximum(m, jnp.max(s, axis=-1, keepdims=True))
    denom = jnp.exp(sink - m)
    o = None
    for s, v in parts:
        p = jnp.exp(s - m)
        denom = denom + jnp.sum(p, axis=-1, keepdims=True)
        pv = jnp.dot(p.astype(BF16), v, preferred_element_type=F32)
        o = pv if o is None else o + pv
    return o / denom


def _attn_prompt_kernel(sink_ref, q_ref, kp_ref, ko_ref, vp_ref, vo_ref, ga_ref, bias_ref, o_ref):
    n = pl.program_id(1)
    no_prev = jnp.where(n == 0, -jnp.inf, 0.0)
    for g in range(KV_HEADS):
        ks = slice(g * HEAD_DIM, (g + 1) * HEAD_DIM)
        kk = jnp.concatenate([kp_ref[:, ks], ko_ref[:, ks]], axis=0).astype(BF16)
        vv_t = jnp.concatenate([vp_ref[:, ks], vo_ref[:, ks]], axis=0).T.astype(BF16)
        for rp in range(REP // 2):
            o_t = []
            for r in (2 * rp, 2 * rp + 1):
                h = g * REP + r
                q = (q_ref[:, h * HEAD_DIM:(h + 1) * HEAD_DIM] * ATTN_SCALE).astype(BF16)
                s_t = _qk(kk, q)
                s_prev = s_t[0:WINDOW] + (bias_ref[h, 0:WINDOW, :] + no_prev)
                s_own = s_t[WINDOW:2 * WINDOW] + bias_ref[h, WINDOW:2 * WINDOW, :]
                sink = sink_ref[h]
                m = jnp.maximum(jnp.maximum(jnp.max(s_prev, axis=0, keepdims=True),
                                            jnp.max(s_own, axis=0, keepdims=True)), sink)
                p = jnp.exp(jnp.concatenate([s_prev, s_own], axis=0) - m)
                denom = jnp.sum(p, axis=0, keepdims=True) + jnp.exp(sink - m)
                o_t.append(jnp.dot(vv_t, p.astype(BF16), preferred_element_type=F32) / denom)
            o_pair = jnp.concatenate(o_t, axis=0).T
            ps = slice((g * REP + 2 * rp) * HEAD_DIM, (g * REP + 2 * rp + 2) * HEAD_DIM)
            o_ref[:, ps] = (o_pair * _silu(ga_ref[:, ps])).astype(BF16)


def _attn_prompt(u, sinks, bias, batch, seq):
    nb = seq // WINDOW
    row = lambda b, n: b * nb + n
    prev = lambda b, n: jnp.maximum(b * nb + n - 1, 0)
    return pl.pallas_call(
        _attn_prompt_kernel,
        grid=(batch, nb),
        in_specs=[pl.BlockSpec(memory_space=pltpu.SMEM),
                  pl.BlockSpec((WINDOW, ATTN_WIDTH), lambda b, n: (row(b, n), _COL.Q // ATTN_WIDTH)),
                  pl.BlockSpec((WINDOW, KV_WIDTH), lambda b, n: (prev(b, n), _COL.K // KV_WIDTH)),
                  pl.BlockSpec((WINDOW, KV_WIDTH), lambda b, n: (row(b, n), _COL.K // KV_WIDTH)),
                  pl.BlockSpec((WINDOW, KV_WIDTH), lambda b, n: (prev(b, n), _COL.V // KV_WIDTH)),
                  pl.BlockSpec((WINDOW, KV_WIDTH), lambda b, n: (row(b, n), _COL.V // KV_WIDTH)),
                  pl.BlockSpec((WINDOW, ATTN_WIDTH), lambda b, n: (row(b, n), _COL.GA // ATTN_WIDTH)),
                  pl.BlockSpec((Q_HEADS, 2 * WINDOW, WINDOW), lambda b, n: (0, 0, 0))],
        out_specs=pl.BlockSpec((WINDOW, ATTN_WIDTH), lambda b, n: (row(b, n), 0)),
        out_shape=jax.ShapeDtypeStruct((batch * seq, ATTN_WIDTH), BF16),
        compiler_params=_cparams(2),
        name="attn_prompt",
    )(sinks, u, u, u, u, u, u, bias)


def _attn_sample_kernel(sink_ref, q_ref, kn_ref, vn_ref, ga_ref, ck_ref, cv_ref, bc_ref, bn_ref,
                        o_ref, nk_ref, nv_ref, seq_len):
    bt = SAMPLE_BT
    per_seq = ck_ref.shape[0] // bt
    w_buf = per_seq // KV_HEADS
    keep = (w_buf - seq_len) * KV_HEADS
    for c_ref, n_ref, new_ref in ((ck_ref, nk_ref, kn_ref), (cv_ref, nv_ref, vn_ref)):
        for b in range(bt):
            base = b * per_seq
            n_ref[base:base + keep, :] = c_ref[base + per_seq - keep:base + per_seq, :]
            for g in range(KV_HEADS):
                n_ref[pl.ds(base + keep + g, seq_len, stride=KV_HEADS), :] = (
                    new_ref[b * seq_len:(b + 1) * seq_len, g * HEAD_DIM:(g + 1) * HEAD_DIM])
    for g in range(KV_HEADS):
        ks = slice(g * HEAD_DIM, (g + 1) * HEAD_DIM)
        kc = ck_ref[pl.ds(g, bt * w_buf, stride=KV_HEADS), :].astype(BF16)
        vc = cv_ref[pl.ds(g, bt * w_buf, stride=KV_HEADS), :].astype(BF16)
        kn = kn_ref[:, ks].astype(BF16)
        vn = vn_ref[:, ks].astype(BF16)
        qs = jnp.concatenate([q_ref[:, (g * REP + r) * HEAD_DIM:(g * REP + r + 1) * HEAD_DIM] for r in range(REP)],
                             axis=0).astype(BF16)
        s_cache = _qk(qs, kc) * ATTN_SCALE + bc_ref[g]
        s_new = _qk(qs, kn) * ATTN_SCALE + bn_ref[g]
        rows = bt * seq_len
        sink = jnp.concatenate([jnp.full((rows, 1), sink_ref[g * REP + r], F32) for r in range(REP)], axis=0)
        o = _softmax_sink_pv([(s_cache, vc), (s_new, vn)], sink)
        for r in range(REP):
            hs = slice((g * REP + r) * HEAD_DIM, (g * REP + r + 1) * HEAD_DIM)
            o_ref[:, hs] = (o[r * rows:(r + 1) * rows, :] * _silu(ga_ref[:, hs])).astype(BF16)


def _attn_sample(u, sinks, cache_k, cache_v, bias_c, bias_n, batch, seq, w_buf):
    bt = SAMPLE_BT
    rows = bt * seq
    crows = bt * w_buf * KV_HEADS
    return pl.pallas_call(
        functools.partial(_attn_sample_kernel, seq_len=seq),
        grid=(batch // bt,),
        in_specs=[pl.BlockSpec(memory_space=pltpu.SMEM),
                  pl.BlockSpec((rows, ATTN_WIDTH), lambda i: (i, _COL.Q // ATTN_WIDTH)),
                  pl.BlockSpec((rows, KV_WIDTH), lambda i: (i, _COL.K // KV_WIDTH)),
                  pl.BlockSpec((rows, KV_WIDTH), lambda i: (i, _COL.V // KV_WIDTH)),
                  pl.BlockSpec((rows, ATTN_WIDTH), lambda i: (i, _COL.GA // ATTN_WIDTH)),
                  pl.BlockSpec((crows, HEAD_DIM), lambda i: (i, 0)),
                  pl.BlockSpec((crows, HEAD_DIM), lambda i: (i, 0)),
                  pl.BlockSpec(bias_c.shape, lambda i: (0, 0, 0)),
                  pl.BlockSpec(bias_n.shape, lambda i: (0, 0, 0))],
        out_specs=[pl.BlockSpec((rows, ATTN_WIDTH), lambda i: (i, 0)),
                   pl.BlockSpec((crows, HEAD_DIM), lambda i: (i, 0)),
                   pl.BlockSpec((crows, HEAD_DIM), lambda i: (i, 0))],
        out_shape=[jax.ShapeDtypeStruct((batch * seq, ATTN_WIDTH), BF16),
                   jax.ShapeDtypeStruct(cache_k.shape, F32),
                   jax.ShapeDtypeStruct(cache_v.shape, F32)],
        compiler_params=_cparams(1),
        name="attn_sample",
    )(sinks, u, u, u, u, cache_k, cache_v, bias_c, bias_n)


MERGE_TN = 256


def _merge_kernel(g_ref, og_ref, ms_ref, ma_ref, ws_ref, wa_ref, o_ref):
    p_ssm = jnp.dot(g_ref[...], ws_ref[...], preferred_element_type=F32)
    p_attn = jnp.dot(og_ref[...], wa_ref[...], preferred_element_type=F32)
    o_ref[...] = (_sigmoid(ms_ref[...]) * p_ssm + _sigmoid(ma_ref[...]) * p_attn).astype(BF16)


def _merge(g, og, u, w_ssm, w_attn, tm):
    t = g.shape[0]
    tn = MERGE_TN
    return pl.pallas_call(
        _merge_kernel,
        grid=(t // tm, D_MODEL // tn),
        in_specs=[pl.BlockSpec((tm, D_INNER), lambda i, j: (i, 0)),
                  pl.BlockSpec((tm, ATTN_WIDTH), lambda i, j: (i, 0)),
                  pl.BlockSpec((tm, tn), lambda i, j: (i, _COL.MS // tn + j)),
                  pl.BlockSpec((tm, tn), lambda i, j: (i, _COL.MA // tn + j)),
                  pl.BlockSpec((D_INNER, tn), lambda i, j: (0, j)),
                  pl.BlockSpec((ATTN_WIDTH, tn), lambda i, j: (0, j))],
        out_specs=pl.BlockSpec((tm, tn), lambda i, j: (i, j)),
        out_shape=jax.ShapeDtypeStruct((t, D_MODEL), BF16),
        compiler_params=_cparams(2),
        name="merge",
    )(g, og, u, u, w_ssm, w_attn)


def _out_proj_kernel(m_ref, x_ref, wo_ref, fw_ref, o_ref):
    xn = x_ref[...] + jnp.dot(m_ref[...], wo_ref[...], preferred_element_type=F32)
    ms = jnp.mean(xn * xn, axis=-1, keepdims=True)
    o_ref[...] = xn * lax.rsqrt(ms + RMS_EPS) * fw_ref[...]


def _out_proj(m, x2d, w_out, final_w, tm):
    t = m.shape[0]
    return pl.pallas_call(
        _out_proj_kernel,
        grid=(t // tm,),
        in_specs=[pl.BlockSpec((tm, D_MODEL), lambda i: (i, 0)),
                  pl.BlockSpec((tm, D_MODEL), lambda i: (i, 0)),
                  pl.BlockSpec((D_MODEL, D_MODEL), lambda i: (0, 0)),
                  pl.BlockSpec((1, D_MODEL), lambda i: (0, 0))],
        out_specs=pl.BlockSpec((tm, D_MODEL), lambda i: (i, 0)),
        out_shape=jax.ShapeDtypeStruct((t, D_MODEL), F32),
        compiler_params=_cparams(1),
        name="out_proj",
    )(m, x2d, w_out, final_w)


RG_TN = 512
RG_STRAIGHT = _OFF_DT // RG_TN
RG_LAST = _COL.DT // RG_TN
assert _OFF_DT % RG_TN == 0 and _COL.DT % RG_TN == 0 and (_OFF_Q - _OFF_DT) * 2 == LANES


def _rg_src_tile(j):
    jj = j - RG_STRAIGHT
    q_t, kv_t = ATTN_WIDTH // RG_TN, 2 * KV_WIDTH // RG_TN
    gate_t = (ATTN_WIDTH + 2 * D_MODEL) // RG_TN
    return jnp.where(jj < q_t, jj, jnp.where(jj < q_t + gate_t, jj + kv_t, jj - gate_t))


def _regroup_kernel(a_ref, b_ref, o_ref):
    j = pl.program_id(0)
    rows = a_ref.shape[0]
    lane = lax.broadcasted_iota(jnp.int32, (rows, LANES), 1)
    low = lane < LANES // 2

    @pl.when(j < RG_STRAIGHT)
    def _():
        o_ref[...] = a_ref[...].astype(BF16)

    @pl.when((j >= RG_STRAIGHT) & (j < RG_LAST))
    def _():
        nt = RG_TN // LANES
        for t in range(nt):
            cur = a_ref[:, t * LANES:(t + 1) * LANES]
            nxt = a_ref[:, (t + 1) * LANES:(t + 2) * LANES] if t + 1 < nt else b_ref[...]
            o_ref[:, t * LANES:(t + 1) * LANES] = jnp.where(
                low, pltpu.roll(cur, LANES // 2, axis=1), pltpu.roll(nxt, LANES // 2, axis=1)).astype(BF16)

    @pl.when(j == RG_LAST)
    def _():
        o_ref[...] = jnp.zeros(o_ref.shape, BF16)
        o_ref[:, 0:LANES] = jnp.where(low, a_ref[:, 0:LANES], 0.0).astype(BF16)


def _regroup_w_in(w):
    rows = w.shape[0]
    shifted = lambda j: (j >= RG_STRAIGHT) & (j < RG_LAST)
    a_idx = lambda j: jnp.where(j < RG_STRAIGHT, j, jnp.where(j == RG_LAST, RG_STRAIGHT, RG_STRAIGHT + _rg_src_tile(j)))
    b_idx = lambda j: jnp.where(shifted(j), (RG_STRAIGHT + _rg_src_tile(j) + 1) * (RG_TN // LANES),
                                (RG_STRAIGHT + 1) * (RG_TN // LANES))
    return pl.pallas_call(
        _regroup_kernel,
        grid=(RG_LAST + 1,),
        in_specs=[pl.BlockSpec((rows, RG_TN), lambda j: (0, a_idx(j))),
                  pl.BlockSpec((rows, LANES), lambda j: (0, b_idx(j)))],
        out_specs=pl.BlockSpec((rows, RG_TN), lambda j: (0, j)),
        out_shape=jax.ShapeDtypeStruct((rows, _COL.TOTAL), BF16),
        compiler_params=_cparams(1),
        name="regroup_w_in",
    )(w, w)


def _pad_lanes(v):
    return jnp.pad(v.reshape(1, -1), ((0, 0), (0, DT_PAD - v.shape[-1])))


def kernel(x_prompt, x_sample, cache_k, cache_v, state_conv, state_ssm, norm_w, w_in, conv_w, conv_b, dt_bias, a_log,
           d_skip, ssm_norm_w, w_ssm_branch, attn_sinks, w_attn_branch, w_out, rel_bias, final_norm_w):
    assert w_in.shape[0] == 1, "single-layer kernel"
    batch, seq, _ = x_prompt.shape
    dec_batch, dec_seq, _ = x_sample.shape
    w_buf = cache_k.shape[2]
    assert seq % SSD_CHUNK == 0 and seq % WINDOW == 0 and dec_seq == SUBLANES and w_buf == WINDOW
    assert dec_batch % SAMPLE_BT == 0

    w_perm = _regroup_w_in(w_in[0])
    nw = norm_w[0].reshape(1, D_MODEL)
    cw, cb = conv_w[0], conv_b[0].reshape(1, CONV_DIM)
    dtb, alog = _pad_lanes(dt_bias[0]), _pad_lanes(a_log[0])
    dskip_cols = jnp.repeat(d_skip[0], SSM_HEAD_DIM).reshape(1, D_INNER)
    snw = ssm_norm_w[0].reshape(1, D_INNER)
    w_ssm = w_ssm_branch[0].astype(BF16)
    w_attn = w_attn_branch[0].astype(BF16)
    wo = w_out[0].astype(BF16)
    fw = final_norm_w.reshape(1, D_MODEL)
    sinks = attn_sinks[0]

    xp2 = x_prompt.reshape(batch * seq, D_MODEL)
    xs2 = x_sample.reshape(dec_batch * dec_seq, D_MODEL)

    band_bias_t = _band_bias_t(rel_bias)
    bias_c, bias_n = _decode_bias(rel_bias, dec_seq, w_buf)

    u_p = _in_proj(xp2, nw, w_perm, tm=1024)
    g_p, st_p = _ssd_prompt(u_p, cw, cb, dtb, alog, dskip_cols, snw, batch, seq)
    og_p = _attn_prompt(u_p, sinks, band_bias_t, batch, seq)
    m_p = _merge(g_p, og_p, u_p, w_ssm, w_attn, tm=1024)
    y_p = _out_proj(m_p, xp2, wo, fw, tm=512)

    u_s = _in_proj(xs2, nw, w_perm, tm=1024)
    g_s, st_s, conv_s = _ssd_sample(u_s, state_conv[0], state_ssm[0].reshape(dec_batch, D_INNER, SSM_STATE),
                                    cw, cb, dtb, alog, dskip_cols, snw, dec_batch, dec_seq)
    cache_rows = dec_batch * w_buf * KV_HEADS
    og_s, nk_s, nv_s = _attn_sample(
        u_s, sinks, cache_k[0].reshape(cache_rows, HEAD_DIM), cache_v[0].reshape(cache_rows, HEAD_DIM),
        bias_c, bias_n, dec_batch, dec_seq, w_buf)
    m_s = _merge(g_s, og_s, u_s, w_ssm, w_attn, tm=1024)
    y_s = _out_proj(m_s, xs2, wo, fw, tm=512)

    u_p3 = u_p.reshape(batch, seq, _COL.TOTAL)
    k_p = u_p3[:, seq - WINDOW:, _COL.K:_COL.K + KV_WIDTH].reshape(1, batch, WINDOW, KV_HEADS, HEAD_DIM)
    v_p = u_p3[:, seq - WINDOW:, _COL.V:_COL.V + KV_WIDTH].reshape(1, batch, WINDOW, KV_HEADS, HEAD_DIM)
    conv_p = u_p3[:, seq - (CONV_WIDTH - 1):, _COL.XS:_COL.XS + CONV_DIM][None]
    conv_s = conv_s[None]
    ssm_p = st_p.reshape(1, batch, SSM_HEADS, SSM_HEAD_DIM, SSM_STATE)
    ssm_s = st_s.reshape(1, dec_batch, SSM_HEADS, SSM_HEAD_DIM, SSM_STATE)
    k_s = nk_s.reshape(1, dec_batch, w_buf, KV_HEADS, HEAD_DIM)
    v_s = nv_s.reshape(1, dec_batch, w_buf, KV_HEADS, HEAD_DIM)
    return (y_p.reshape(batch, seq, D_MODEL), y_s.reshape(dec_batch, dec_seq, D_MODEL),
            k_p, v_p, conv_p, ssm_p, k_s, v_s, conv_s, ssm_s)
```

```python
import functools
import math

import numpy as np
import jax
import jax.numpy as jnp
from jax import lax
from jax.experimental import pallas as pl
from jax.experimental.pallas import tpu as pltpu

F32 = jnp.float32
BF16 = jnp.bfloat16
HIGHEST = lax.Precision.HIGHEST

D_MODEL = 2048
D_INNER = 4096
SSM_HEAD_DIM = 64
SSM_HEADS = 64
SSM_GROUPS = 8
SSM_STATE = 128
CONV_WIDTH = 4
CONV_HALO = CONV_WIDTH - 1
GN = SSM_GROUPS * SSM_STATE
CONV_DIM = D_INNER + 2 * GN
SSD_CHUNK = 128
GROUP_W = D_INNER // SSM_GROUPS
HEAD_DIM = 64
Q_HEADS = 32
KV_HEADS = 8
REP = Q_HEADS // KV_HEADS
ATTN_WIDTH = Q_HEADS * HEAD_DIM
KV_WIDTH = KV_HEADS * HEAD_DIM
WINDOW = 128
N_BUCKETS = 32
MAX_EXACT = N_BUCKETS // 2
RMS_EPS = 1e-6

_OFF_Z = 0
_OFF_XBC = _OFF_Z + D_INNER
_OFF_DT = _OFF_XBC + CONV_DIM
_OFF_Q = _OFF_DT + SSM_HEADS
_OFF_K = _OFF_Q + ATTN_WIDTH
_OFF_V = _OFF_K + KV_WIDTH
_OFF_GA = _OFF_V + KV_WIDTH
_OFF_MS = _OFF_GA + ATTN_WIDTH
_OFF_MA = _OFF_MS + D_MODEL
_IN_COLS = _OFF_MA + D_MODEL

LANES = 128
SUBLANES = 8
DT_PAD = LANES


class _COL:
    Z = 0
    XS = Z + D_INNER
    BC = XS + D_INNER
    Q = BC + 2 * GN
    GA = Q + ATTN_WIDTH
    MS = GA + ATTN_WIDTH
    MA = MS + D_MODEL
    K = MA + D_MODEL
    V = K + KV_WIDTH
    TOTAL = V + KV_WIDTH


VMEM_LIMIT = 52 * 1024 * 1024


def _cparams(n_grid):
    return pltpu.CompilerParams(dimension_semantics=("arbitrary",) * n_grid, vmem_limit_bytes=VMEM_LIMIT)


def _silu(v):
    return v * (1.0 / (1.0 + jnp.exp(-v)))


def _sigmoid(v):
    return 1.0 / (1.0 + jnp.exp(-v))


def _dot_nt(a, b):
    return lax.dot_general(a, b, (((1,), (1,)), ((), ())), preferred_element_type=F32)


NORM_TM = 512


def _norm_dt_kernel(xp_ref, xs_ref, nw_ref, wdt_ref, h_ref, dt_ref, n_prompt_tiles):
    i = pl.program_id(0)

    def emit(x_ref):
        xf = x_ref[...]
        ms = jnp.mean(xf * xf, axis=-1, keepdims=True)
        h = (xf * lax.rsqrt(ms + RMS_EPS) * nw_ref[...]).astype(BF16)
        h_ref[...] = h
        dt_ref[...] = _dot_nt(h, wdt_ref[...].astype(BF16))

    @pl.when(i < n_prompt_tiles)
    def _():
        emit(xp_ref)

    @pl.when(i >= n_prompt_tiles)
    def _():
        emit(xs_ref)


def _norm_dt(xp2, xs2, norm_w, w_t):
    tp, ts = xp2.shape[0], xs2.shape[0]
    n_p, n_s = tp // NORM_TM, ts // NORM_TM
    return pl.pallas_call(
        functools.partial(_norm_dt_kernel, n_prompt_tiles=n_p),
        grid=(n_p + n_s,),
        in_specs=[pl.BlockSpec((NORM_TM, D_MODEL), lambda i: (jnp.minimum(i, n_p - 1), 0)),
                  pl.BlockSpec((NORM_TM, D_MODEL), lambda i: (jnp.maximum(i - n_p, 0), 0)),
                  pl.BlockSpec((1, D_MODEL), lambda i: (0, 0)),
                  pl.BlockSpec((DT_PAD, D_MODEL), lambda i: (_OFF_DT // DT_PAD, 0))],
        out_specs=[pl.BlockSpec((NORM_TM, D_MODEL), lambda i: (i, 0)),
                   pl.BlockSpec((NORM_TM, DT_PAD), lambda i: (i, 0))],
        out_shape=[jax.ShapeDtypeStruct((tp + ts, D_MODEL), BF16),
                   jax.ShapeDtypeStruct((tp + ts, DT_PAD), F32)],
        compiler_params=_cparams(1),
        name="norm_dt",
    )(xp2, xs2, norm_w, w_t)


IN_TM = 1024
IN_TN = 1024
IN_CHUNK = 512
assert _COL.TOTAL % IN_TN == 0 and IN_TN == 2 * IN_CHUNK and _OFF_DT % IN_CHUNK == 0


def _src_row(c):
    straight = _OFF_DT // IN_CHUNK
    jj = c - straight
    q_t, kv_t = ATTN_WIDTH // IN_CHUNK, 2 * KV_WIDTH // IN_CHUNK
    gate_t = (ATTN_WIDTH + 2 * D_MODEL) // IN_CHUNK
    shifted = jnp.where(jj < q_t, jj, jnp.where(jj < q_t + gate_t, jj + kv_t, jj - gate_t))
    return pl.multiple_of(jnp.where(c < straight, c * IN_CHUNK, _OFF_Q + shifted * IN_CHUNK), SSM_HEADS)


def _in_proj_kernel(h_ref, w0_ref, w1_ref, o_ref, wbf_ref):
    @pl.when(pl.program_id(1) == 0)
    def _():
        wbf_ref[0:IN_CHUNK, :] = w0_ref[...].astype(BF16)
        wbf_ref[IN_CHUNK:IN_TN, :] = w1_ref[...].astype(BF16)

    o_ref[...] = _dot_nt(h_ref[...], wbf_ref[...])


def _in_proj(h, w_t):
    t = h.shape[0]
    return pl.pallas_call(
        _in_proj_kernel,
        grid=(_COL.TOTAL // IN_TN, t // IN_TM),
        in_specs=[pl.BlockSpec((IN_TM, D_MODEL), lambda j, i: (i, 0)),
                  pl.BlockSpec((pl.Element(IN_CHUNK), pl.Element(D_MODEL)), lambda j, i: (_src_row(2 * j), 0)),
                  pl.BlockSpec((pl.Element(IN_CHUNK), pl.Element(D_MODEL)), lambda j, i: (_src_row(2 * j + 1), 0))],
        out_specs=pl.BlockSpec((IN_TM, IN_TN), lambda j, i: (i, j)),
        out_shape=jax.ShapeDtypeStruct((t, _COL.TOTAL), F32),
        scratch_shapes=[pltpu.VMEM((IN_TN, D_MODEL), BF16)],
        compiler_params=_cparams(2),
        name="in_proj",
    )(h, w_t, w_t)


def _kv_t_kernel(w_ref, h_ref, o_ref):
    o_ref[...] = _dot_nt(w_ref[...].astype(BF16), h_ref[...])


def _kv_t(h, w_t, row0, rows):
    assert row0 % rows == 0
    return pl.pallas_call(
        _kv_t_kernel,
        grid=(1,),
        in_specs=[pl.BlockSpec((pl.Element(2 * KV_WIDTH), pl.Element(D_MODEL)), lambda i: (_OFF_K, 0)),
                  pl.BlockSpec((rows, D_MODEL), lambda i: (row0 // rows, 0))],
        out_specs=pl.BlockSpec((2 * KV_WIDTH, rows), lambda i: (0, 0)),
        out_shape=jax.ShapeDtypeStruct((2 * KV_WIDTH, rows), F32),
        compiler_params=_cparams(1),
        name="kv_t",
    )(w_t, h)


def _bucket_of_dist(dist):
    n = np.maximum(dist, 0)
    nf = np.maximum(n, 1).astype(np.float32)
    large = MAX_EXACT + (np.log(nf / MAX_EXACT) / math.log(WINDOW / MAX_EXACT)
                         * (N_BUCKETS - MAX_EXACT)).astype(np.int32)
    large = np.minimum(large, N_BUCKETS - 1)
    bucket = np.where(n < MAX_EXACT, n, large)
    return np.where((dist >= 0) & (dist <= WINDOW), bucket, -1).astype(np.int32)


def _bias_of_bucket(bk, table_ref, h):
    acc = jnp.full(bk.shape, -jnp.inf, F32)
    for b in range(N_BUCKETS):
        acc = jnp.where(bk == b, table_ref[b, h], acc)
    return acc


def _band_bias_kernel(table_ref, bucket_ref, o_ref):
    o_ref[0] = _bias_of_bucket(bucket_ref[...], table_ref, pl.program_id(0))


def _band_bias_t(rel_bias):
    kj = np.arange(2 * WINDOW)[:, None]
    qi = np.arange(WINDOW)[None, :]
    bucket = jnp.asarray(_bucket_of_dist(qi + WINDOW - kj))
    return pl.pallas_call(
        _band_bias_kernel,
        grid=(Q_HEADS,),
        in_specs=[pl.BlockSpec(memory_space=pltpu.SMEM),
                  pl.BlockSpec((2 * WINDOW, WINDOW), lambda h: (0, 0))],
        out_specs=pl.BlockSpec((1, 2 * WINDOW, WINDOW), lambda h: (h, 0, 0)),
        out_shape=jax.ShapeDtypeStruct((Q_HEADS, 2 * WINDOW, WINDOW), F32),
        compiler_params=_cparams(1),
        name="band_bias",
    )(rel_bias, bucket)


def _decode_bias_kernel(table_ref, bucket_ref, o_ref, seq):
    g = pl.program_id(0)
    for r in range(REP):
        o_ref[0, r * seq:(r + 1) * seq, :] = _bias_of_bucket(bucket_ref[...], table_ref, g * REP + r)


def _decode_bias(rel_bias, seq, w_buf):
    l = np.arange(seq)[:, None]
    j = np.arange(w_buf)[None, :]
    dist_old = np.where((j < seq), l + w_buf - j, -1)
    dist_new = np.where(j < w_buf - seq, l + w_buf - (j + seq), l - (j - (w_buf - seq)))
    bucket = jnp.asarray(np.concatenate([_bucket_of_dist(dist_old), _bucket_of_dist(dist_new)], axis=1))
    return pl.pallas_call(
        functools.partial(_decode_bias_kernel, seq=seq),
        grid=(KV_HEADS,),
        in_specs=[pl.BlockSpec(memory_space=pltpu.SMEM),
                  pl.BlockSpec(bucket.shape, lambda g: (0, 0))],
        out_specs=pl.BlockSpec((1, REP * seq, 2 * w_buf), lambda g: (g, 0, 0)),
        out_shape=jax.ShapeDtypeStruct((KV_HEADS, REP * seq, 2 * w_buf), F32),
        compiler_params=_cparams(1),
        name="decode_bias",
    )(rel_bias, bucket)


CONV_STRIP = 512


def _conv_silu(ext_ref, base, rows, cw_ref, cb_ref, xc_ref, out_row):
    for c0 in range(0, CONV_DIM, CONV_STRIP):
        cs = slice(c0, c0 + CONV_STRIP)
        acc = cb_ref[:, cs] + cw_ref[3:4, cs] * ext_ref[base:base + rows, cs]
        for k in range(1, CONV_WIDTH):
            acc = acc + cw_ref[3 - k:4 - k, cs] * ext_ref[base - k:base - k + rows, cs]
        xc_ref[out_row:out_row + rows, cs] = _silu(acc)


CONV_S_SEQS = 16


def _conv_sample_kernel(xs_ref, bc_ref, sc_ref, cw_ref, cb_ref, xc_ref, nsc_ref, ext_ref, seq_len):
    slab = SUBLANES + seq_len
    for b in range(CONV_S_SEQS):
        base = b * slab + SUBLANES
        for k in range(CONV_HALO):
            ext_ref[base - CONV_HALO + k:base - CONV_HALO + k + 1, :] = sc_ref[k, b:b + 1, :]
        ext_ref[base:base + seq_len, 0:D_INNER] = xs_ref[b * seq_len:(b + 1) * seq_len, :]
        ext_ref[base:base + seq_len, D_INNER:CONV_DIM] = bc_ref[b * seq_len:(b + 1) * seq_len, :]
        _conv_silu(ext_ref, base, seq_len, cw_ref, cb_ref, xc_ref, b * seq_len)
        for k in range(CONV_HALO):
            row = base + seq_len - CONV_HALO + k
            nsc_ref[k, b:b + 1, :] = ext_ref[row:row + 1, :]


def _conv_sample(u, sc_t, conv_w, conv_b, row0, batch, seq):
    assert seq == SUBLANES and batch % CONV_S_SEQS == 0
    rows = CONV_S_SEQS * seq
    assert row0 % rows == 0
    blk0 = row0 // rows
    full = lambda shape: pl.BlockSpec(shape, lambda i: (0,) * len(shape))
    return pl.pallas_call(
        functools.partial(_conv_sample_kernel, seq_len=seq),
        grid=(batch // CONV_S_SEQS,),
        in_specs=[pl.BlockSpec((rows, D_INNER), lambda i: (blk0 + i, _COL.XS // D_INNER)),
                  pl.BlockSpec((rows, 2 * GN), lambda i: (blk0 + i, _COL.BC // (2 * GN))),
                  pl.BlockSpec((CONV_HALO, CONV_S_SEQS, CONV_DIM), lambda i: (0, i, 0)),
                  full((CONV_WIDTH, CONV_DIM)), full((1, CONV_DIM))],
        out_specs=[pl.BlockSpec((rows, CONV_DIM), lambda i: (i, 0)),
                   pl.BlockSpec((CONV_HALO, CONV_S_SEQS, CONV_DIM), lambda i: (0, i, 0))],
        out_shape=[jax.ShapeDtypeStruct((batch * seq, CONV_DIM), F32),
                   jax.ShapeDtypeStruct((CONV_HALO, batch, CONV_DIM), F32)],
        scratch_shapes=[pltpu.VMEM((CONV_S_SEQS * (SUBLANES + seq), CONV_DIM), F32)],
        compiler_params=_cparams(1),
        name="conv_sample",
    )(u, u, sc_t, conv_w, conv_b)


def _ssd_core(rows, same_seq, z_ref, dt_ref, dtb_ref, alog_ref, dskip_ref, nw_ref, xc_ref, xd_ref, y_ref,
              yoff_fn, state_fn, g_ref):
    li = lax.broadcasted_iota(jnp.int32, (rows, rows), 0)
    si = lax.broadcasted_iota(jnp.int32, (rows, rows), 1)
    causal = same_seq & (si <= li)
    lane = lax.broadcasted_iota(jnp.int32, (rows, LANES), 1)
    first_half = lane < SSM_HEAD_DIM

    dt = jax.nn.softplus(dt_ref[...] + dtb_ref[...])
    a_neg = -jnp.exp(alog_ref[...])
    d_a = dt * a_neg
    acs = jnp.dot(causal.astype(F32), d_a, precision=HIGHEST, preferred_element_type=F32)
    a_end = jnp.dot(same_seq.astype(F32), d_a, precision=HIGHEST, preferred_element_type=F32)
    acs_t = acs.T
    dt_t = dt.T
    w_in_state = dt * jnp.exp(a_end - acs)
    e_acs = jnp.exp(acs)
    cd_rows = jnp.exp(a_end)

    def pair_expand(m, j):
        a = jnp.broadcast_to(m[:, 2 * j:2 * j + 1], (rows, LANES))
        b = jnp.broadcast_to(m[:, 2 * j + 1:2 * j + 2], (rows, LANES))
        return jnp.where(first_half, a, b)

    pairs_per_group = SSM_HEADS // SSM_GROUPS // 2
    for g in range(SSM_GROUPS):
        b_g = xc_ref[:, D_INNER + g * SSM_STATE:D_INNER + (g + 1) * SSM_STATE].astype(BF16)
        c_g = xc_ref[:, D_INNER + GN + g * SSM_STATE:D_INNER + GN + (g + 1) * SSM_STATE].astype(BF16)
        cb = _dot_nt(c_g, b_g)
        yoff = yoff_fn(g, c_g)
        for jj in range(pairs_per_group):
            j = g * pairs_per_group + jj
            ps = slice(j * LANES, (j + 1) * LANES)
            xs_pair = xc_ref[:, ps]
            xs_bf16 = xs_pair.astype(BF16)
            parts = []
            for hh in range(2):
                h = 2 * j + hh
                seg = acs[:, h:h + 1] - acs_t[h:h + 1, :]
                decay = jnp.exp(jnp.where(causal, seg, -jnp.inf))
                m = (cb * decay * dt_t[h:h + 1, :]).astype(BF16)
                parts.append(jnp.dot(m, xs_bf16, preferred_element_type=F32))
            yd = jnp.where(first_half, parts[0], parts[1])
            xd_ref[:, ps] = (xs_pair * pair_expand(w_in_state, j)).astype(xd_ref.dtype)
            y_ref[:, ps] = (yd + yoff[:, jj * LANES:(jj + 1) * LANES] * pair_expand(e_acs, j)
                            + dskip_ref[:, ps] * xs_pair)
        state_fn(g, cd_rows)

    for g in range(SSM_GROUPS):
        gs = slice(g * GROUP_W, (g + 1) * GROUP_W)
        gg = y_ref[:, gs] * _silu(z_ref[:, gs])
        ms = jnp.mean(gg * gg, axis=-1, keepdims=True)
        g_ref[:, gs] = (gg * lax.rsqrt(ms + RMS_EPS) * nw_ref[:, gs]).astype(BF16)


def _ssd_prompt_kernel(z_ref, xs_ref, bc_ref, dt_ref, cw_ref, cb_ref, dtb_ref, alog_ref, dskip_ref, nw_ref,
                       g_ref, st_ref, ext_ref, xc_ref, xd_ref, y_ref, stt_ref):
    c = pl.program_id(1)
    rows = SSD_CHUNK

    @pl.when(c == 0)
    def _():
        ext_ref[0:SUBLANES, :] = jnp.zeros((SUBLANES, CONV_DIM), F32)
        stt_ref[...] = jnp.zeros(stt_ref.shape, F32)

    @pl.when(c != 0)
    def _():
        ext_ref[0:SUBLANES, :] = ext_ref[rows:rows + SUBLANES, :]

    ext_ref[SUBLANES:SUBLANES + rows, 0:D_INNER] = xs_ref[...]
    ext_ref[SUBLANES:SUBLANES + rows, D_INNER:CONV_DIM] = bc_ref[...]
    _conv_silu(ext_ref, SUBLANES, rows, cw_ref, cb_ref, xc_ref, 0)

    def yoff_fn(g, c_g):
        return jnp.dot(c_g, stt_ref[:, g * GROUP_W:(g + 1) * GROUP_W].astype(BF16), preferred_element_type=F32)

    def state_fn(g, cd_rows):
        b_g = xc_ref[:, D_INNER + g * SSM_STATE:D_INNER + (g + 1) * SSM_STATE].astype(BF16)
        xd_g = xd_ref[:, g * GROUP_W:(g + 1) * GROUP_W]
        new = lax.dot_general(b_g, xd_g, (((0,), (0,)), ((), ())), preferred_element_type=F32)
        lane = lax.broadcasted_iota(jnp.int32, (SSM_STATE, LANES), 1)
        for jj in range(GROUP_W // LANES):
            j = g * (GROUP_W // LANES) + jj
            ps = slice(j * LANES, (j + 1) * LANES)
            cd = jnp.where(lane < SSM_HEAD_DIM,
                           jnp.broadcast_to(cd_rows[0:1, 2 * j:2 * j + 1], (SSM_STATE, LANES)),
                           jnp.broadcast_to(cd_rows[0:1, 2 * j + 1:2 * j + 2], (SSM_STATE, LANES)))
            stt_ref[:, ps] = stt_ref[:, ps] * cd + new[:, jj * LANES:(jj + 1) * LANES]

    same_seq = jnp.full((rows, rows), True)
    _ssd_core(rows, same_seq, z_ref, dt_ref, dtb_ref, alog_ref, dskip_ref, nw_ref, xc_ref, xd_ref, y_ref,
              yoff_fn, state_fn, g_ref)

    @pl.when(c == pl.num_programs(1) - 1)
    def _():
        for j in range(D_INNER // LANES):
            st_ref[0, j * LANES:(j + 1) * LANES, :] = stt_ref[:, j * LANES:(j + 1) * LANES].T


def _ssd_prompt(u, dt_raw, conv_w, conv_b, dt_bias, a_log, dskip_cols, ssm_norm_w, batch, seq):
    nc = seq // SSD_CHUNK
    rows = SSD_CHUNK
    row_blk = lambda b, c: b * nc + c
    full = lambda shape: pl.BlockSpec(shape, lambda b, c: (0,) * len(shape))
    return pl.pallas_call(
        _ssd_prompt_kernel,
        grid=(batch, nc),
        in_specs=[pl.BlockSpec((rows, D_INNER), lambda b, c: (row_blk(b, c), _COL.Z // D_INNER)),
                  pl.BlockSpec((rows, D_INNER), lambda b, c: (row_blk(b, c), _COL.XS // D_INNER)),
                  pl.BlockSpec((rows, 2 * GN), lambda b, c: (row_blk(b, c), _COL.BC // (2 * GN))),
                  pl.BlockSpec((rows, DT_PAD), lambda b, c: (row_blk(b, c), 0)),
                  full((CONV_WIDTH, CONV_DIM)), full((1, CONV_DIM)), full((1, DT_PAD)), full((1, DT_PAD)),
                  full((1, D_INNER)), full((1, D_INNER))],
        out_specs=[pl.BlockSpec((rows, D_INNER), lambda b, c: (row_blk(b, c), 0)),
                   pl.BlockSpec((1, D_INNER, SSM_STATE), lambda b, c: (b, 0, 0))],
        out_shape=[jax.ShapeDtypeStruct((batch * seq, D_INNER), BF16),
                   jax.ShapeDtypeStruct((batch, D_INNER, SSM_STATE), F32)],
        scratch_shapes=[pltpu.VMEM((rows + SUBLANES, CONV_DIM), F32),
                        pltpu.VMEM((rows, CONV_DIM), F32),
                        pltpu.VMEM((rows, D_INNER), BF16),
                        pltpu.VMEM((rows, D_INNER), F32),
                        pltpu.VMEM((SSM_STATE, D_INNER), F32)],
        compiler_params=_cparams(2),
        name="ssd_prompt",
    )(u, u, u, dt_raw, conv_w, conv_b, dt_bias, a_log, dskip_cols, ssm_norm_w)


SAMPLE_BT = 4


def _ssd_sample_kernel(z_ref, xc_ref, dt_ref, st_ref, dtb_ref, alog_ref, dskip_ref, nw_ref,
                       g_ref, nst_ref, xd_ref, y_ref, yoff_ref, seq_len):
    bt = SAMPLE_BT
    rows = bt * seq_len
    heads_per_group = SSM_HEADS // SSM_GROUPS

    for b in range(bt):
        rs = slice(b * seq_len, (b + 1) * seq_len)
        for g in range(SSM_GROUPS):
            c_bg = xc_ref[rs, D_INNER + GN + g * SSM_STATE:D_INNER + GN + (g + 1) * SSM_STATE].astype(BF16)
            s_bg = st_ref[b, g * GROUP_W:(g + 1) * GROUP_W, :].astype(BF16)
            yoff_ref[rs, g * GROUP_W:(g + 1) * GROUP_W] = _dot_nt(c_bg, s_bg)

    def yoff_fn(g, c_g):
        return yoff_ref[:, g * GROUP_W:(g + 1) * GROUP_W]

    def state_fn(g, cd_rows):
        for b in range(bt):
            rs = slice(b * seq_len, (b + 1) * seq_len)
            b_bg = xc_ref[rs, D_INNER + g * SSM_STATE:D_INNER + (g + 1) * SSM_STATE].astype(BF16)
            xd_bg = xd_ref[rs, g * GROUP_W:(g + 1) * GROUP_W].astype(BF16)
            new = lax.dot_general(xd_bg, b_bg, (((0,), (0,)), ((), ())), preferred_element_type=F32)
            for hh in range(heads_per_group):
                h = g * heads_per_group + hh
                hs = slice(h * SSM_HEAD_DIM, (h + 1) * SSM_HEAD_DIM)
                cd = jnp.broadcast_to(cd_rows[b * seq_len:b * seq_len + 1, h:h + 1], (SSM_HEAD_DIM, SSM_STATE))
                nst_ref[b, hs, :] = st_ref[b, hs, :] * cd + new[hh * SSM_HEAD_DIM:(hh + 1) * SSM_HEAD_DIM, :]

    li = lax.broadcasted_iota(jnp.int32, (rows, rows), 0)
    si = lax.broadcasted_iota(jnp.int32, (rows, rows), 1)
    same_seq = (li // seq_len) == (si // seq_len)
    _ssd_core(rows, same_seq, z_ref, dt_ref, dtb_ref, alog_ref, dskip_ref, nw_ref, xc_ref, xd_ref, y_ref,
              yoff_fn, state_fn, g_ref)


def _ssd_sample(u, dt_raw, xc, state, dt_bias, a_log, dskip_cols, ssm_norm_w, row0, batch, seq):
    bt = SAMPLE_BT
    rows = bt * seq
    assert row0 % rows == 0
    blk0 = row0 // rows
    full = lambda shape: pl.BlockSpec(shape, lambda i: (0,) * len(shape))
    return pl.pallas_call(
        functools.partial(_ssd_sample_kernel, seq_len=seq),
        grid=(batch // bt,),
        in_specs=[pl.BlockSpec((rows, D_INNER), lambda i: (blk0 + i, _COL.Z // D_INNER)),
                  pl.BlockSpec((rows, CONV_DIM), lambda i: (i, 0)),
                  pl.BlockSpec((rows, DT_PAD), lambda i: (blk0 + i, 0)),
                  pl.BlockSpec((bt, D_INNER, SSM_STATE), lambda i: (i, 0, 0)),
                  full((1, DT_PAD)), full((1, DT_PAD)), full((1, D_INNER)), full((1, D_INNER))],
        out_specs=[pl.BlockSpec((rows, D_INNER), lambda i: (i, 0)),
                   pl.BlockSpec((bt, D_INNER, SSM_STATE), lambda i: (i, 0, 0))],
        out_shape=[jax.ShapeDtypeStruct((batch * seq, D_INNER), BF16),
                   jax.ShapeDtypeStruct((batch, D_INNER, SSM_STATE), F32)],
        scratch_shapes=[pltpu.VMEM((rows, D_INNER), F32),
                        pltpu.VMEM((rows, D_INNER), F32),
                        pltpu.VMEM((rows, D_INNER), F32)],
        compiler_params=_cparams(1),
        name="ssd_sample",
    )(u, xc, dt_raw, state, dt_bias, a_log, dskip_cols, ssm_norm_w)


ATTN_SCALE = HEAD_DIM ** -0.5


def _attn_prompt_kernel(sink_ref, q_ref, kp_ref, ko_ref, vp_ref, vo_ref, ga_ref, bias_ref, o_ref):
    n = pl.program_id(1)
    no_prev = jnp.where(n == 0, jnp.inf, 0.0)

    def scores(g):
        ks = slice(g * HEAD_DIM, (g + 1) * HEAD_DIM)
        kk = jnp.concatenate([kp_ref[:, ks], ko_ref[:, ks]], axis=0).astype(BF16)
        vv_t = jnp.concatenate([vp_ref[:, ks], vo_ref[:, ks]], axis=0).T.astype(BF16)
        s_list = []
        for r in range(REP):
            h = g * REP + r
            q = (q_ref[:, h * HEAD_DIM:(h + 1) * HEAD_DIM] * ATTN_SCALE).astype(BF16)
            s_list.append(_dot_nt(kk, q))
        return vv_t, s_list

    def finish(g, vv_t, s_list):
        probs = []
        for r in range(REP):
            h = g * REP + r
            s_prev = s_list[r][0:WINDOW] + bias_ref[h, 0:WINDOW, :]
            s_own = s_list[r][WINDOW:2 * WINDOW] + bias_ref[h, WINDOW:2 * WINDOW, :]
            sink = sink_ref[h]
            m = jnp.maximum(jnp.maximum(jnp.max(s_prev, axis=0, keepdims=True) - no_prev,
                                        jnp.max(s_own, axis=0, keepdims=True)), sink)
            p = jnp.concatenate([jnp.exp(s_prev - (m + no_prev)), jnp.exp(s_own - m)], axis=0)
            denom = jnp.sum(p, axis=0, keepdims=True) + jnp.exp(sink - m)
            probs.append((p.astype(BF16), denom))
        o_t = [jnp.dot(vv_t, p, preferred_element_type=F32) / denom for p, denom in probs]
        for rp in range(REP // 2):
            o_pair = jnp.concatenate(o_t[2 * rp:2 * rp + 2], axis=0).T
            ps = slice((g * REP + 2 * rp) * HEAD_DIM, (g * REP + 2 * rp + 2) * HEAD_DIM)
            o_ref[:, ps] = (o_pair * _silu(ga_ref[:, ps])).astype(BF16)

    pending = scores(0)
    for g in range(KV_HEADS):
        cur = pending
        if g + 1 < KV_HEADS:
            pending = scores(g + 1)
        finish(g, *cur)


def _attn_prompt(u, sinks, bias, batch, seq):
    nb = seq // WINDOW
    row = lambda b, n: b * nb + n
    prev = lambda b, n: jnp.maximum(b * nb + n - 1, 0)
    return pl.pallas_call(
        _attn_prompt_kernel,
        grid=(batch, nb),
        in_specs=[pl.BlockSpec(memory_space=pltpu.SMEM),
                  pl.BlockSpec((WINDOW, ATTN_WIDTH), lambda b, n: (row(b, n), _COL.Q // ATTN_WIDTH)),
                  pl.BlockSpec((WINDOW, KV_WIDTH), lambda b, n: (prev(b, n), _COL.K // KV_WIDTH)),
                  pl.BlockSpec((WINDOW, KV_WIDTH), lambda b, n: (row(b, n), _COL.K // KV_WIDTH)),
                  pl.BlockSpec((WINDOW, KV_WIDTH), lambda b, n: (prev(b, n), _COL.V // KV_WIDTH)),
                  pl.BlockSpec((WINDOW, KV_WIDTH), lambda b, n: (row(b, n), _COL.V // KV_WIDTH)),
                  pl.BlockSpec((WINDOW, ATTN_WIDTH), lambda b, n: (row(b, n), _COL.GA // ATTN_WIDTH)),
                  pl.BlockSpec((Q_HEADS, 2 * WINDOW, WINDOW), lambda b, n: (0, 0, 0))],
        out_specs=pl.BlockSpec((WINDOW, ATTN_WIDTH), lambda b, n: (row(b, n), 0)),
        out_shape=jax.ShapeDtypeStruct((batch * seq, ATTN_WIDTH), BF16),
        compiler_params=_cparams(2),
        name="attn_prompt",
    )(sinks, u, u, u, u, u, u, bias)


def _attn_sample_kernel(sink_ref, q_ref, ga_ref, kvt_ref, ck_ref, cv_ref, bias_ref, o_ref, nk_ref, nv_ref, seq_len):
    bt = SAMPLE_BT
    w_buf = ck_ref.shape[-1]
    keep = w_buf - seq_len
    step_lanes = bt * seq_len
    lane = lax.broadcasted_iota(jnp.int32, (HEAD_DIM, w_buf), 1)
    lane0 = (pl.program_id(0) * step_lanes) % kvt_ref.shape[1]

    def start(b, g):
        new_shift = (keep + w_buf - (lane0 + b * seq_len)) % w_buf
        tiles = []
        for c_ref, n_ref, row0 in ((ck_ref, nk_ref, 0), (cv_ref, nv_ref, KV_WIDTH)):
            old = c_ref[b, g]
            fresh = kvt_ref[row0 + g * HEAD_DIM:row0 + (g + 1) * HEAD_DIM, :]
            new = jnp.where(lane < keep, pltpu.roll(old, keep, axis=1), pltpu.roll(fresh, new_shift, axis=1))
            n_ref[b, g] = new
            tiles.append(jnp.concatenate([old, new], axis=1).astype(BF16))
        rs = slice(b * seq_len, (b + 1) * seq_len)
        qs = jnp.concatenate([q_ref[rs, (g * REP + r) * HEAD_DIM:(g * REP + r + 1) * HEAD_DIM] for r in range(REP)],
                             axis=0)
        s = jnp.dot((qs * ATTN_SCALE).astype(BF16), tiles[0], preferred_element_type=F32)
        return s, tiles[1]

    def finish(b, g, s, v_both):
        s = s + bias_ref[g]
        sink = jnp.concatenate([jnp.full((seq_len, 1), sink_ref[g * REP + r], F32) for r in range(REP)], axis=0)
        m = jnp.maximum(jnp.max(s, axis=-1, keepdims=True), sink)
        p = jnp.exp(s - m)
        denom = jnp.sum(p, axis=-1, keepdims=True) + jnp.exp(sink - m)
        o = _dot_nt(p.astype(BF16), v_both) / denom
        rs = slice(b * seq_len, (b + 1) * seq_len)
        for r in range(REP):
            hs = slice((g * REP + r) * HEAD_DIM, (g * REP + r + 1) * HEAD_DIM)
            o_ref[rs, hs] = (o[r * seq_len:(r + 1) * seq_len, :] * _silu(ga_ref[rs, hs])).astype(BF16)

    units = [(b, g) for b in range(bt) for g in range(KV_HEADS)]
    pending = start(*units[0])
    for idx, unit in enumerate(units):
        cur = pending
        if idx + 1 < len(units):
            pending = start(*units[idx + 1])
        finish(*unit, *cur)


def _attn_sample(u, sinks, kv_t, cache_k, cache_v, bias, row0, batch, seq):
    bt = SAMPLE_BT
    rows = bt * seq
    w_buf = cache_k.shape[-1]
    assert w_buf == LANES and row0 % rows == 0 and LANES % rows == 0
    blk0 = row0 // rows
    cache_spec = pl.BlockSpec((bt, KV_HEADS, HEAD_DIM, w_buf), lambda i: (i, 0, 0, 0))
    return pl.pallas_call(
        functools.partial(_attn_sample_kernel, seq_len=seq),
        grid=(batch // bt,),
        in_specs=[pl.BlockSpec(memory_space=pltpu.SMEM),
                  pl.BlockSpec((rows, ATTN_WIDTH), lambda i: (blk0 + i, _COL.Q // ATTN_WIDTH)),
                  pl.BlockSpec((rows, ATTN_WIDTH), lambda i: (blk0 + i, _COL.GA // ATTN_WIDTH)),
                  pl.BlockSpec((2 * KV_WIDTH, LANES), lambda i: (0, i * rows // LANES)),
                  cache_spec, cache_spec,
                  pl.BlockSpec(bias.shape, lambda i: (0, 0, 0))],
        out_specs=[pl.BlockSpec((rows, ATTN_WIDTH), lambda i: (i, 0)), cache_spec, cache_spec],
        out_shape=[jax.ShapeDtypeStruct((batch * seq, ATTN_WIDTH), BF16),
                   jax.ShapeDtypeStruct(cache_k.shape, F32),
                   jax.ShapeDtypeStruct(cache_v.shape, F32)],
        compiler_params=_cparams(1),
        name="attn_sample",
    )(sinks, u, u, kv_t, cache_k, cache_v, bias)


MERGE_TN = 256


def _merge_kernel(g_ref, og_ref, ms_ref, ma_ref, ws_ref, wa_ref, o_ref):
    p_ssm = jnp.dot(g_ref[...], ws_ref[...], preferred_element_type=F32)
    p_attn = jnp.dot(og_ref[...], wa_ref[...], preferred_element_type=F32)
    o_ref[...] = (_sigmoid(ms_ref[...]) * p_ssm + _sigmoid(ma_ref[...]) * p_attn).astype(BF16)


def _merge(g, og, u, w_ssm, w_attn, row0, tm):
    t = g.shape[0]
    tn = MERGE_TN
    assert row0 % tm == 0
    blk0 = row0 // tm
    return pl.pallas_call(
        _merge_kernel,
        grid=(t // tm, D_MODEL // tn),
        in_specs=[pl.BlockSpec((tm, D_INNER), lambda i, j: (i, 0)),
                  pl.BlockSpec((tm, ATTN_WIDTH), lambda i, j: (i, 0)),
                  pl.BlockSpec((tm, tn), lambda i, j: (blk0 + i, _COL.MS // tn + j)),
                  pl.BlockSpec((tm, tn), lambda i, j: (blk0 + i, _COL.MA // tn + j)),
                  pl.BlockSpec((D_INNER, tn), lambda i, j: (0, j)),
                  pl.BlockSpec((ATTN_WIDTH, tn), lambda i, j: (0, j))],
        out_specs=pl.BlockSpec((tm, tn), lambda i, j: (i, j)),
        out_shape=jax.ShapeDtypeStruct((t, D_MODEL), BF16),
        compiler_params=_cparams(2),
        name="merge",
    )(g, og, u, u, w_ssm, w_attn)


def _out_proj_kernel(m_ref, x_ref, wo_ref, fw_ref, o_ref):
    xn = x_ref[...] + jnp.dot(m_ref[...], wo_ref[...], preferred_element_type=F32)
    ms = jnp.mean(xn * xn, axis=-1, keepdims=True)
    o_ref[...] = xn * lax.rsqrt(ms + RMS_EPS) * fw_ref[...]


def _out_proj(m, x2d, w_out, final_w, tm):
    t = m.shape[0]
    return pl.pallas_call(
        _out_proj_kernel,
        grid=(t // tm,),
        in_specs=[pl.BlockSpec((tm, D_MODEL), lambda i: (i, 0)),
                  pl.BlockSpec((tm, D_MODEL), lambda i: (i, 0)),
                  pl.BlockSpec((D_MODEL, D_MODEL), lambda i: (0, 0)),
                  pl.BlockSpec((1, D_MODEL), lambda i: (0, 0))],
        out_specs=pl.BlockSpec((tm, D_MODEL), lambda i: (i, 0)),
        out_shape=jax.ShapeDtypeStruct((t, D_MODEL), F32),
        compiler_params=_cparams(1),
        name="out_proj",
    )(m, x2d, w_out, final_w)


def _pad_lanes(v):
    return jnp.pad(v.reshape(1, -1), ((0, 0), (0, DT_PAD - v.shape[-1])))


def kernel(x_prompt, x_sample, cache_k, cache_v, state_conv, state_ssm, norm_w, w_in, conv_w, conv_b, dt_bias, a_log,
           d_skip, ssm_norm_w, w_ssm_branch, attn_sinks, w_attn_branch, w_out, rel_bias, final_norm_w):
    assert w_in.shape[0] == 1, "single-layer kernel"
    batch, seq, _ = x_prompt.shape
    dec_batch, dec_seq, _ = x_sample.shape
    w_buf = cache_k.shape[2]
    t_p, t_s = batch * seq, dec_batch * dec_seq
    assert seq % SSD_CHUNK == 0 and seq % WINDOW == 0 and dec_seq == SUBLANES and w_buf == WINDOW
    assert dec_batch % SAMPLE_BT == 0 and t_p % IN_TM == 0 and t_s % IN_TM == 0

    w_t = jnp.transpose(w_in[0])
    sc_t = jnp.transpose(state_conv[0], (1, 0, 2))
    ck_t = jnp.transpose(cache_k[0], (0, 2, 3, 1))
    cv_t = jnp.transpose(cache_v[0], (0, 2, 3, 1))

    nw = norm_w[0].reshape(1, D_MODEL)
    cw, cb = conv_w[0], conv_b[0].reshape(1, CONV_DIM)
    dtb, alog = _pad_lanes(dt_bias[0]), _pad_lanes(a_log[0])
    dskip_cols = jnp.repeat(d_skip[0], SSM_HEAD_DIM).reshape(1, D_INNER)
    snw = ssm_norm_w[0].reshape(1, D_INNER)
    w_ssm = w_ssm_branch[0].astype(BF16)
    w_attn = w_attn_branch[0].astype(BF16)
    wo = w_out[0].astype(BF16)
    fw = final_norm_w.reshape(1, D_MODEL)
    sinks = attn_sinks[0]

    xp2 = x_prompt.reshape(t_p, D_MODEL)
    xs2 = x_sample.reshape(t_s, D_MODEL)

    band_bias_t = _band_bias_t(rel_bias)
    dec_bias = _decode_bias(rel_bias, dec_seq, w_buf)

    h, dt_raw = _norm_dt(xp2, xs2, nw, w_t)
    u = _in_proj(h, w_t)
    kv_t = _kv_t(h, w_t, t_p, t_s)

    g_p, st_p = _ssd_prompt(u, dt_raw, cw, cb, dtb, alog, dskip_cols, snw, batch, seq)
    og_p = _attn_prompt(u, sinks, band_bias_t, batch, seq)
    m_p = _merge(g_p, og_p, u, w_ssm, w_attn, 0, tm=1024)
    y_p = _out_proj(m_p, xp2, wo, fw, tm=512)

    xc_s, nsc_t = _conv_sample(u, sc_t, cw, cb, t_p, dec_batch, dec_seq)
    g_s, st_s = _ssd_sample(u, dt_raw, xc_s, state_ssm[0].reshape(dec_batch, D_INNER, SSM_STATE),
                            dtb, alog, dskip_cols, snw, t_p, dec_batch, dec_seq)
    og_s, nk_t, nv_t = _attn_sample(u, sinks, kv_t, ck_t, cv_t, dec_bias, t_p, dec_batch, dec_seq)
    m_s = _merge(g_s, og_s, u, w_ssm, w_attn, t_p, tm=1024)
    y_s = _out_proj(m_s, xs2, wo, fw, tm=512)

    def seq_tail(n_rows, col0, width):
        return jnp.stack([u[(b + 1) * seq - n_rows:(b + 1) * seq, col0:col0 + width] for b in range(batch)])

    k_p = seq_tail(WINDOW, _COL.K, KV_WIDTH).reshape(1, batch, WINDOW, KV_HEADS, HEAD_DIM)
    v_p = seq_tail(WINDOW, _COL.V, KV_WIDTH).reshape(1, batch, WINDOW, KV_HEADS, HEAD_DIM)
    conv_p = seq_tail(CONV_HALO, _COL.XS, CONV_DIM)[None]
    conv_s = jnp.transpose(nsc_t, (1, 0, 2))[None]
    ssm_p = st_p.reshape(1, batch, SSM_HEADS, SSM_HEAD_DIM, SSM_STATE)
    ssm_s = st_s.reshape(1, dec_batch, SSM_HEADS, SSM_HEAD_DIM, SSM_STATE)
    k_s = jnp.transpose(nk_t, (0, 3, 1, 2))[None]
    v_s = jnp.transpose(nv_t, (0, 3, 1, 2))[None]
    return (y_p.reshape(batch, seq, D_MODEL), y_s.reshape(dec_batch, dec_seq, D_MODEL),
            k_p, v_p, conv_p, ssm_p, k_s, v_s, conv_s, ssm_s)
```

```python
import functools
import math

import numpy as np
import jax
import jax.numpy as jnp
from jax import lax
from jax.experimental import pallas as pl
from jax.experimental.pallas import tpu as pltpu

F32 = jnp.float32
BF16 = jnp.bfloat16
HIGHEST = lax.Precision.HIGHEST

D_MODEL = 2048
D_INNER = 4096
SSM_HEAD_DIM = 64
SSM_HEADS = 64
SSM_GROUPS = 8
SSM_STATE = 128
CONV_WIDTH = 4
CONV_HALO = CONV_WIDTH - 1
GN = SSM_GROUPS * SSM_STATE
CONV_DIM = D_INNER + 2 * GN
SSD_CHUNK = 128
GROUP_W = D_INNER // SSM_GROUPS
HEAD_DIM = 64
Q_HEADS = 32
KV_HEADS = 8
REP = Q_HEADS // KV_HEADS
ATTN_WIDTH = Q_HEADS * HEAD_DIM
KV_WIDTH = KV_HEADS * HEAD_DIM
WINDOW = 128
N_BUCKETS = 32
MAX_EXACT = N_BUCKETS // 2
RMS_EPS = 1e-6

_OFF_Z = 0
_OFF_XBC = _OFF_Z + D_INNER
_OFF_DT = _OFF_XBC + CONV_DIM
_OFF_Q = _OFF_DT + SSM_HEADS
_OFF_K = _OFF_Q + ATTN_WIDTH
_OFF_V = _OFF_K + KV_WIDTH
_OFF_GA = _OFF_V + KV_WIDTH
_OFF_MS = _OFF_GA + ATTN_WIDTH
_OFF_MA = _OFF_MS + D_MODEL
_IN_COLS = _OFF_MA + D_MODEL

LANES = 128
SUBLANES = 8
DT_PAD = LANES


class _COL:
    Z = 0
    XS = Z + D_INNER
    BC = XS + D_INNER
    Q = BC + 2 * GN
    GA = Q + ATTN_WIDTH
    MS = GA + ATTN_WIDTH
    MA = MS + D_MODEL
    K = MA + D_MODEL
    V = K + KV_WIDTH
    TOTAL = V + KV_WIDTH


VMEM_LIMIT = 52 * 1024 * 1024


def _cparams(n_grid):
    return pltpu.CompilerParams(dimension_semantics=("arbitrary",) * n_grid, vmem_limit_bytes=VMEM_LIMIT)


def _silu(v):
    return v * (1.0 / (1.0 + jnp.exp(-v)))


def _sigmoid(v):
    return 1.0 / (1.0 + jnp.exp(-v))


def _dot_nt(a, b):
    return lax.dot_general(a, b, (((1,), (1,)), ((), ())), preferred_element_type=F32)


NORM_TM = 512


def _norm_dt_kernel(xp_ref, xs_ref, nw_ref, wdt_ref, h_ref, dt_ref, n_prompt_tiles):
    i = pl.program_id(0)

    def emit(x_ref):
        xf = x_ref[...]
        ms = jnp.mean(xf * xf, axis=-1, keepdims=True)
        h = (xf * lax.rsqrt(ms + RMS_EPS) * nw_ref[...]).astype(BF16)
        h_ref[...] = h
        dt_ref[...] = _dot_nt(h, wdt_ref[...].astype(BF16))

    @pl.when(i < n_prompt_tiles)
    def _():
        emit(xp_ref)

    @pl.when(i >= n_prompt_tiles)
    def _():
        emit(xs_ref)


def _norm_dt(xp2, xs2, norm_w, w_t):
    tp, ts = xp2.shape[0], xs2.shape[0]
    n_p, n_s = tp // NORM_TM, ts // NORM_TM
    return pl.pallas_call(
        functools.partial(_norm_dt_kernel, n_prompt_tiles=n_p),
        grid=(n_p + n_s,),
        in_specs=[pl.BlockSpec((NORM_TM, D_MODEL), lambda i: (jnp.minimum(i, n_p - 1), 0)),
                  pl.BlockSpec((NORM_TM, D_MODEL), lambda i: (jnp.maximum(i - n_p, 0), 0)),
                  pl.BlockSpec((1, D_MODEL), lambda i: (0, 0)),
                  pl.BlockSpec((DT_PAD, D_MODEL), lambda i: (_OFF_DT // DT_PAD, 0))],
        out_specs=[pl.BlockSpec((NORM_TM, D_MODEL), lambda i: (i, 0)),
                   pl.BlockSpec((NORM_TM, DT_PAD), lambda i: (i, 0))],
        out_shape=[jax.ShapeDtypeStruct((tp + ts, D_MODEL), BF16),
                   jax.ShapeDtypeStruct((tp + ts, DT_PAD), F32)],
        compiler_params=_cparams(1),
        name="norm_dt",
    )(xp2, xs2, norm_w, w_t)


IN_TM = 1024
IN_TN = 1024
IN_CHUNK = 512
assert _COL.TOTAL % IN_TN == 0 and IN_TN == 2 * IN_CHUNK and _OFF_DT % IN_CHUNK == 0


def _src_row(c):
    straight = _OFF_DT // IN_CHUNK
    jj = c - straight
    q_t, kv_t = ATTN_WIDTH // IN_CHUNK, 2 * KV_WIDTH // IN_CHUNK
    gate_t = (ATTN_WIDTH + 2 * D_MODEL) // IN_CHUNK
    shifted = jnp.where(jj < q_t, jj, jnp.where(jj < q_t + gate_t, jj + kv_t, jj - gate_t))
    return pl.multiple_of(jnp.where(c < straight, c * IN_CHUNK, _OFF_Q + shifted * IN_CHUNK), SSM_HEADS)


IN_NSPLIT = 4
IN_MW = IN_TM // IN_NSPLIT
J_Z1 = _COL.XS // IN_TN
J_GA0, J_GA1 = _COL.GA // IN_TN, _COL.MS // IN_TN
assert _COL.XS % IN_TN == 0 and _COL.GA % IN_TN == 0 and _COL.MS % IN_TN == 0


def _in_proj_kernel(h_ref, w0_ref, w1_ref, o_ref, wbf_ref, *stage_refs):
    j = pl.program_id(0)

    @pl.when(pl.program_id(1) == 0)
    def _():
        wbf_ref[0:IN_CHUNK, :] = w0_ref[...].astype(BF16)
        wbf_ref[IN_CHUNK:IN_TN, :] = w1_ref[...].astype(BF16)

    is_silu_tile = (j < J_Z1) | ((j >= J_GA0) & (j < J_GA1))

    @pl.when(is_silu_tile)
    def _():
        def silu_chunk(n):
            o_ref[n * IN_MW:(n + 1) * IN_MW, :] = _silu(stage_refs[n][...])

        for n in range(IN_NSPLIT):
            stage_refs[n][...] = _dot_nt(h_ref[n * IN_MW:(n + 1) * IN_MW, :], wbf_ref[...])
            if n > 0:
                silu_chunk(n - 1)
        silu_chunk(IN_NSPLIT - 1)

    @pl.when(jnp.logical_not(is_silu_tile))
    def _():
        o_ref[...] = _dot_nt(h_ref[...], wbf_ref[...])


def _in_proj(h, w_t):
    t = h.shape[0]
    return pl.pallas_call(
        _in_proj_kernel,
        grid=(_COL.TOTAL // IN_TN, t // IN_TM),
        in_specs=[pl.BlockSpec((IN_TM, D_MODEL), lambda j, i: (i, 0)),
                  pl.BlockSpec((pl.Element(IN_CHUNK), pl.Element(D_MODEL)), lambda j, i: (_src_row(2 * j), 0)),
                  pl.BlockSpec((pl.Element(IN_CHUNK), pl.Element(D_MODEL)), lambda j, i: (_src_row(2 * j + 1), 0))],
        out_specs=pl.BlockSpec((IN_TM, IN_TN), lambda j, i: (i, j)),
        out_shape=jax.ShapeDtypeStruct((t, _COL.TOTAL), F32),
        scratch_shapes=[pltpu.VMEM((IN_TN, D_MODEL), BF16)]
        + [pltpu.VMEM((IN_MW, IN_TN), F32) for _ in range(IN_NSPLIT)],
        compiler_params=_cparams(2),
        name="in_proj",
    )(h, w_t, w_t)


def _kv_t_kernel(w_ref, h_ref, o_ref):
    o_ref[...] = _dot_nt(w_ref[...].astype(BF16), h_ref[...])


def _kv_t(h, w_t, row0, rows):
    assert row0 % rows == 0
    return pl.pallas_call(
        _kv_t_kernel,
        grid=(1,),
        in_specs=[pl.BlockSpec((pl.Element(2 * KV_WIDTH), pl.Element(D_MODEL)), lambda i: (_OFF_K, 0)),
                  pl.BlockSpec((rows, D_MODEL), lambda i: (row0 // rows, 0))],
        out_specs=pl.BlockSpec((2 * KV_WIDTH, rows), lambda i: (0, 0)),
        out_shape=jax.ShapeDtypeStruct((2 * KV_WIDTH, rows), F32),
        compiler_params=_cparams(1),
        name="kv_t",
    )(w_t, h)


def _bucket_of_dist(dist):
    n = np.maximum(dist, 0)
    nf = np.maximum(n, 1).astype(np.float32)
    large = MAX_EXACT + (np.log(nf / MAX_EXACT) / math.log(WINDOW / MAX_EXACT)
                         * (N_BUCKETS - MAX_EXACT)).astype(np.int32)
    large = np.minimum(large, N_BUCKETS - 1)
    bucket = np.where(n < MAX_EXACT, n, large)
    return np.where((dist >= 0) & (dist <= WINDOW), bucket, -1).astype(np.int32)


def _bias_of_bucket(bk, table_ref, h):
    acc = jnp.full(bk.shape, -jnp.inf, F32)
    for b in range(N_BUCKETS):
        acc = jnp.where(bk == b, table_ref[b, h], acc)
    return acc


def _band_bias_kernel(table_ref, bucket_ref, o_ref):
    o_ref[0] = _bias_of_bucket(bucket_ref[...], table_ref, pl.program_id(0))


def _band_bias_t(rel_bias):
    kj = np.arange(2 * WINDOW)[:, None]
    qi = np.arange(WINDOW)[None, :]
    bucket = jnp.asarray(_bucket_of_dist(qi + WINDOW - kj))
    return pl.pallas_call(
        _band_bias_kernel,
        grid=(Q_HEADS,),
        in_specs=[pl.BlockSpec(memory_space=pltpu.SMEM),
                  pl.BlockSpec((2 * WINDOW, WINDOW), lambda h: (0, 0))],
        out_specs=pl.BlockSpec((1, 2 * WINDOW, WINDOW), lambda h: (h, 0, 0)),
        out_shape=jax.ShapeDtypeStruct((Q_HEADS, 2 * WINDOW, WINDOW), F32),
        compiler_params=_cparams(1),
        name="band_bias",
    )(rel_bias, bucket)


def _decode_bias_kernel(table_ref, sink_ref, bucket_ref, o_ref, seq):
    g = pl.program_id(0)
    sink_lane = lax.broadcasted_iota(jnp.int32, bucket_ref.shape, 1) == seq
    for r in range(REP):
        h = g * REP + r
        bias = _bias_of_bucket(bucket_ref[...], table_ref, h)
        o_ref[r * seq:(r + 1) * seq, :] = jnp.where(sink_lane, sink_ref[h], bias)


def _decode_bias(rel_bias, sinks, seq, w_buf):
    l = np.arange(seq)[:, None]
    j = np.arange(w_buf)[None, :]
    dist_old = np.where((j < seq), l + w_buf - j, -1)
    dist_new = np.where(j < w_buf - seq, l + w_buf - (j + seq), l - (j - (w_buf - seq)))
    bucket = jnp.asarray(np.concatenate([_bucket_of_dist(dist_old), _bucket_of_dist(dist_new)], axis=1))
    return pl.pallas_call(
        functools.partial(_decode_bias_kernel, seq=seq),
        grid=(KV_HEADS,),
        in_specs=[pl.BlockSpec(memory_space=pltpu.SMEM), pl.BlockSpec(memory_space=pltpu.SMEM),
                  pl.BlockSpec(bucket.shape, lambda g: (0, 0))],
        out_specs=pl.BlockSpec((REP * seq, 2 * w_buf), lambda g: (g, 0)),
        out_shape=jax.ShapeDtypeStruct((Q_HEADS * seq, 2 * w_buf), F32),
        compiler_params=_cparams(1),
        name="decode_bias",
    )(rel_bias, sinks, bucket)


CONV_STRIP = 512


def _conv_silu(ext_ref, base, rows, cw_ref, cb_ref, xc_ref, out_row):
    for c0 in range(0, CONV_DIM, CONV_STRIP):
        cs = slice(c0, c0 + CONV_STRIP)
        acc = cb_ref[:, cs] + cw_ref[3:4, cs] * ext_ref[base:base + rows, cs]
        for k in range(1, CONV_WIDTH):
            acc = acc + cw_ref[3 - k:4 - k, cs] * ext_ref[base - k:base - k + rows, cs]
        xc_ref[out_row:out_row + rows, cs] = _silu(acc)


CONV_S_SEQS = 16


def _conv_sample_kernel(xs_ref, bc_ref, sc_ref, cw_ref, cb_ref, xc_ref, nsc_ref, ext_ref, seq_len):
    slab = SUBLANES + seq_len
    for b in range(CONV_S_SEQS):
        base = b * slab + SUBLANES
        for k in range(CONV_HALO):
            ext_ref[base - CONV_HALO + k:base - CONV_HALO + k + 1, :] = sc_ref[k, b:b + 1, :]
        ext_ref[base:base + seq_len, 0:D_INNER] = xs_ref[b * seq_len:(b + 1) * seq_len, :]
        ext_ref[base:base + seq_len, D_INNER:CONV_DIM] = bc_ref[b * seq_len:(b + 1) * seq_len, :]
        _conv_silu(ext_ref, base, seq_len, cw_ref, cb_ref, xc_ref, b * seq_len)
        for k in range(CONV_HALO):
            row = base + seq_len - CONV_HALO + k
            nsc_ref[k, b:b + 1, :] = ext_ref[row:row + 1, :]


def _conv_sample(u, sc_t, conv_w, conv_b, row0, batch, seq):
    assert seq == SUBLANES and batch % CONV_S_SEQS == 0
    rows = CONV_S_SEQS * seq
    assert row0 % rows == 0
    blk0 = row0 // rows
    full = lambda shape: pl.BlockSpec(shape, lambda i: (0,) * len(shape))
    return pl.pallas_call(
        functools.partial(_conv_sample_kernel, seq_len=seq),
        grid=(batch // CONV_S_SEQS,),
        in_specs=[pl.BlockSpec((rows, D_INNER), lambda i: (blk0 + i, _COL.XS // D_INNER)),
                  pl.BlockSpec((rows, 2 * GN), lambda i: (blk0 + i, _COL.BC // (2 * GN))),
                  pl.BlockSpec((CONV_HALO, CONV_S_SEQS, CONV_DIM), lambda i: (0, i, 0)),
                  full((CONV_WIDTH, CONV_DIM)), full((1, CONV_DIM))],
        out_specs=[pl.BlockSpec((rows, CONV_DIM), lambda i: (i, 0)),
                   pl.BlockSpec((CONV_HALO, CONV_S_SEQS, CONV_DIM), lambda i: (0, i, 0))],
        out_shape=[jax.ShapeDtypeStruct((batch * seq, CONV_DIM), F32),
                   jax.ShapeDtypeStruct((CONV_HALO, batch, CONV_DIM), F32)],
        scratch_shapes=[pltpu.VMEM((CONV_S_SEQS * (SUBLANES + seq), CONV_DIM), F32)],
        compiler_params=_cparams(1),
        name="conv_sample",
    )(u, u, sc_t, conv_w, conv_b)


def _ssd_core(rows, same_seq, zs_ref, dt_ref, dtb_ref, alog_ref, dskip_ref, nw_ref, xs_ref, bc_ref, xd_ref, y_ref,
              yoff_fn, state_fn, g_ref):
    li = lax.broadcasted_iota(jnp.int32, (rows, rows), 0)
    si = lax.broadcasted_iota(jnp.int32, (rows, rows), 1)
    causal = same_seq & (si <= li)
    lane = lax.broadcasted_iota(jnp.int32, (rows, LANES), 1)
    first_half = lane < SSM_HEAD_DIM

    dt = jax.nn.softplus(dt_ref[...] + dtb_ref[...])
    a_neg = -jnp.exp(alog_ref[...])
    d_a = dt * a_neg
    acs = jnp.dot(causal.astype(F32), d_a, precision=HIGHEST, preferred_element_type=F32)
    a_end = jnp.dot(same_seq.astype(F32), d_a, precision=HIGHEST, preferred_element_type=F32)
    acs_t = acs.T
    dt_t = dt.T
    w_in_state = dt * jnp.exp(a_end - acs)
    e_acs = jnp.exp(acs)
    cd_rows = jnp.exp(a_end)

    def pair_expand(m, j):
        a = jnp.broadcast_to(m[:, 2 * j:2 * j + 1], (rows, LANES))
        b = jnp.broadcast_to(m[:, 2 * j + 1:2 * j + 2], (rows, LANES))
        return jnp.where(first_half, a, b)

    pairs_per_group = SSM_HEADS // SSM_GROUPS // 2
    for g in range(SSM_GROUPS):
        b_g = bc_ref[:, g * SSM_STATE:(g + 1) * SSM_STATE].astype(BF16)
        c_g = bc_ref[:, GN + g * SSM_STATE:GN + (g + 1) * SSM_STATE].astype(BF16)
        cb = _dot_nt(c_g, b_g)
        yoff = yoff_fn(g, c_g)
        for jj in range(pairs_per_group):
            j = g * pairs_per_group + jj
            ps = slice(j * LANES, (j + 1) * LANES)
            xs_pair = xs_ref[:, ps]
            xs_bf16 = xs_pair.astype(BF16)
            parts = []
            for hh in range(2):
                h = 2 * j + hh
                seg = acs[:, h:h + 1] - acs_t[h:h + 1, :]
                decay = jnp.exp(jnp.where(causal, seg, -jnp.inf))
                m = (cb * decay * dt_t[h:h + 1, :]).astype(BF16)
                parts.append(jnp.dot(m, xs_bf16, preferred_element_type=F32))
            yd = jnp.where(first_half, parts[0], parts[1])
            xd_ref[:, ps] = (xs_pair * pair_expand(w_in_state, j)).astype(xd_ref.dtype)
            y_ref[:, ps] = (yd + yoff[:, jj * LANES:(jj + 1) * LANES] * pair_expand(e_acs, j)
                            + dskip_ref[:, ps] * xs_pair)
        state_fn(g, cd_rows)

    for g in range(SSM_GROUPS):
        gs = slice(g * GROUP_W, (g + 1) * GROUP_W)
        gg = y_ref[:, gs] * zs_ref[:, gs]
        ms = jnp.mean(gg * gg, axis=-1, keepdims=True)
        g_ref[:, gs] = (gg * lax.rsqrt(ms + RMS_EPS) * nw_ref[:, gs]).astype(BF16)


def _ssd_prompt_kernel(zs_ref, xs_raw_ref, bc_raw_ref, dt_ref, cw_ref, cb_ref, dtb_ref, alog_ref, dskip_ref, nw_ref,
                       g_ref, st_ref, ext_ref, xc_ref, xd_ref, y_ref, stt_ref):
    c = pl.program_id(1)
    rows = SSD_CHUNK

    @pl.when(c == 0)
    def _():
        ext_ref[0:SUBLANES, :] = jnp.zeros((SUBLANES, CONV_DIM), F32)
        stt_ref[...] = jnp.zeros(stt_ref.shape, F32)

    @pl.when(c != 0)
    def _():
        ext_ref[0:SUBLANES, :] = ext_ref[rows:rows + SUBLANES, :]

    ext_ref[SUBLANES:SUBLANES + rows, 0:D_INNER] = xs_raw_ref[...]
    ext_ref[SUBLANES:SUBLANES + rows, D_INNER:CONV_DIM] = bc_raw_ref[...]
    _conv_silu(ext_ref, SUBLANES, rows, cw_ref, cb_ref, xc_ref, 0)
    xs_ref = xc_ref.at[:, 0:D_INNER]
    bc_ref = xc_ref.at[:, D_INNER:CONV_DIM]

    def yoff_fn(g, c_g):
        return jnp.dot(c_g, stt_ref[:, g * GROUP_W:(g + 1) * GROUP_W].astype(BF16), preferred_element_type=F32)

    def state_fn(g, cd_rows):
        b_g = bc_ref[:, g * SSM_STATE:(g + 1) * SSM_STATE].astype(BF16)
        xd_g = xd_ref[:, g * GROUP_W:(g + 1) * GROUP_W]
        new = lax.dot_general(b_g, xd_g, (((0,), (0,)), ((), ())), preferred_element_type=F32)
        lane = lax.broadcasted_iota(jnp.int32, (SSM_STATE, LANES), 1)
        for jj in range(GROUP_W // LANES):
            j = g * (GROUP_W // LANES) + jj
            ps = slice(j * LANES, (j + 1) * LANES)
            cd = jnp.where(lane < SSM_HEAD_DIM,
                           jnp.broadcast_to(cd_rows[0:1, 2 * j:2 * j + 1], (SSM_STATE, LANES)),
                           jnp.broadcast_to(cd_rows[0:1, 2 * j + 1:2 * j + 2], (SSM_STATE, LANES)))
            stt_ref[:, ps] = stt_ref[:, ps] * cd + new[:, jj * LANES:(jj + 1) * LANES]

    same_seq = jnp.full((rows, rows), True)
    _ssd_core(rows, same_seq, zs_ref, dt_ref, dtb_ref, alog_ref, dskip_ref, nw_ref, xs_ref, bc_ref, xd_ref, y_ref,
              yoff_fn, state_fn, g_ref)

    @pl.when(c == pl.num_programs(1) - 1)
    def _():
        for j in range(D_INNER // LANES):
            st_ref[0, j * LANES:(j + 1) * LANES, :] = stt_ref[:, j * LANES:(j + 1) * LANES].T


def _ssd_prompt(u, dt_raw, conv_w, conv_b, dt_bias, a_log, dskip_cols, ssm_norm_w, batch, seq):
    nc = seq // SSD_CHUNK
    rows = SSD_CHUNK
    row_blk = lambda b, c: b * nc + c
    full = lambda shape: pl.BlockSpec(shape, lambda b, c: (0,) * len(shape))
    return pl.pallas_call(
        _ssd_prompt_kernel,
        grid=(batch, nc),
        in_specs=[pl.BlockSpec((rows, D_INNER), lambda b, c: (row_blk(b, c), _COL.Z // D_INNER)),
                  pl.BlockSpec((rows, D_INNER), lambda b, c: (row_blk(b, c), _COL.XS // D_INNER)),
                  pl.BlockSpec((rows, 2 * GN), lambda b, c: (row_blk(b, c), _COL.BC // (2 * GN))),
                  pl.BlockSpec((rows, DT_PAD), lambda b, c: (row_blk(b, c), 0)),
                  full((CONV_WIDTH, CONV_DIM)), full((1, CONV_DIM)), full((1, DT_PAD)), full((1, DT_PAD)),
                  full((1, D_INNER)), full((1, D_INNER))],
        out_specs=[pl.BlockSpec((rows, D_INNER), lambda b, c: (row_blk(b, c), 0)),
                   pl.BlockSpec((1, D_INNER, SSM_STATE), lambda b, c: (b, 0, 0))],
        out_shape=[jax.ShapeDtypeStruct((batch * seq, D_INNER), BF16),
                   jax.ShapeDtypeStruct((batch, D_INNER, SSM_STATE), F32)],
        scratch_shapes=[pltpu.VMEM((rows + SUBLANES, CONV_DIM), F32),
                        pltpu.VMEM((rows, CONV_DIM), F32),
                        pltpu.VMEM((rows, D_INNER), BF16),
                        pltpu.VMEM((rows, D_INNER), F32),
                        pltpu.VMEM((SSM_STATE, D_INNER), F32)],
        compiler_params=_cparams(2),
        name="ssd_prompt",
    )(u, u, u, dt_raw, conv_w, conv_b, dt_bias, a_log, dskip_cols, ssm_norm_w)


SAMPLE_BT = 4


def _ssd_sample_kernel(zs_ref, xs_ref, bc_ref, dt_ref, st_ref, dtb_ref, alog_ref, dskip_ref, nw_ref,
                       g_ref, nst_ref, xd_ref, y_ref, yoff_ref, seq_len):
    bt = SAMPLE_BT
    rows = bt * seq_len
    heads_per_group = SSM_HEADS // SSM_GROUPS

    for b in range(bt):
        rs = slice(b * seq_len, (b + 1) * seq_len)
        for g in range(SSM_GROUPS):
            c_bg = bc_ref[rs, GN + g * SSM_STATE:GN + (g + 1) * SSM_STATE].astype(BF16)
            s_bg = st_ref[b, g * GROUP_W:(g + 1) * GROUP_W, :].astype(BF16)
            yoff_ref[rs, g * GROUP_W:(g + 1) * GROUP_W] = _dot_nt(c_bg, s_bg)

    def yoff_fn(g, c_g):
        return yoff_ref[:, g * GROUP_W:(g + 1) * GROUP_W]

    def state_fn(g, cd_rows):
        for b in range(bt):
            rs = slice(b * seq_len, (b + 1) * seq_len)
            b_bg = bc_ref[rs, g * SSM_STATE:(g + 1) * SSM_STATE].astype(BF16)
            xd_bg = xd_ref[rs, g * GROUP_W:(g + 1) * GROUP_W].astype(BF16)
            new = lax.dot_general(xd_bg, b_bg, (((0,), (0,)), ((), ())), preferred_element_type=F32)
            for hh in range(heads_per_group):
                h = g * heads_per_group + hh
                hs = slice(h * SSM_HEAD_DIM, (h + 1) * SSM_HEAD_DIM)
                cd = jnp.broadcast_to(cd_rows[b * seq_len:b * seq_len + 1, h:h + 1], (SSM_HEAD_DIM, SSM_STATE))
                nst_ref[b, hs, :] = st_ref[b, hs, :] * cd + new[hh * SSM_HEAD_DIM:(hh + 1) * SSM_HEAD_DIM, :]

    li = lax.broadcasted_iota(jnp.int32, (rows, rows), 0)
    si = lax.broadcasted_iota(jnp.int32, (rows, rows), 1)
    same_seq = (li // seq_len) == (si // seq_len)
    _ssd_core(rows, same_seq, zs_ref, dt_ref, dtb_ref, alog_ref, dskip_ref, nw_ref, xs_ref, bc_ref, xd_ref, y_ref,
              yoff_fn, state_fn, g_ref)


def _ssd_sample(u, dt_raw, xc, state, dt_bias, a_log, dskip_cols, ssm_norm_w, row0, batch, seq):
    bt = SAMPLE_BT
    rows = bt * seq
    assert row0 % rows == 0
    blk0 = row0 // rows
    full = lambda shape: pl.BlockSpec(shape, lambda i: (0,) * len(shape))
    return pl.pallas_call(
        functools.partial(_ssd_sample_kernel, seq_len=seq),
        grid=(batch // bt,),
        in_specs=[pl.BlockSpec((rows, D_INNER), lambda i: (blk0 + i, _COL.Z // D_INNER)),
                  pl.BlockSpec((rows, D_INNER), lambda i: (i, 0)),
                  pl.BlockSpec((rows, 2 * GN), lambda i: (i, D_INNER // (2 * GN))),
                  pl.BlockSpec((rows, DT_PAD), lambda i: (blk0 + i, 0)),
                  pl.BlockSpec((bt, D_INNER, SSM_STATE), lambda i: (i, 0, 0)),
                  full((1, DT_PAD)), full((1, DT_PAD)), full((1, D_INNER)), full((1, D_INNER))],
        out_specs=[pl.BlockSpec((rows, D_INNER), lambda i: (i, 0)),
                   pl.BlockSpec((bt, D_INNER, SSM_STATE), lambda i: (i, 0, 0))],
        out_shape=[jax.ShapeDtypeStruct((batch * seq, D_INNER), BF16),
                   jax.ShapeDtypeStruct((batch, D_INNER, SSM_STATE), F32)],
        scratch_shapes=[pltpu.VMEM((rows, D_INNER), F32),
                        pltpu.VMEM((rows, D_INNER), F32),
                        pltpu.VMEM((rows, D_INNER), F32)],
        compiler_params=_cparams(1),
        name="ssd_sample",
    )(u, xc, xc, dt_raw, state, dt_bias, a_log, dskip_cols, ssm_norm_w)


ATTN_SCALE = HEAD_DIM ** -0.5


def _attn_prompt_kernel(sink_ref, q_ref, kp_ref, ko_ref, vp_ref, vo_ref, ga_ref, bias_ref, o_ref):
    n = pl.program_id(1)
    no_prev = jnp.where(n == 0, jnp.inf, 0.0)

    def scores(g):
        ks = slice(g * HEAD_DIM, (g + 1) * HEAD_DIM)
        kk = jnp.concatenate([kp_ref[:, ks], ko_ref[:, ks]], axis=0).astype(BF16)
        vv_t = jnp.concatenate([vp_ref[:, ks], vo_ref[:, ks]], axis=0).T.astype(BF16)
        s_list = []
        for r in range(REP):
            h = g * REP + r
            q = (q_ref[:, h * HEAD_DIM:(h + 1) * HEAD_DIM] * ATTN_SCALE).astype(BF16)
            s_list.append(_dot_nt(kk, q))
        return vv_t, s_list

    def finish(g, vv_t, s_list):
        probs = []
        for r in range(REP):
            h = g * REP + r
            s_prev = s_list[r][0:WINDOW] + bias_ref[h, 0:WINDOW, :]
            s_own = s_list[r][WINDOW:2 * WINDOW] + bias_ref[h, WINDOW:2 * WINDOW, :]
            sink = sink_ref[h]
            m = jnp.maximum(jnp.maximum(jnp.max(s_prev, axis=0, keepdims=True) - no_prev,
                                        jnp.max(s_own, axis=0, keepdims=True)), sink)
            p = jnp.concatenate([jnp.exp(s_prev - (m + no_prev)), jnp.exp(s_own - m)], axis=0)
            denom = jnp.sum(p, axis=0, keepdims=True) + jnp.exp(sink - m)
            probs.append((p.astype(BF16), denom))
        o_t = [jnp.dot(vv_t, p, preferred_element_type=F32) / denom for p, denom in probs]
        for rp in range(REP // 2):
            o_pair = jnp.concatenate(o_t[2 * rp:2 * rp + 2], axis=0).T
            ps = slice((g * REP + 2 * rp) * HEAD_DIM, (g * REP + 2 * rp + 2) * HEAD_DIM)
            o_ref[:, ps] = (o_pair * ga_ref[:, ps]).astype(BF16)

    pending = scores(0)
    for g in range(KV_HEADS):
        cur = pending
        if g + 1 < KV_HEADS:
            pending = scores(g + 1)
        finish(g, *cur)


def _attn_prompt(u, sinks, bias, batch, seq):
    nb = seq // WINDOW
    row = lambda b, n: b * nb + n
    prev = lambda b, n: jnp.maximum(b * nb + n - 1, 0)
    return pl.pallas_call(
        _attn_prompt_kernel,
        grid=(batch, nb),
        in_specs=[pl.BlockSpec(memory_space=pltpu.SMEM),
                  pl.BlockSpec((WINDOW, ATTN_WIDTH), lambda b, n: (row(b, n), _COL.Q // ATTN_WIDTH)),
                  pl.BlockSpec((WINDOW, KV_WIDTH), lambda b, n: (prev(b, n), _COL.K // KV_WIDTH)),
                  pl.BlockSpec((WINDOW, KV_WIDTH), lambda b, n: (row(b, n), _COL.K // KV_WIDTH)),
                  pl.BlockSpec((WINDOW, KV_WIDTH), lambda b, n: (prev(b, n), _COL.V // KV_WIDTH)),
                  pl.BlockSpec((WINDOW, KV_WIDTH), lambda b, n: (row(b, n), _COL.V // KV_WIDTH)),
                  pl.BlockSpec((WINDOW, ATTN_WIDTH), lambda b, n: (row(b, n), _COL.GA // ATTN_WIDTH)),
                  pl.BlockSpec((Q_HEADS, 2 * WINDOW, WINDOW), lambda b, n: (0, 0, 0))],
        out_specs=pl.BlockSpec((WINDOW, ATTN_WIDTH), lambda b, n: (row(b, n), 0)),
        out_shape=jax.ShapeDtypeStruct((batch * seq, ATTN_WIDTH), BF16),
        compiler_params=_cparams(2),
        name="attn_prompt",
    )(sinks, u, u, u, u, u, u, bias)


def _attn_sample_kernel(q_ref, ga_ref, kvt_ref, ck_ref, cv_ref, bias_ref, o_ref, nk_ref, nv_ref,
                        s_ref, p_ref, seq_len):
    bt = SAMPLE_BT
    w_buf = ck_ref.shape[-1]
    keep = w_buf - seq_len
    unit_rows = REP * seq_len
    seq_rows = KV_HEADS * unit_rows
    lane0 = (pl.program_id(0) * bt * seq_len) % kvt_ref.shape[1]
    units = [(b, g) for b in range(bt) for g in range(KV_HEADS)]

    lane = lax.broadcasted_iota(jnp.int32, (KV_WIDTH, w_buf), 1)
    for c_ref, n_ref, row0 in ((ck_ref, nk_ref, 0), (cv_ref, nv_ref, KV_WIDTH)):
        fresh = kvt_ref[row0:row0 + KV_WIDTH, :]
        for b in range(bt):
            new_shift = (keep + w_buf - (lane0 + b * seq_len)) % w_buf
            old = c_ref[b].reshape(KV_WIDTH, w_buf)
            new = jnp.where(lane < keep, pltpu.roll(old, keep, axis=1), pltpu.roll(fresh, new_shift, axis=1))
            n_ref[b] = new.reshape(KV_HEADS, HEAD_DIM, w_buf)

    def both(c_ref, n_ref, b, g):
        return jnp.concatenate([c_ref[b, g], n_ref[b, g]], axis=1).astype(BF16)

    for u, (b, g) in enumerate(units):
        rs = slice(b * seq_len, (b + 1) * seq_len)
        qs = jnp.concatenate([q_ref[rs, (g * REP + r) * HEAD_DIM:(g * REP + r + 1) * HEAD_DIM] for r in range(REP)],
                             axis=0)
        s_ref[u * unit_rows:(u + 1) * unit_rows, :] = jnp.dot(
            (qs * ATTN_SCALE).astype(BF16), both(ck_ref, nk_ref, b, g), preferred_element_type=F32)

    col = lax.broadcasted_iota(jnp.int32, (seq_rows, 2 * w_buf), 1)
    not_sink = (col != seq_len).astype(F32)
    for b in range(bt):
        rows_b = slice(b * seq_rows, (b + 1) * seq_rows)
        s = s_ref[rows_b, :] * not_sink + bias_ref[...]
        m = jnp.max(s, axis=-1, keepdims=True)
        p = jnp.exp(s - m)
        inv = 1.0 / jnp.sum(p, axis=-1, keepdims=True)
        p_ref[rows_b, :] = (p * inv * not_sink).astype(BF16)

    for u, (b, g) in enumerate(units):
        o = _dot_nt(p_ref[u * unit_rows:(u + 1) * unit_rows, :], both(cv_ref, nv_ref, b, g))
        rs = slice(b * seq_len, (b + 1) * seq_len)
        for r in range(REP):
            hs = slice((g * REP + r) * HEAD_DIM, (g * REP + r + 1) * HEAD_DIM)
            o_ref[rs, hs] = (o[r * seq_len:(r + 1) * seq_len, :] * ga_ref[rs, hs]).astype(BF16)


def _attn_sample(u, kv_t, cache_k, cache_v, bias, row0, batch, seq):
    bt = SAMPLE_BT
    rows = bt * seq
    w_buf = cache_k.shape[-1]
    assert w_buf == LANES and row0 % rows == 0 and LANES % rows == 0
    blk0 = row0 // rows
    score_rows = bt * Q_HEADS * seq
    cache_spec = pl.BlockSpec((bt, KV_HEADS, HEAD_DIM, w_buf), lambda i: (i, 0, 0, 0))
    return pl.pallas_call(
        functools.partial(_attn_sample_kernel, seq_len=seq),
        grid=(batch // bt,),
        in_specs=[pl.BlockSpec((rows, ATTN_WIDTH), lambda i: (blk0 + i, _COL.Q // ATTN_WIDTH)),
                  pl.BlockSpec((rows, ATTN_WIDTH), lambda i: (blk0 + i, _COL.GA // ATTN_WIDTH)),
                  pl.BlockSpec((2 * KV_WIDTH, LANES), lambda i: (0, i * rows // LANES)),
                  cache_spec, cache_spec,
                  pl.BlockSpec(bias.shape, lambda i: (0, 0))],
        out_specs=[pl.BlockSpec((rows, ATTN_WIDTH), lambda i: (i, 0)), cache_spec, cache_spec],
        out_shape=[jax.ShapeDtypeStruct((batch * seq, ATTN_WIDTH), BF16),
                   jax.ShapeDtypeStruct(cache_k.shape, F32),
                   jax.ShapeDtypeStruct(cache_v.shape, F32)],
        scratch_shapes=[pltpu.VMEM((score_rows, 2 * w_buf), F32),
                        pltpu.VMEM((score_rows, 2 * w_buf), BF16)],
        compiler_params=_cparams(1),
        name="attn_sample",
    )(u, u, kv_t, cache_k, cache_v, bias)


MERGE_TN = 256


def _merge_kernel(g_ref, og_ref, ms_ref, ma_ref, ws_ref, wa_ref, o_ref):
    p_ssm = jnp.dot(g_ref[...], ws_ref[...], preferred_element_type=F32)
    p_attn = jnp.dot(og_ref[...], wa_ref[...], preferred_element_type=F32)
    o_ref[...] = (_sigmoid(ms_ref[...]) * p_ssm + _sigmoid(ma_ref[...]) * p_attn).astype(BF16)


def _merge(g, og, u, w_ssm, w_attn, row0, tm):
    t = g.shape[0]
    tn = MERGE_TN
    assert row0 % tm == 0
    blk0 = row0 // tm
    return pl.pallas_call(
        _merge_kernel,
        grid=(t // tm, D_MODEL // tn),
        in_specs=[pl.BlockSpec((tm, D_INNER), lambda i, j: (i, 0)),
                  pl.BlockSpec((tm, ATTN_WIDTH), lambda i, j: (i, 0)),
                  pl.BlockSpec((tm, tn), lambda i, j: (blk0 + i, _COL.MS // tn + j)),
                  pl.BlockSpec((tm, tn), lambda i, j: (blk0 + i, _COL.MA // tn + j)),
                  pl.BlockSpec((D_INNER, tn), lambda i, j: (0, j)),
                  pl.BlockSpec((ATTN_WIDTH, tn), lambda i, j: (0, j))],
        out_specs=pl.BlockSpec((tm, tn), lambda i, j: (i, j)),
        out_shape=jax.ShapeDtypeStruct((t, D_MODEL), BF16),
        compiler_params=_cparams(2),
        name="merge",
    )(g, og, u, u, w_ssm, w_attn)


def _out_proj_kernel(m_ref, x_ref, wo_ref, fw_ref, o_ref):
    xn = x_ref[...] + jnp.dot(m_ref[...], wo_ref[...], preferred_element_type=F32)
    ms = jnp.mean(xn * xn, axis=-1, keepdims=True)
    o_ref[...] = xn * lax.rsqrt(ms + RMS_EPS) * fw_ref[...]


def _out_proj(m, x2d, w_out, final_w, tm):
    t = m.shape[0]
    return pl.pallas_call(
        _out_proj_kernel,
        grid=(t // tm,),
        in_specs=[pl.BlockSpec((tm, D_MODEL), lambda i: (i, 0)),
                  pl.BlockSpec((tm, D_MODEL), lambda i: (i, 0)),
                  pl.BlockSpec((D_MODEL, D_MODEL), lambda i: (0, 0)),
                  pl.BlockSpec((1, D_MODEL), lambda i: (0, 0))],
        out_specs=pl.BlockSpec((tm, D_MODEL), lambda i: (i, 0)),
        out_shape=jax.ShapeDtypeStruct((t, D_MODEL), F32),
        compiler_params=_cparams(1),
        name="out_proj",
    )(m, x2d, w_out, final_w)


def _pad_lanes(v):
    return jnp.pad(v.reshape(1, -1), ((0, 0), (0, DT_PAD - v.shape[-1])))


def kernel(x_prompt, x_sample, cache_k, cache_v, state_conv, state_ssm, norm_w, w_in, conv_w, conv_b, dt_bias, a_log,
           d_skip, ssm_norm_w, w_ssm_branch, attn_sinks, w_attn_branch, w_out, rel_bias, final_norm_w):
    assert w_in.shape[0] == 1, "single-layer kernel"
    batch, seq, _ = x_prompt.shape
    dec_batch, dec_seq, _ = x_sample.shape
    w_buf = cache_k.shape[2]
    t_p, t_s = batch * seq, dec_batch * dec_seq
    assert seq % SSD_CHUNK == 0 and seq % WINDOW == 0 and dec_seq == SUBLANES and w_buf == WINDOW
    assert dec_batch % SAMPLE_BT == 0 and t_p % IN_TM == 0 and t_s % IN_TM == 0

    w_t = jnp.transpose(w_in[0])
    sc_t = jnp.transpose(state_conv[0], (1, 0, 2))
    ck_t = jnp.transpose(cache_k[0], (0, 2, 3, 1))
    cv_t = jnp.transpose(cache_v[0], (0, 2, 3, 1))

    nw = norm_w[0].reshape(1, D_MODEL)
    cw, cb = conv_w[0], conv_b[0].reshape(1, CONV_DIM)
    dtb, alog = _pad_lanes(dt_bias[0]), _pad_lanes(a_log[0])
    dskip_cols = jnp.repeat(d_skip[0], SSM_HEAD_DIM).reshape(1, D_INNER)
    snw = ssm_norm_w[0].reshape(1, D_INNER)
    w_ssm = w_ssm_branch[0].astype(BF16)
    w_attn = w_attn_branch[0].astype(BF16)
    wo = w_out[0].astype(BF16)
    fw = final_norm_w.reshape(1, D_MODEL)
    sinks = attn_sinks[0]

    xp2 = x_prompt.reshape(t_p, D_MODEL)
    xs2 = x_sample.reshape(t_s, D_MODEL)

    band_bias_t = _band_bias_t(rel_bias)
    dec_bias = _decode_bias(rel_bias, sinks, dec_seq, w_buf)

    h, dt_raw = _norm_dt(xp2, xs2, nw, w_t)
    u = _in_proj(h, w_t)
    kv_t = _kv_t(h, w_t, t_p, t_s)

    g_p, st_p = _ssd_prompt(u, dt_raw, cw, cb, dtb, alog, dskip_cols, snw, batch, seq)
    og_p = _attn_prompt(u, sinks, band_bias_t, batch, seq)
    m_p = _merge(g_p, og_p, u, w_ssm, w_attn, 0, tm=1024)
    y_p = _out_proj(m_p, xp2, wo, fw, tm=512)

    xc_s, nsc_t = _conv_sample(u, sc_t, cw, cb, t_p, dec_batch, dec_seq)
    g_s, st_s = _ssd_sample(u, dt_raw, xc_s, state_ssm[0].reshape(dec_batch, D_INNER, SSM_STATE),
                            dtb, alog, dskip_cols, snw, t_p, dec_batch, dec_seq)
    og_s, nk_t, nv_t = _attn_sample(u, kv_t, ck_t, cv_t, dec_bias, t_p, dec_batch, dec_seq)
    m_s = _merge(g_s, og_s, u, w_ssm, w_attn, t_p, tm=1024)
    y_s = _out_proj(m_s, xs2, wo, fw, tm=512)

    def seq_tail(n_rows, col0, width):
        return jnp.stack([u[(b + 1) * seq - n_rows:(b + 1) * seq, col0:col0 + width] for b in range(batch)])

    k_p = seq_tail(WINDOW, _COL.K, KV_WIDTH).reshape(1, batch, WINDOW, KV_HEADS, HEAD_DIM)
    v_p = seq_tail(WINDOW, _COL.V, KV_WIDTH).reshape(1, batch, WINDOW, KV_HEADS, HEAD_DIM)
    conv_p = seq_tail(CONV_HALO, _COL.XS, CONV_DIM)[None]
    conv_s = jnp.transpose(nsc_t, (1, 0, 2))[None]
    ssm_p = st_p.reshape(1, batch, SSM_HEADS, SSM_HEAD_DIM, SSM_STATE)
    ssm_s = st_s.reshape(1, dec_batch, SSM_HEADS, SSM_HEAD_DIM, SSM_STATE)
    k_s = jnp.transpose(nk_t, (0, 3, 1, 2))[None]
    v_s = jnp.transpose(nv_t, (0, 3, 1, 2))[None]
    return (y_p.reshape(batch, seq, D_MODEL), y_s.reshape(dec_batch, dec_seq, D_MODEL),
            k_p, v_p, conv_p, ssm_p, k_s, v_s, conv_s, ssm_s)
```

```python
import functools
import math

import numpy as np
import jax
import jax.numpy as jnp
from jax import lax
from jax.experimental import pallas as pl
from jax.experimental.pallas import tpu as pltpu

F32 = jnp.float32
BF16 = jnp.bfloat16
HIGHEST = lax.Precision.HIGHEST

D_MODEL = 2048
D_INNER = 4096
SSM_HEAD_DIM = 64
SSM_HEADS = 64
SSM_GROUPS = 8
SSM_STATE = 128
CONV_WIDTH = 4
CONV_HALO = CONV_WIDTH - 1
GN = SSM_GROUPS * SSM_STATE
CONV_DIM = D_INNER + 2 * GN
SSD_CHUNK = 128
GROUP_W = D_INNER // SSM_GROUPS
HEAD_DIM = 64
Q_HEADS = 32
KV_HEADS = 8
REP = Q_HEADS // KV_HEADS
ATTN_WIDTH = Q_HEADS * HEAD_DIM
KV_WIDTH = KV_HEADS * HEAD_DIM
WINDOW = 128
N_BUCKETS = 32
MAX_EXACT = N_BUCKETS // 2
RMS_EPS = 1e-6

_OFF_Z = 0
_OFF_XBC = _OFF_Z + D_INNER
_OFF_DT = _OFF_XBC + CONV_DIM
_OFF_Q = _OFF_DT + SSM_HEADS
_OFF_K = _OFF_Q + ATTN_WIDTH
_OFF_V = _OFF_K + KV_WIDTH
_OFF_GA = _OFF_V + KV_WIDTH
_OFF_MS = _OFF_GA + ATTN_WIDTH
_OFF_MA = _OFF_MS + D_MODEL
_IN_COLS = _OFF_MA + D_MODEL

LANES = 128
SUBLANES = 8
DT_PAD = LANES


class _COL:
    Z = 0
    XS = Z + D_INNER
    BC = XS + D_INNER
    Q = BC + 2 * GN
    GA = Q + ATTN_WIDTH
    MS = GA + ATTN_WIDTH
    MA = MS + D_MODEL
    K = MA + D_MODEL
    V = K + KV_WIDTH
    TOTAL = V + KV_WIDTH


VMEM_LIMIT = 52 * 1024 * 1024
BIG_VMEM_LIMIT = 57 * 1024 * 1024


def _cparams(n_grid, limit=VMEM_LIMIT):
    return pltpu.CompilerParams(dimension_semantics=("arbitrary",) * n_grid, vmem_limit_bytes=limit)


def _silu(v):
    return v * (1.0 / (1.0 + jnp.exp(-v)))


def _sigmoid(v):
    return 1.0 / (1.0 + jnp.exp(-v))


def _dot_nt(a, b):
    return lax.dot_general(a, b, (((1,), (1,)), ((), ())), preferred_element_type=F32)


NORM_TM = 512


def _norm_dt_kernel(xp_ref, xs_ref, nw_ref, wdt_ref, h_ref, dt_ref, n_prompt_tiles):
    i = pl.program_id(0)

    def emit(x_ref):
        xf = x_ref[...]
        ms = jnp.mean(xf * xf, axis=-1, keepdims=True)
        h = (xf * lax.rsqrt(ms + RMS_EPS) * nw_ref[...]).astype(BF16)
        h_ref[...] = h
        dt_ref[...] = _dot_nt(h, wdt_ref[...].astype(BF16))

    @pl.when(i < n_prompt_tiles)
    def _():
        emit(xp_ref)

    @pl.when(i >= n_prompt_tiles)
    def _():
        emit(xs_ref)


def _norm_dt(xp2, xs2, norm_w, w_t):
    tp, ts = xp2.shape[0], xs2.shape[0]
    n_p, n_s = tp // NORM_TM, ts // NORM_TM
    return pl.pallas_call(
        functools.partial(_norm_dt_kernel, n_prompt_tiles=n_p),
        grid=(n_p + n_s,),
        in_specs=[pl.BlockSpec((NORM_TM, D_MODEL), lambda i: (jnp.minimum(i, n_p - 1), 0)),
                  pl.BlockSpec((NORM_TM, D_MODEL), lambda i: (jnp.maximum(i - n_p, 0), 0)),
                  pl.BlockSpec((1, D_MODEL), lambda i: (0, 0)),
                  pl.BlockSpec((DT_PAD, D_MODEL), lambda i: (_OFF_DT // DT_PAD, 0))],
        out_specs=[pl.BlockSpec((NORM_TM, D_MODEL), lambda i: (i, 0)),
                   pl.BlockSpec((NORM_TM, DT_PAD), lambda i: (i, 0))],
        out_shape=[jax.ShapeDtypeStruct((tp + ts, D_MODEL), BF16),
                   jax.ShapeDtypeStruct((tp + ts, DT_PAD), F32)],
        compiler_params=_cparams(1),
        name="norm_dt",
    )(xp2, xs2, norm_w, w_t)


IN_TM = 1536
IN_TN = 1024
IN_CHUNK = 512
assert _COL.TOTAL % IN_TN == 0 and IN_TN == 2 * IN_CHUNK and _OFF_DT % IN_CHUNK == 0


def _src_row(c):
    straight = _OFF_DT // IN_CHUNK
    jj = c - straight
    q_t, kv_t = ATTN_WIDTH // IN_CHUNK, 2 * KV_WIDTH // IN_CHUNK
    gate_t = (ATTN_WIDTH + 2 * D_MODEL) // IN_CHUNK
    shifted = jnp.where(jj < q_t, jj, jnp.where(jj < q_t + gate_t, jj + kv_t, jj - gate_t))
    return pl.multiple_of(jnp.where(c < straight, c * IN_CHUNK, _OFF_Q + shifted * IN_CHUNK), SSM_HEADS)


IN_NSPLIT = 4
IN_MW = IN_TM // IN_NSPLIT
J_Z1 = _COL.XS // IN_TN
J_GA0, J_GA1 = _COL.GA // IN_TN, _COL.MS // IN_TN
assert _COL.XS % IN_TN == 0 and _COL.GA % IN_TN == 0 and _COL.MS % IN_TN == 0


def _in_proj_kernel(h_ref, w0_ref, w1_ref, o_ref, wbf_ref, *stage_refs):
    j = pl.program_id(0)

    @pl.when(pl.program_id(1) == 0)
    def _():
        wbf_ref[0:IN_CHUNK, :] = w0_ref[...].astype(BF16)
        wbf_ref[IN_CHUNK:IN_TN, :] = w1_ref[...].astype(BF16)

    is_silu_tile = (j < J_Z1) | ((j >= J_GA0) & (j < J_GA1))

    @pl.when(is_silu_tile)
    def _():
        def silu_chunk(n):
            o_ref[n * IN_MW:(n + 1) * IN_MW, :] = _silu(stage_refs[n][...])

        for n in range(IN_NSPLIT):
            stage_refs[n][...] = _dot_nt(h_ref[n * IN_MW:(n + 1) * IN_MW, :], wbf_ref[...])
            if n > 0:
                silu_chunk(n - 1)
        silu_chunk(IN_NSPLIT - 1)

    @pl.when(jnp.logical_not(is_silu_tile))
    def _():
        o_ref[...] = _dot_nt(h_ref[...], wbf_ref[...])


def _in_proj(h, w_t):
    t = h.shape[0]
    return pl.pallas_call(
        _in_proj_kernel,
        grid=(_COL.TOTAL // IN_TN, t // IN_TM),
        in_specs=[pl.BlockSpec((IN_TM, D_MODEL), lambda j, i: (i, 0)),
                  pl.BlockSpec((pl.Element(IN_CHUNK), pl.Element(D_MODEL)), lambda j, i: (_src_row(2 * j), 0)),
                  pl.BlockSpec((pl.Element(IN_CHUNK), pl.Element(D_MODEL)), lambda j, i: (_src_row(2 * j + 1), 0))],
        out_specs=pl.BlockSpec((IN_TM, IN_TN), lambda j, i: (i, j)),
        out_shape=jax.ShapeDtypeStruct((t, _COL.TOTAL), F32),
        scratch_shapes=[pltpu.VMEM((IN_TN, D_MODEL), BF16)]
        + [pltpu.VMEM((IN_MW, IN_TN), F32) for _ in range(IN_NSPLIT)],
        compiler_params=_cparams(2, BIG_VMEM_LIMIT),
        name="in_proj",
    )(h, w_t, w_t)


def _kv_t_kernel(w_ref, h_ref, o_ref):
    o_ref[...] = _dot_nt(w_ref[...].astype(BF16), h_ref[...])


def _kv_t(h, w_t, row0, rows):
    assert row0 % rows == 0
    return pl.pallas_call(
        _kv_t_kernel,
        grid=(1,),
        in_specs=[pl.BlockSpec((pl.Element(2 * KV_WIDTH), pl.Element(D_MODEL)), lambda i: (_OFF_K, 0)),
                  pl.BlockSpec((rows, D_MODEL), lambda i: (row0 // rows, 0))],
        out_specs=pl.BlockSpec((2 * KV_WIDTH, rows), lambda i: (0, 0)),
        out_shape=jax.ShapeDtypeStruct((2 * KV_WIDTH, rows), F32),
        compiler_params=_cparams(1),
        name="kv_t",
    )(w_t, h)


def _bucket_of_dist(dist):
    n = np.maximum(dist, 0)
    nf = np.maximum(n, 1).astype(np.float32)
    large = MAX_EXACT + (np.log(nf / MAX_EXACT) / math.log(WINDOW / MAX_EXACT)
                         * (N_BUCKETS - MAX_EXACT)).astype(np.int32)
    large = np.minimum(large, N_BUCKETS - 1)
    bucket = np.where(n < MAX_EXACT, n, large)
    return np.where((dist >= 0) & (dist <= WINDOW), bucket, -1).astype(np.int32)


def _bias_of_bucket(bk, table_ref, h):
    acc = jnp.full(bk.shape, -jnp.inf, F32)
    for b in range(N_BUCKETS):
        acc = jnp.where(bk == b, table_ref[b, h], acc)
    return acc


BAND_BIAS_HEADS = 8


def _band_bias_kernel(table_ref, bucket_ref, o_ref):
    for hh in range(BAND_BIAS_HEADS):
        o_ref[hh] = _bias_of_bucket(bucket_ref[...], table_ref, pl.program_id(0) * BAND_BIAS_HEADS + hh)


def _band_bias_t(rel_bias):
    kj = np.arange(2 * WINDOW)[:, None]
    qi = np.arange(WINDOW)[None, :]
    bucket = jnp.asarray(_bucket_of_dist(qi + WINDOW - kj))
    return pl.pallas_call(
        _band_bias_kernel,
        grid=(Q_HEADS // BAND_BIAS_HEADS,),
        in_specs=[pl.BlockSpec(memory_space=pltpu.SMEM),
                  pl.BlockSpec((2 * WINDOW, WINDOW), lambda h: (0, 0))],
        out_specs=pl.BlockSpec((BAND_BIAS_HEADS, 2 * WINDOW, WINDOW), lambda h: (h, 0, 0)),
        out_shape=jax.ShapeDtypeStruct((Q_HEADS, 2 * WINDOW, WINDOW), F32),
        compiler_params=_cparams(1),
        name="band_bias",
    )(rel_bias, bucket)


def _decode_bias_kernel(table_ref, sink_ref, bucket_ref, o_ref, seq):
    g = pl.program_id(0)
    sink_lane = lax.broadcasted_iota(jnp.int32, bucket_ref.shape, 1) == seq
    for r in range(REP):
        h = g * REP + r
        bias = _bias_of_bucket(bucket_ref[...], table_ref, h)
        o_ref[r * seq:(r + 1) * seq, :] = jnp.where(sink_lane, sink_ref[h], bias)


def _decode_bias(rel_bias, sinks, seq, w_buf):
    l = np.arange(seq)[:, None]
    j = np.arange(w_buf)[None, :]
    dist_old = np.where((j < seq), l + w_buf - j, -1)
    dist_new = np.where(j < w_buf - seq, l + w_buf - (j + seq), l - (j - (w_buf - seq)))
    bucket = jnp.asarray(np.concatenate([_bucket_of_dist(dist_old), _bucket_of_dist(dist_new)], axis=1))
    return pl.pallas_call(
        functools.partial(_decode_bias_kernel, seq=seq),
        grid=(KV_HEADS,),
        in_specs=[pl.BlockSpec(memory_space=pltpu.SMEM), pl.BlockSpec(memory_space=pltpu.SMEM),
                  pl.BlockSpec(bucket.shape, lambda g: (0, 0))],
        out_specs=pl.BlockSpec((REP * seq, 2 * w_buf), lambda g: (g, 0)),
        out_shape=jax.ShapeDtypeStruct((Q_HEADS * seq, 2 * w_buf), F32),
        compiler_params=_cparams(1),
        name="decode_bias",
    )(rel_bias, sinks, bucket)


CONV_STRIP = 512


def _conv_silu(ext_ref, base, rows, cw_ref, cb_ref, xc_ref, out_row):
    for c0 in range(0, CONV_DIM, CONV_STRIP):
        cs = slice(c0, c0 + CONV_STRIP)
        acc = cb_ref[:, cs] + cw_ref[3:4, cs] * ext_ref[base:base + rows, cs]
        for k in range(1, CONV_WIDTH):
            acc = acc + cw_ref[3 - k:4 - k, cs] * ext_ref[base - k:base - k + rows, cs]
        xc_ref[out_row:out_row + rows, cs] = _silu(acc)


CONV_S_SEQS = 16


def _conv_sample_kernel(xs_ref, bc_ref, sc_ref, cw_ref, cb_ref, xc_ref, nsc_ref, ext_ref, seq_len):
    slab = SUBLANES + seq_len
    for b in range(CONV_S_SEQS):
        base = b * slab + SUBLANES
        for k in range(CONV_HALO):
            ext_ref[base - CONV_HALO + k:base - CONV_HALO + k + 1, :] = sc_ref[k, b:b + 1, :]
        ext_ref[base:base + seq_len, 0:D_INNER] = xs_ref[b * seq_len:(b + 1) * seq_len, :]
        ext_ref[base:base + seq_len, D_INNER:CONV_DIM] = bc_ref[b * seq_len:(b + 1) * seq_len, :]
        _conv_silu(ext_ref, base, seq_len, cw_ref, cb_ref, xc_ref, b * seq_len)
        for k in range(CONV_HALO):
            row = base + seq_len - CONV_HALO + k
            nsc_ref[k, b:b + 1, :] = ext_ref[row:row + 1, :]


def _conv_sample(u, sc_t, conv_w, conv_b, row0, batch, seq):
    assert seq == SUBLANES and batch % CONV_S_SEQS == 0
    rows = CONV_S_SEQS * seq
    assert row0 % rows == 0
    blk0 = row0 // rows
    full = lambda shape: pl.BlockSpec(shape, lambda i: (0,) * len(shape))
    return pl.pallas_call(
        functools.partial(_conv_sample_kernel, seq_len=seq),
        grid=(batch // CONV_S_SEQS,),
        in_specs=[pl.BlockSpec((rows, D_INNER), lambda i: (blk0 + i, _COL.XS // D_INNER)),
                  pl.BlockSpec((rows, 2 * GN), lambda i: (blk0 + i, _COL.BC // (2 * GN))),
                  pl.BlockSpec((CONV_HALO, CONV_S_SEQS, CONV_DIM), lambda i: (0, i, 0)),
                  full((CONV_WIDTH, CONV_DIM)), full((1, CONV_DIM))],
        out_specs=[pl.BlockSpec((rows, CONV_DIM), lambda i: (i, 0)),
                   pl.BlockSpec((CONV_HALO, CONV_S_SEQS, CONV_DIM), lambda i: (0, i, 0))],
        out_shape=[jax.ShapeDtypeStruct((batch * seq, CONV_DIM), F32),
                   jax.ShapeDtypeStruct((CONV_HALO, batch, CONV_DIM), F32)],
        scratch_shapes=[pltpu.VMEM((CONV_S_SEQS * (SUBLANES + seq), CONV_DIM), F32)],
        compiler_params=_cparams(1),
        name="conv_sample",
    )(u, u, sc_t, conv_w, conv_b)


def _ssd_core(rows, same_seq, zs_ref, dt_ref, dtb_ref, alog_ref, dskip_ref, nw_ref, xs_ref, bc_ref, xd_ref, y_ref,
              yoff_fn, state_fn, g_ref):
    li = lax.broadcasted_iota(jnp.int32, (rows, rows), 0)
    si = lax.broadcasted_iota(jnp.int32, (rows, rows), 1)
    causal = same_seq & (si <= li)
    lane = lax.broadcasted_iota(jnp.int32, (rows, LANES), 1)
    first_half = lane < SSM_HEAD_DIM

    dt = jax.nn.softplus(dt_ref[...] + dtb_ref[...])
    a_neg = -jnp.exp(alog_ref[...])
    d_a = dt * a_neg
    acs = jnp.dot(causal.astype(F32), d_a, precision=HIGHEST, preferred_element_type=F32)
    a_end = jnp.dot(same_seq.astype(F32), d_a, precision=HIGHEST, preferred_element_type=F32)
    acs_t = acs.T
    dt_t = dt.T
    w_in_state = dt * jnp.exp(a_end - acs)
    e_acs = jnp.exp(acs)
    cd_rows = jnp.exp(a_end)

    def pair_expand(m, j):
        a = jnp.broadcast_to(m[:, 2 * j:2 * j + 1], (rows, LANES))
        b = jnp.broadcast_to(m[:, 2 * j + 1:2 * j + 2], (rows, LANES))
        return jnp.where(first_half, a, b)

    pairs_per_group = SSM_HEADS // SSM_GROUPS // 2
    for g in range(SSM_GROUPS):
        b_g = bc_ref[:, g * SSM_STATE:(g + 1) * SSM_STATE].astype(BF16)
        c_g = bc_ref[:, GN + g * SSM_STATE:GN + (g + 1) * SSM_STATE].astype(BF16)
        cb = _dot_nt(c_g, b_g)
        yoff = yoff_fn(g, c_g)
        for jj in range(pairs_per_group):
            j = g * pairs_per_group + jj
            ps = slice(j * LANES, (j + 1) * LANES)
            xs_pair = xs_ref[:, ps]
            xs_bf16 = xs_pair.astype(BF16)
            parts = []
            for hh in range(2):
                h = 2 * j + hh
                seg = acs[:, h:h + 1] - acs_t[h:h + 1, :]
                decay = jnp.exp(jnp.where(causal, seg, -jnp.inf))
                m = (cb * decay * dt_t[h:h + 1, :]).astype(BF16)
                parts.append(jnp.dot(m, xs_bf16, preferred_element_type=F32))
            yd = jnp.where(first_half, parts[0], parts[1])
            xd_ref[:, ps] = (xs_pair * pair_expand(w_in_state, j)).astype(xd_ref.dtype)
            y_ref[:, ps] = (yd + yoff[:, jj * LANES:(jj + 1) * LANES] * pair_expand(e_acs, j)
                            + dskip_ref[:, ps] * xs_pair)
        state_fn(g, cd_rows)

    for g in range(SSM_GROUPS):
        gs = slice(g * GROUP_W, (g + 1) * GROUP_W)
        gg = y_ref[:, gs] * zs_ref[:, gs]
        ms = jnp.mean(gg * gg, axis=-1, keepdims=True)
        g_ref[:, gs] = (gg * lax.rsqrt(ms + RMS_EPS) * nw_ref[:, gs]).astype(BF16)


def _ssd_prompt_kernel(zs_ref, xs_raw_ref, bc_raw_ref, dt_ref, cw_ref, cb_ref, dtb_ref, alog_ref, dskip_ref, nw_ref,
                       g_ref, st_ref, ext_ref, xc_ref, xd_ref, y_ref, stt_ref):
    c = pl.program_id(1)
    rows = SSD_CHUNK

    @pl.when(c == 0)
    def _():
        ext_ref[0:SUBLANES, :] = jnp.zeros((SUBLANES, CONV_DIM), F32)
        stt_ref[...] = jnp.zeros(stt_ref.shape, F32)

    @pl.when(c != 0)
    def _():
        ext_ref[0:SUBLANES, :] = ext_ref[rows:rows + SUBLANES, :]

    ext_ref[SUBLANES:SUBLANES + rows, 0:D_INNER] = xs_raw_ref[...]
    ext_ref[SUBLANES:SUBLANES + rows, D_INNER:CONV_DIM] = bc_raw_ref[...]
    _conv_silu(ext_ref, SUBLANES, rows, cw_ref, cb_ref, xc_ref, 0)
    xs_ref = xc_ref.at[:, 0:D_INNER]
    bc_ref = xc_ref.at[:, D_INNER:CONV_DIM]

    def yoff_fn(g, c_g):
        return jnp.dot(c_g, stt_ref[:, g * GROUP_W:(g + 1) * GROUP_W].astype(BF16), preferred_element_type=F32)

    def state_fn(g, cd_rows):
        b_g = bc_ref[:, g * SSM_STATE:(g + 1) * SSM_STATE].astype(BF16)
        xd_g = xd_ref[:, g * GROUP_W:(g + 1) * GROUP_W]
        new = lax.dot_general(b_g, xd_g, (((0,), (0,)), ((), ())), preferred_element_type=F32)
        lane = lax.broadcasted_iota(jnp.int32, (SSM_STATE, LANES), 1)
        for jj in range(GROUP_W // LANES):
            j = g * (GROUP_W // LANES) + jj
            ps = slice(j * LANES, (j + 1) * LANES)
            cd = jnp.where(lane < SSM_HEAD_DIM,
                           jnp.broadcast_to(cd_rows[0:1, 2 * j:2 * j + 1], (SSM_STATE, LANES)),
                           jnp.broadcast_to(cd_rows[0:1, 2 * j + 1:2 * j + 2], (SSM_STATE, LANES)))
            stt_ref[:, ps] = stt_ref[:, ps] * cd + new[:, jj * LANES:(jj + 1) * LANES]

    same_seq = jnp.full((rows, rows), True)
    _ssd_core(rows, same_seq, zs_ref, dt_ref, dtb_ref, alog_ref, dskip_ref, nw_ref, xs_ref, bc_ref, xd_ref, y_ref,
              yoff_fn, state_fn, g_ref)

    @pl.when(c == pl.num_programs(1) - 1)
    def _():
        for j in range(D_INNER // LANES):
            st_ref[0, j * LANES:(j + 1) * LANES, :] = stt_ref[:, j * LANES:(j + 1) * LANES].T


def _ssd_prompt(u, dt_raw, conv_w, conv_b, dt_bias, a_log, dskip_cols, ssm_norm_w, batch, seq):
    nc = seq // SSD_CHUNK
    rows = SSD_CHUNK
    row_blk = lambda b, c: b * nc + c
    full = lambda shape: pl.BlockSpec(shape, lambda b, c: (0,) * len(shape))
    return pl.pallas_call(
        _ssd_prompt_kernel,
        grid=(batch, nc),
        in_specs=[pl.BlockSpec((rows, D_INNER), lambda b, c: (row_blk(b, c), _COL.Z // D_INNER)),
                  pl.BlockSpec((rows, D_INNER), lambda b, c: (row_blk(b, c), _COL.XS // D_INNER)),
                  pl.BlockSpec((rows, 2 * GN), lambda b, c: (row_blk(b, c), _COL.BC // (2 * GN))),
                  pl.BlockSpec((rows, DT_PAD), lambda b, c: (row_blk(b, c), 0)),
                  full((CONV_WIDTH, CONV_DIM)), full((1, CONV_DIM)), full((1, DT_PAD)), full((1, DT_PAD)),
                  full((1, D_INNER)), full((1, D_INNER))],
        out_specs=[pl.BlockSpec((rows, D_INNER), lambda b, c: (row_blk(b, c), 0)),
                   pl.BlockSpec((1, D_INNER, SSM_STATE), lambda b, c: (b, 0, 0))],
        out_shape=[jax.ShapeDtypeStruct((batch * seq, D_INNER), BF16),
                   jax.ShapeDtypeStruct((batch, D_INNER, SSM_STATE), F32)],
        scratch_shapes=[pltpu.VMEM((rows + SUBLANES, CONV_DIM), F32),
                        pltpu.VMEM((rows, CONV_DIM), F32),
                        pltpu.VMEM((rows, D_INNER), BF16),
                        pltpu.VMEM((rows, D_INNER), F32),
                        pltpu.VMEM((SSM_STATE, D_INNER), F32)],
        compiler_params=_cparams(2),
        name="ssd_prompt",
    )(u, u, u, dt_raw, conv_w, conv_b, dt_bias, a_log, dskip_cols, ssm_norm_w)


SAMPLE_BT = 4


def _ssd_sample_kernel(zs_ref, xs_ref, bc_ref, dt_ref, st_ref, dtb_ref, alog_ref, dskip_ref, nw_ref,
                       g_ref, nst_ref, xd_ref, y_ref, yoff_ref, seq_len):
    bt = SAMPLE_BT
    rows = bt * seq_len
    heads_per_group = SSM_HEADS // SSM_GROUPS

    for b in range(bt):
        rs = slice(b * seq_len, (b + 1) * seq_len)
        for g in range(SSM_GROUPS):
            c_bg = bc_ref[rs, GN + g * SSM_STATE:GN + (g + 1) * SSM_STATE].astype(BF16)
            s_bg = st_ref[b, g * GROUP_W:(g + 1) * GROUP_W, :].astype(BF16)
            yoff_ref[rs, g * GROUP_W:(g + 1) * GROUP_W] = _dot_nt(c_bg, s_bg)

    def yoff_fn(g, c_g):
        return yoff_ref[:, g * GROUP_W:(g + 1) * GROUP_W]

    def state_fn(g, cd_rows):
        for b in range(bt):
            rs = slice(b * seq_len, (b + 1) * seq_len)
            b_bg = bc_ref[rs, g * SSM_STATE:(g + 1) * SSM_STATE].astype(BF16)
            xd_bg = xd_ref[rs, g * GROUP_W:(g + 1) * GROUP_W].astype(BF16)
            new = lax.dot_general(xd_bg, b_bg, (((0,), (0,)), ((), ())), preferred_element_type=F32)
            for hh in range(heads_per_group):
                h = g * heads_per_group + hh
                hs = slice(h * SSM_HEAD_DIM, (h + 1) * SSM_HEAD_DIM)
                cd = jnp.broadcast_to(cd_rows[b * seq_len:b * seq_len + 1, h:h + 1], (SSM_HEAD_DIM, SSM_STATE))
                nst_ref[b, hs, :] = st_ref[b, hs, :] * cd + new[hh * SSM_HEAD_DIM:(hh + 1) * SSM_HEAD_DIM, :]

    li = lax.broadcasted_iota(jnp.int32, (rows, rows), 0)
    si = lax.broadcasted_iota(jnp.int32, (rows, rows), 1)
    same_seq = (li // seq_len) == (si // seq_len)
    _ssd_core(rows, same_seq, zs_ref, dt_ref, dtb_ref, alog_ref, dskip_ref, nw_ref, xs_ref, bc_ref, xd_ref, y_ref,
              yoff_fn, state_fn, g_ref)


def _ssd_sample(u, dt_raw, xc, state, dt_bias, a_log, dskip_cols, ssm_norm_w, row0, batch, seq):
    bt = SAMPLE_BT
    rows = bt * seq
    assert row0 % rows == 0
    blk0 = row0 // rows
    full = lambda shape: pl.BlockSpec(shape, lambda i: (0,) * len(shape))
    return pl.pallas_call(
        functools.partial(_ssd_sample_kernel, seq_len=seq),
        grid=(batch // bt,),
        in_specs=[pl.BlockSpec((rows, D_INNER), lambda i: (blk0 + i, _COL.Z // D_INNER)),
                  pl.BlockSpec((rows, D_INNER), lambda i: (i, 0)),
                  pl.BlockSpec((rows, 2 * GN), lambda i: (i, D_INNER // (2 * GN))),
                  pl.BlockSpec((rows, DT_PAD), lambda i: (blk0 + i, 0)),
                  pl.BlockSpec((bt, D_INNER, SSM_STATE), lambda i: (i, 0, 0)),
                  full((1, DT_PAD)), full((1, DT_PAD)), full((1, D_INNER)), full((1, D_INNER))],
        out_specs=[pl.BlockSpec((rows, D_INNER), lambda i: (i, 0)),
                   pl.BlockSpec((bt, D_INNER, SSM_STATE), lambda i: (i, 0, 0))],
        out_shape=[jax.ShapeDtypeStruct((batch * seq, D_INNER), BF16),
                   jax.ShapeDtypeStruct((batch, D_INNER, SSM_STATE), F32)],
        scratch_shapes=[pltpu.VMEM((rows, D_INNER), F32),
                        pltpu.VMEM((rows, D_INNER), F32),
                        pltpu.VMEM((rows, D_INNER), F32)],
        compiler_params=_cparams(1),
        name="ssd_sample",
    )(u, xc, xc, dt_raw, state, dt_bias, a_log, dskip_cols, ssm_norm_w)


ATTN_SCALE = HEAD_DIM ** -0.5


def _attn_prompt_kernel(sink_ref, q_ref, kp_ref, ko_ref, vp_ref, vo_ref, ga_ref, bias_ref, o_ref):
    n = pl.program_id(1)
    no_prev = jnp.where(n == 0, jnp.inf, 0.0)

    def scores(g):
        ks = slice(g * HEAD_DIM, (g + 1) * HEAD_DIM)
        kk = jnp.concatenate([kp_ref[:, ks], ko_ref[:, ks]], axis=0).astype(BF16)
        vv_t = jnp.concatenate([vp_ref[:, ks], vo_ref[:, ks]], axis=0).T.astype(BF16)
        s_list = []
        for r in range(REP):
            h = g * REP + r
            q = (q_ref[:, h * HEAD_DIM:(h + 1) * HEAD_DIM] * ATTN_SCALE).astype(BF16)
            s_list.append(_dot_nt(kk, q))
        return vv_t, s_list

    def finish(g, vv_t, s_list):
        probs = []
        for r in range(REP):
            h = g * REP + r
            s_prev = s_list[r][0:WINDOW] + bias_ref[h, 0:WINDOW, :]
            s_own = s_list[r][WINDOW:2 * WINDOW] + bias_ref[h, WINDOW:2 * WINDOW, :]
            sink = sink_ref[h]
            m = jnp.maximum(jnp.maximum(jnp.max(s_prev, axis=0, keepdims=True) - no_prev,
                                        jnp.max(s_own, axis=0, keepdims=True)), sink)
            p = jnp.concatenate([jnp.exp(s_prev - (m + no_prev)), jnp.exp(s_own - m)], axis=0)
            denom = jnp.sum(p, axis=0, keepdims=True) + jnp.exp(sink - m)
            probs.append((p.astype(BF16), denom))
        o_t = [jnp.dot(vv_t, p, preferred_element_type=F32) / denom for p, denom in probs]
        for rp in range(REP // 2):
            o_pair = jnp.concatenate(o_t[2 * rp:2 * rp + 2], axis=0).T
            ps = slice((g * REP + 2 * rp) * HEAD_DIM, (g * REP + 2 * rp + 2) * HEAD_DIM)
            o_ref[:, ps] = (o_pair * ga_ref[:, ps]).astype(BF16)

    pending = scores(0)
    for g in range(KV_HEADS):
        cur = pending
        if g + 1 < KV_HEADS:
            pending = scores(g + 1)
        finish(g, *cur)


def _attn_prompt(u, sinks, bias, batch, seq):
    nb = seq // WINDOW
    row = lambda b, n: b * nb + n
    prev = lambda b, n: jnp.maximum(b * nb + n - 1, 0)
    return pl.pallas_call(
        _attn_prompt_kernel,
        grid=(batch, nb),
        in_specs=[pl.BlockSpec(memory_space=pltpu.SMEM),
                  pl.BlockSpec((WINDOW, ATTN_WIDTH), lambda b, n: (row(b, n), _COL.Q // ATTN_WIDTH)),
                  pl.BlockSpec((WINDOW, KV_WIDTH), lambda b, n: (prev(b, n), _COL.K // KV_WIDTH)),
                  pl.BlockSpec((WINDOW, KV_WIDTH), lambda b, n: (row(b, n), _COL.K // KV_WIDTH)),
                  pl.BlockSpec((WINDOW, KV_WIDTH), lambda b, n: (prev(b, n), _COL.V // KV_WIDTH)),
                  pl.BlockSpec((WINDOW, KV_WIDTH), lambda b, n: (row(b, n), _COL.V // KV_WIDTH)),
                  pl.BlockSpec((WINDOW, ATTN_WIDTH), lambda b, n: (row(b, n), _COL.GA // ATTN_WIDTH)),
                  pl.BlockSpec((Q_HEADS, 2 * WINDOW, WINDOW), lambda b, n: (0, 0, 0))],
        out_specs=pl.BlockSpec((WINDOW, ATTN_WIDTH), lambda b, n: (row(b, n), 0)),
        out_shape=jax.ShapeDtypeStruct((batch * seq, ATTN_WIDTH), BF16),
        compiler_params=_cparams(2),
        name="attn_prompt",
    )(sinks, u, u, u, u, u, u, bias)


def _attn_sample_kernel(q_ref, ga_ref, kvt_ref, ck_ref, cv_ref, bias_ref, o_ref, nk_ref, nv_ref,
                        s_ref, p_ref, seq_len):
    bt = SAMPLE_BT
    w_buf = ck_ref.shape[-1]
    keep = w_buf - seq_len
    unit_rows = REP * seq_len
    seq_rows = KV_HEADS * unit_rows
    lane0 = (pl.program_id(0) * bt * seq_len) % kvt_ref.shape[1]
    units = [(b, g) for b in range(bt) for g in range(KV_HEADS)]

    lane = lax.broadcasted_iota(jnp.int32, (KV_WIDTH, w_buf), 1)
    for c_ref, n_ref, row0 in ((ck_ref, nk_ref, 0), (cv_ref, nv_ref, KV_WIDTH)):
        fresh = kvt_ref[row0:row0 + KV_WIDTH, :]
        for b in range(bt):
            new_shift = (keep + w_buf - (lane0 + b * seq_len)) % w_buf
            old = c_ref[b].reshape(KV_WIDTH, w_buf)
            new = jnp.where(lane < keep, pltpu.roll(old, keep, axis=1), pltpu.roll(fresh, new_shift, axis=1))
            n_ref[b] = new.reshape(KV_HEADS, HEAD_DIM, w_buf)

    def both(c_ref, n_ref, b, g):
        return jnp.concatenate([c_ref[b, g], n_ref[b, g]], axis=1).astype(BF16)

    for u, (b, g) in enumerate(units):
        rs = slice(b * seq_len, (b + 1) * seq_len)
        qs = jnp.concatenate([q_ref[rs, (g * REP + r) * HEAD_DIM:(g * REP + r + 1) * HEAD_DIM] for r in range(REP)],
                             axis=0)
        s_ref[u * unit_rows:(u + 1) * unit_rows, :] = jnp.dot(
            (qs * ATTN_SCALE).astype(BF16), both(ck_ref, nk_ref, b, g), preferred_element_type=F32)

    col = lax.broadcasted_iota(jnp.int32, (seq_rows, 2 * w_buf), 1)
    not_sink = (col != seq_len).astype(F32)
    for b in range(bt):
        rows_b = slice(b * seq_rows, (b + 1) * seq_rows)
        s = s_ref[rows_b, :] * not_sink + bias_ref[...]
        m = jnp.max(s, axis=-1, keepdims=True)
        p = jnp.exp(s - m)
        inv = 1.0 / jnp.sum(p, axis=-1, keepdims=True)
        p_ref[rows_b, :] = (p * inv * not_sink).astype(BF16)

    for u, (b, g) in enumerate(units):
        o = _dot_nt(p_ref[u * unit_rows:(u + 1) * unit_rows, :], both(cv_ref, nv_ref, b, g))
        rs = slice(b * seq_len, (b + 1) * seq_len)
        for r in range(REP):
            hs = slice((g * REP + r) * HEAD_DIM, (g * REP + r + 1) * HEAD_DIM)
            o_ref[rs, hs] = (o[r * seq_len:(r + 1) * seq_len, :] * ga_ref[rs, hs]).astype(BF16)


def _attn_sample(u, kv_t, cache_k, cache_v, bias, row0, batch, seq):
    bt = SAMPLE_BT
    rows = bt * seq
    w_buf = cache_k.shape[-1]
    assert w_buf == LANES and row0 % rows == 0 and LANES % rows == 0
    blk0 = row0 // rows
    score_rows = bt * Q_HEADS * seq
    cache_spec = pl.BlockSpec((bt, KV_HEADS, HEAD_DIM, w_buf), lambda i: (i, 0, 0, 0))
    return pl.pallas_call(
        functools.partial(_attn_sample_kernel, seq_len=seq),
        grid=(batch // bt,),
        in_specs=[pl.BlockSpec((rows, ATTN_WIDTH), lambda i: (blk0 + i, _COL.Q // ATTN_WIDTH)),
                  pl.BlockSpec((rows, ATTN_WIDTH), lambda i: (blk0 + i, _COL.GA // ATTN_WIDTH)),
                  pl.BlockSpec((2 * KV_WIDTH, LANES), lambda i: (0, i * rows // LANES)),
                  cache_spec, cache_spec,
                  pl.BlockSpec(bias.shape, lambda i: (0, 0))],
        out_specs=[pl.BlockSpec((rows, ATTN_WIDTH), lambda i: (i, 0)), cache_spec, cache_spec],
        out_shape=[jax.ShapeDtypeStruct((batch * seq, ATTN_WIDTH), BF16),
                   jax.ShapeDtypeStruct(cache_k.shape, F32),
                   jax.ShapeDtypeStruct(cache_v.shape, F32)],
        scratch_shapes=[pltpu.VMEM((score_rows, 2 * w_buf), F32),
                        pltpu.VMEM((score_rows, 2 * w_buf), BF16)],
        compiler_params=_cparams(1),
        name="attn_sample",
    )(u, u, kv_t, cache_k, cache_v, bias)


MERGE_TN = 512
EPI_SPLIT = 4


def _merge_kernel(g_ref, og_ref, ms_ref, ma_ref, ws_ref, wa_ref, o_ref, *stage_refs):
    rows = o_ref.shape[0] // EPI_SPLIT
    ssm_refs, attn_refs = stage_refs[:EPI_SPLIT], stage_refs[EPI_SPLIT:]

    def gate_chunk(n):
        rs = slice(n * rows, (n + 1) * rows)
        o_ref[rs, :] = (_sigmoid(ms_ref[rs, :]) * ssm_refs[n][...]
                        + _sigmoid(ma_ref[rs, :]) * attn_refs[n][...]).astype(BF16)

    for n in range(EPI_SPLIT):
        rs = slice(n * rows, (n + 1) * rows)
        ssm_refs[n][...] = jnp.dot(g_ref[rs, :], ws_ref[...], preferred_element_type=F32)
        attn_refs[n][...] = jnp.dot(og_ref[rs, :], wa_ref[...], preferred_element_type=F32)
        if n > 0:
            gate_chunk(n - 1)
    gate_chunk(EPI_SPLIT - 1)


def _merge(g, og, u, w_ssm, w_attn, row0, tm):
    t = g.shape[0]
    tn = MERGE_TN
    assert row0 % tm == 0
    blk0 = row0 // tm
    return pl.pallas_call(
        _merge_kernel,
        grid=(t // tm, D_MODEL // tn),
        in_specs=[pl.BlockSpec((tm, D_INNER), lambda i, j: (i, 0)),
                  pl.BlockSpec((tm, ATTN_WIDTH), lambda i, j: (i, 0)),
                  pl.BlockSpec((tm, tn), lambda i, j: (blk0 + i, _COL.MS // tn + j)),
                  pl.BlockSpec((tm, tn), lambda i, j: (blk0 + i, _COL.MA // tn + j)),
                  pl.BlockSpec((D_INNER, tn), lambda i, j: (0, j)),
                  pl.BlockSpec((ATTN_WIDTH, tn), lambda i, j: (0, j))],
        out_specs=pl.BlockSpec((tm, tn), lambda i, j: (i, j)),
        out_shape=jax.ShapeDtypeStruct((t, D_MODEL), BF16),
        scratch_shapes=[pltpu.VMEM((tm // EPI_SPLIT, tn), F32) for _ in range(2 * EPI_SPLIT)],
        compiler_params=_cparams(2, BIG_VMEM_LIMIT),
        name="merge",
    )(g, og, u, u, w_ssm, w_attn)


def _out_proj_kernel(m_ref, x_ref, wo_ref, fw_ref, o_ref, *stage_refs):
    rows = o_ref.shape[0] // EPI_SPLIT

    def norm_chunk(n):
        rs = slice(n * rows, (n + 1) * rows)
        xn = x_ref[rs, :] + stage_refs[n][...]
        ms = jnp.mean(xn * xn, axis=-1, keepdims=True)
        o_ref[rs, :] = xn * lax.rsqrt(ms + RMS_EPS) * fw_ref[...]

    for n in range(EPI_SPLIT):
        stage_refs[n][...] = jnp.dot(m_ref[n * rows:(n + 1) * rows, :], wo_ref[...], preferred_element_type=F32)
        if n > 0:
            norm_chunk(n - 1)
    norm_chunk(EPI_SPLIT - 1)


def _out_proj(m, x2d, w_out, final_w, tm):
    t = m.shape[0]
    return pl.pallas_call(
        _out_proj_kernel,
        grid=(t // tm,),
        in_specs=[pl.BlockSpec((tm, D_MODEL), lambda i: (i, 0)),
                  pl.BlockSpec((tm, D_MODEL), lambda i: (i, 0)),
                  pl.BlockSpec((D_MODEL, D_MODEL), lambda i: (0, 0)),
                  pl.BlockSpec((1, D_MODEL), lambda i: (0, 0))],
        out_specs=pl.BlockSpec((tm, D_MODEL), lambda i: (i, 0)),
        out_shape=jax.ShapeDtypeStruct((t, D_MODEL), F32),
        scratch_shapes=[pltpu.VMEM((tm // EPI_SPLIT, D_MODEL), F32) for _ in range(EPI_SPLIT)],
        compiler_params=_cparams(1),
        name="out_proj",
    )(m, x2d, w_out, final_w)


def _pad_lanes(v):
    return jnp.pad(v.reshape(1, -1), ((0, 0), (0, DT_PAD - v.shape[-1])))


def kernel(x_prompt, x_sample, cache_k, cache_v, state_conv, state_ssm, norm_w, w_in, conv_w, conv_b, dt_bias, a_log,
           d_skip, ssm_norm_w, w_ssm_branch, attn_sinks, w_attn_branch, w_out, rel_bias, final_norm_w):
    assert w_in.shape[0] == 1, "single-layer kernel"
    batch, seq, _ = x_prompt.shape
    dec_batch, dec_seq, _ = x_sample.shape
    w_buf = cache_k.shape[2]
    t_p, t_s = batch * seq, dec_batch * dec_seq
    assert seq % SSD_CHUNK == 0 and seq % WINDOW == 0 and dec_seq == SUBLANES and w_buf == WINDOW
    assert dec_batch % SAMPLE_BT == 0 and (t_p + t_s) % IN_TM == 0

    w_t = jnp.transpose(w_in[0])
    sc_t = jnp.transpose(state_conv[0], (1, 0, 2))
    ck_t = jnp.transpose(cache_k[0], (0, 2, 3, 1))
    cv_t = jnp.transpose(cache_v[0], (0, 2, 3, 1))

    nw = norm_w[0].reshape(1, D_MODEL)
    cw, cb = conv_w[0], conv_b[0].reshape(1, CONV_DIM)
    dtb, alog = _pad_lanes(dt_bias[0]), _pad_lanes(a_log[0])
    dskip_cols = jnp.repeat(d_skip[0], SSM_HEAD_DIM).reshape(1, D_INNER)
    snw = ssm_norm_w[0].reshape(1, D_INNER)
    w_ssm = w_ssm_branch[0].astype(BF16)
    w_attn = w_attn_branch[0].astype(BF16)
    wo = w_out[0].astype(BF16)
    fw = final_norm_w.reshape(1, D_MODEL)
    sinks = attn_sinks[0]

    xp2 = x_prompt.reshape(t_p, D_MODEL)
    xs2 = x_sample.reshape(t_s, D_MODEL)

    band_bias_t = _band_bias_t(rel_bias)
    dec_bias = _decode_bias(rel_bias, sinks, dec_seq, w_buf)

    h, dt_raw = _norm_dt(xp2, xs2, nw, w_t)
    u = _in_proj(h, w_t)
    kv_t = _kv_t(h, w_t, t_p, t_s)

    g_p, st_p = _ssd_prompt(u, dt_raw, cw, cb, dtb, alog, dskip_cols, snw, batch, seq)
    og_p = _attn_prompt(u, sinks, band_bias_t, batch, seq)
    m_p = _merge(g_p, og_p, u, w_ssm, w_attn, 0, tm=1024)
    y_p = _out_proj(m_p, xp2, wo, fw, tm=512)

    xc_s, nsc_t = _conv_sample(u, sc_t, cw, cb, t_p, dec_batch, dec_seq)
    g_s, st_s = _ssd_sample(u, dt_raw, xc_s, state_ssm[0].reshape(dec_batch, D_INNER, SSM_STATE),
                            dtb, alog, dskip_cols, snw, t_p, dec_batch, dec_seq)
    og_s, nk_t, nv_t = _attn_sample(u, kv_t, ck_t, cv_t, dec_bias, t_p, dec_batch, dec_seq)
    m_s = _merge(g_s, og_s, u, w_ssm, w_attn, t_p, tm=1024)
    y_s = _out_proj(m_s, xs2, wo, fw, tm=512)

    def seq_tail(n_rows, col0, width):
        return jnp.stack([u[(b + 1) * seq - n_rows:(b + 1) * seq, col0:col0 + width] for b in range(batch)])

    k_p = seq_tail(WINDOW, _COL.K, KV_WIDTH).reshape(1, batch, WINDOW, KV_HEADS, HEAD_DIM)
    v_p = seq_tail(WINDOW, _COL.V, KV_WIDTH).reshape(1, batch, WINDOW, KV_HEADS, HEAD_DIM)
    conv_p = seq_tail(CONV_HALO, _COL.XS, CONV_DIM)[None]
    conv_s = jnp.transpose(nsc_t, (1, 0, 2))[None]
    ssm_p = st_p.reshape(1, batch, SSM_HEADS, SSM_HEAD_DIM, SSM_STATE)
    ssm_s = st_s.reshape(1, dec_batch, SSM_HEADS, SSM_HEAD_DIM, SSM_STATE)
    k_s = jnp.transpose(nk_t, (0, 3, 1, 2))[None]
    v_s = jnp.transpose(nv_t, (0, 3, 1, 2))[None]
    return (y_p.reshape(batch, seq, D_MODEL), y_s.reshape(dec_batch, dec_seq, D_MODEL),
            k_p, v_p, conv_p, ssm_p, k_s, v_s, conv_s, ssm_s)
```

```python
import functools
import math

import numpy as np
import jax
import jax.numpy as jnp
from jax import lax
from jax.experimental import pallas as pl
from jax.experimental.pallas import tpu as pltpu

F32 = jnp.float32
BF16 = jnp.bfloat16
HIGHEST = lax.Precision.HIGHEST

D_MODEL = 2048
D_INNER = 4096
SSM_HEAD_DIM = 64
SSM_HEADS = 64
SSM_GROUPS = 8
SSM_STATE = 128
CONV_WIDTH = 4
CONV_HALO = CONV_WIDTH - 1
GN = SSM_GROUPS * SSM_STATE
CONV_DIM = D_INNER + 2 * GN
SSD_CHUNK = 128
GROUP_W = D_INNER // SSM_GROUPS
HEAD_DIM = 64
Q_HEADS = 32
KV_HEADS = 8
REP = Q_HEADS // KV_HEADS
ATTN_WIDTH = Q_HEADS * HEAD_DIM
KV_WIDTH = KV_HEADS * HEAD_DIM
WINDOW = 128
N_BUCKETS = 32
MAX_EXACT = N_BUCKETS // 2
RMS_EPS = 1e-6

_OFF_Z = 0
_OFF_XBC = _OFF_Z + D_INNER
_OFF_DT = _OFF_XBC + CONV_DIM
_OFF_Q = _OFF_DT + SSM_HEADS
_OFF_K = _OFF_Q + ATTN_WIDTH
_OFF_V = _OFF_K + KV_WIDTH
_OFF_GA = _OFF_V + KV_WIDTH
_OFF_MS = _OFF_GA + ATTN_WIDTH
_OFF_MA = _OFF_MS + D_MODEL
_IN_COLS = _OFF_MA + D_MODEL

LANES = 128
SUBLANES = 8
DT_PAD = LANES


class _COL:
    Z = 0
    XS = Z + D_INNER
    BC = XS + D_INNER
    Q = BC + 2 * GN
    GA = Q + ATTN_WIDTH
    MS = GA + ATTN_WIDTH
    MA = MS + D_MODEL
    K = MA + D_MODEL
    V = K + KV_WIDTH
    TOTAL = V + KV_WIDTH


VMEM_LIMIT = 52 * 1024 * 1024
BIG_VMEM_LIMIT = 57 * 1024 * 1024


def _cparams(n_grid, limit=VMEM_LIMIT):
    return pltpu.CompilerParams(dimension_semantics=("arbitrary",) * n_grid, vmem_limit_bytes=limit)


def _silu(v):
    return v * (1.0 / (1.0 + jnp.exp(-v)))


def _sigmoid(v):
    return 1.0 / (1.0 + jnp.exp(-v))


def _dot_nt(a, b):
    return lax.dot_general(a, b, (((1,), (1,)), ((), ())), preferred_element_type=F32)


NORM_TM = 512


def _norm_dt_kernel(xp_ref, xs_ref, nw_ref, wdt_ref, h_ref, dt_ref, n_prompt_tiles):
    i = pl.program_id(0)

    def emit(x_ref):
        xf = x_ref[...]
        ms = jnp.mean(xf * xf, axis=-1, keepdims=True)
        h = (xf * lax.rsqrt(ms + RMS_EPS) * nw_ref[...]).astype(BF16)
        h_ref[...] = h
        dt_ref[...] = _dot_nt(h, wdt_ref[...].astype(BF16))

    @pl.when(i < n_prompt_tiles)
    def _():
        emit(xp_ref)

    @pl.when(i >= n_prompt_tiles)
    def _():
        emit(xs_ref)


def _norm_dt(xp2, xs2, norm_w, w_t):
    tp, ts = xp2.shape[0], xs2.shape[0]
    n_p, n_s = tp // NORM_TM, ts // NORM_TM
    return pl.pallas_call(
        functools.partial(_norm_dt_kernel, n_prompt_tiles=n_p),
        grid=(n_p + n_s,),
        in_specs=[pl.BlockSpec((NORM_TM, D_MODEL), lambda i: (jnp.minimum(i, n_p - 1), 0)),
                  pl.BlockSpec((NORM_TM, D_MODEL), lambda i: (jnp.maximum(i - n_p, 0), 0)),
                  pl.BlockSpec((1, D_MODEL), lambda i: (0, 0)),
                  pl.BlockSpec((DT_PAD, D_MODEL), lambda i: (_OFF_DT // DT_PAD, 0))],
        out_specs=[pl.BlockSpec((NORM_TM, D_MODEL), lambda i: (i, 0)),
                   pl.BlockSpec((NORM_TM, DT_PAD), lambda i: (i, 0))],
        out_shape=[jax.ShapeDtypeStruct((tp + ts, D_MODEL), BF16),
                   jax.ShapeDtypeStruct((tp + ts, DT_PAD), F32)],
        compiler_params=_cparams(1),
        name="norm_dt",
    )(xp2, xs2, norm_w, w_t)


IN_TM = 1536
IN_TN = 1024
IN_CHUNK = 512
assert _COL.TOTAL % IN_TN == 0 and IN_TN == 2 * IN_CHUNK and _OFF_DT % IN_CHUNK == 0


def _src_row(c):
    straight = _OFF_DT // IN_CHUNK
    jj = c - straight
    q_t, kv_t = ATTN_WIDTH // IN_CHUNK, 2 * KV_WIDTH // IN_CHUNK
    gate_t = (ATTN_WIDTH + 2 * D_MODEL) // IN_CHUNK
    shifted = jnp.where(jj < q_t, jj, jnp.where(jj < q_t + gate_t, jj + kv_t, jj - gate_t))
    return pl.multiple_of(jnp.where(c < straight, c * IN_CHUNK, _OFF_Q + shifted * IN_CHUNK), SSM_HEADS)


IN_NSPLIT = 4
IN_MW = IN_TM // IN_NSPLIT
J_Z1 = _COL.XS // IN_TN
J_GA0, J_GA1 = _COL.GA // IN_TN, _COL.MS // IN_TN
assert _COL.XS % IN_TN == 0 and _COL.GA % IN_TN == 0 and _COL.MS % IN_TN == 0


def _in_proj_kernel(h_ref, w0_ref, w1_ref, o_ref, wbf_ref, *stage_refs):
    j = pl.program_id(0)

    @pl.when(pl.program_id(1) == 0)
    def _():
        wbf_ref[0:IN_CHUNK, :] = w0_ref[...].astype(BF16)
        wbf_ref[IN_CHUNK:IN_TN, :] = w1_ref[...].astype(BF16)

    is_silu_tile = (j < J_Z1) | ((j >= J_GA0) & (j < J_GA1))

    @pl.when(is_silu_tile)
    def _():
        def silu_chunk(n):
            o_ref[n * IN_MW:(n + 1) * IN_MW, :] = _silu(stage_refs[n][...])

        for n in range(IN_NSPLIT):
            stage_refs[n][...] = _dot_nt(h_ref[n * IN_MW:(n + 1) * IN_MW, :], wbf_ref[...])
            if n > 0:
                silu_chunk(n - 1)
        silu_chunk(IN_NSPLIT - 1)

    @pl.when(jnp.logical_not(is_silu_tile))
    def _():
        o_ref[...] = _dot_nt(h_ref[...], wbf_ref[...])


def _in_proj(h, w_t):
    t = h.shape[0]
    return pl.pallas_call(
        _in_proj_kernel,
        grid=(_COL.TOTAL // IN_TN, t // IN_TM),
        in_specs=[pl.BlockSpec((IN_TM, D_MODEL), lambda j, i: (i, 0)),
                  pl.BlockSpec((pl.Element(IN_CHUNK), pl.Element(D_MODEL)), lambda j, i: (_src_row(2 * j), 0)),
                  pl.BlockSpec((pl.Element(IN_CHUNK), pl.Element(D_MODEL)), lambda j, i: (_src_row(2 * j + 1), 0))],
        out_specs=pl.BlockSpec((IN_TM, IN_TN), lambda j, i: (i, j)),
        out_shape=jax.ShapeDtypeStruct((t, _COL.TOTAL), F32),
        scratch_shapes=[pltpu.VMEM((IN_TN, D_MODEL), BF16)]
        + [pltpu.VMEM((IN_MW, IN_TN), F32) for _ in range(IN_NSPLIT)],
        compiler_params=_cparams(2, BIG_VMEM_LIMIT),
        name="in_proj",
    )(h, w_t, w_t)


def _kv_t_kernel(w_ref, h_ref, o_ref):
    o_ref[...] = _dot_nt(w_ref[...].astype(BF16), h_ref[...])


def _kv_t(h, w_t, row0, rows):
    assert row0 % rows == 0
    return pl.pallas_call(
        _kv_t_kernel,
        grid=(1,),
        in_specs=[pl.BlockSpec((pl.Element(2 * KV_WIDTH), pl.Element(D_MODEL)), lambda i: (_OFF_K, 0)),
                  pl.BlockSpec((rows, D_MODEL), lambda i: (row0 // rows, 0))],
        out_specs=pl.BlockSpec((2 * KV_WIDTH, rows), lambda i: (0, 0)),
        out_shape=jax.ShapeDtypeStruct((2 * KV_WIDTH, rows), F32),
        compiler_params=_cparams(1),
        name="kv_t",
    )(w_t, h)


def _bucket_of_dist(dist):
    n = np.maximum(dist, 0)
    nf = np.maximum(n, 1).astype(np.float32)
    large = MAX_EXACT + (np.log(nf / MAX_EXACT) / math.log(WINDOW / MAX_EXACT)
                         * (N_BUCKETS - MAX_EXACT)).astype(np.int32)
    large = np.minimum(large, N_BUCKETS - 1)
    bucket = np.where(n < MAX_EXACT, n, large)
    return np.where((dist >= 0) & (dist <= WINDOW), bucket, -1).astype(np.int32)


def _bias_of_bucket(bk, table_ref, h):
    acc = jnp.full(bk.shape, -jnp.inf, F32)
    for b in range(N_BUCKETS):
        acc = jnp.where(bk == b, table_ref[b, h], acc)
    return acc


BAND_BIAS_HEADS = 8


def _band_bias_kernel(table_ref, bucket_ref, o_ref):
    for hh in range(BAND_BIAS_HEADS):
        o_ref[hh] = _bias_of_bucket(bucket_ref[...], table_ref, pl.program_id(0) * BAND_BIAS_HEADS + hh)


def _band_bias_t(rel_bias):
    kj = np.arange(2 * WINDOW)[:, None]
    qi = np.arange(WINDOW)[None, :]
    bucket = jnp.asarray(_bucket_of_dist(qi + WINDOW - kj))
    return pl.pallas_call(
        _band_bias_kernel,
        grid=(Q_HEADS // BAND_BIAS_HEADS,),
        in_specs=[pl.BlockSpec(memory_space=pltpu.SMEM),
                  pl.BlockSpec((2 * WINDOW, WINDOW), lambda h: (0, 0))],
        out_specs=pl.BlockSpec((BAND_BIAS_HEADS, 2 * WINDOW, WINDOW), lambda h: (h, 0, 0)),
        out_shape=jax.ShapeDtypeStruct((Q_HEADS, 2 * WINDOW, WINDOW), F32),
        compiler_params=_cparams(1),
        name="band_bias",
    )(rel_bias, bucket)


def _decode_bias_kernel(table_ref, sink_ref, bucket_ref, o_ref, seq):
    g = pl.program_id(0)
    sink_lane = lax.broadcasted_iota(jnp.int32, bucket_ref.shape, 1) == seq
    for r in range(REP):
        h = g * REP + r
        bias = _bias_of_bucket(bucket_ref[...], table_ref, h)
        o_ref[r * seq:(r + 1) * seq, :] = jnp.where(sink_lane, sink_ref[h], bias)


def _decode_bias(rel_bias, sinks, seq, w_buf):
    l = np.arange(seq)[:, None]
    j = np.arange(w_buf)[None, :]
    dist_old = np.where((j < seq), l + w_buf - j, -1)
    dist_new = np.where(j < w_buf - seq, l + w_buf - (j + seq), l - (j - (w_buf - seq)))
    bucket = jnp.asarray(np.concatenate([_bucket_of_dist(dist_old), _bucket_of_dist(dist_new)], axis=1))
    return pl.pallas_call(
        functools.partial(_decode_bias_kernel, seq=seq),
        grid=(KV_HEADS,),
        in_specs=[pl.BlockSpec(memory_space=pltpu.SMEM), pl.BlockSpec(memory_space=pltpu.SMEM),
                  pl.BlockSpec(bucket.shape, lambda g: (0, 0))],
        out_specs=pl.BlockSpec((REP * seq, 2 * w_buf), lambda g: (g, 0)),
        out_shape=jax.ShapeDtypeStruct((Q_HEADS * seq, 2 * w_buf), F32),
        compiler_params=_cparams(1),
        name="decode_bias",
    )(rel_bias, sinks, bucket)


CONV_STRIP = 512


def _conv_silu(ext_ref, base, rows, cw_ref, cb_ref, xc_ref, out_row):
    for c0 in range(0, CONV_DIM, CONV_STRIP):
        cs = slice(c0, c0 + CONV_STRIP)
        acc = cb_ref[:, cs] + cw_ref[3:4, cs] * ext_ref[base:base + rows, cs]
        for k in range(1, CONV_WIDTH):
            acc = acc + cw_ref[3 - k:4 - k, cs] * ext_ref[base - k:base - k + rows, cs]
        xc_ref[out_row:out_row + rows, cs] = _silu(acc)


CONV_S_SEQS = 16


def _conv_sample_kernel(xs_ref, bc_ref, sc_ref, cw_ref, cb_ref, xc_ref, nsc_ref, ext_ref, seq_len):
    slab = SUBLANES + seq_len
    for b in range(CONV_S_SEQS):
        base = b * slab + SUBLANES
        for k in range(CONV_HALO):
            ext_ref[base - CONV_HALO + k:base - CONV_HALO + k + 1, :] = sc_ref[k, b:b + 1, :]
        ext_ref[base:base + seq_len, 0:D_INNER] = xs_ref[b * seq_len:(b + 1) * seq_len, :]
        ext_ref[base:base + seq_len, D_INNER:CONV_DIM] = bc_ref[b * seq_len:(b + 1) * seq_len, :]
        for k in range(CONV_HALO):
            row = base + seq_len - CONV_HALO + k
            nsc_ref[k, b:b + 1, :] = ext_ref[row:row + 1, :]
    for c0 in range(0, CONV_DIM, CONV_STRIP):
        cs = slice(c0, c0 + CONV_STRIP)
        bias = jnp.broadcast_to(cb_ref[:, cs], (seq_len, CONV_STRIP))
        taps = [jnp.broadcast_to(cw_ref[CONV_HALO - k:CONV_WIDTH - k, cs], (seq_len, CONV_STRIP))
                for k in range(CONV_WIDTH)]
        for b in range(CONV_S_SEQS):
            base = b * slab + SUBLANES
            acc = bias
            for k in range(CONV_WIDTH):
                acc = acc + taps[k] * ext_ref[base - k:base - k + seq_len, cs]
            xc_ref[b * seq_len:(b + 1) * seq_len, cs] = _silu(acc)


def _conv_sample(u, sc_t, conv_w, conv_b, row0, batch, seq):
    assert seq == SUBLANES and batch % CONV_S_SEQS == 0
    rows = CONV_S_SEQS * seq
    assert row0 % rows == 0
    blk0 = row0 // rows
    full = lambda shape: pl.BlockSpec(shape, lambda i: (0,) * len(shape))
    return pl.pallas_call(
        functools.partial(_conv_sample_kernel, seq_len=seq),
        grid=(batch // CONV_S_SEQS,),
        in_specs=[pl.BlockSpec((rows, D_INNER), lambda i: (blk0 + i, _COL.XS // D_INNER)),
                  pl.BlockSpec((rows, 2 * GN), lambda i: (blk0 + i, _COL.BC // (2 * GN))),
                  pl.BlockSpec((CONV_HALO, CONV_S_SEQS, CONV_DIM), lambda i: (0, i, 0)),
                  full((CONV_WIDTH, CONV_DIM)), full((1, CONV_DIM))],
        out_specs=[pl.BlockSpec((rows, CONV_DIM), lambda i: (i, 0)),
                   pl.BlockSpec((CONV_HALO, CONV_S_SEQS, CONV_DIM), lambda i: (0, i, 0))],
        out_shape=[jax.ShapeDtypeStruct((batch * seq, CONV_DIM), F32),
                   jax.ShapeDtypeStruct((CONV_HALO, batch, CONV_DIM), F32)],
        scratch_shapes=[pltpu.VMEM((CONV_S_SEQS * (SUBLANES + seq), CONV_DIM), F32)],
        compiler_params=_cparams(1),
        name="conv_sample",
    )(u, u, sc_t, conv_w, conv_b)


def _ssd_core(rows, same_seq, zs_ref, dt_ref, dtb_ref, alog_ref, dskip_ref, nw_ref, xs_ref, bc_ref, xd_ref, y_ref,
              yoff_fn, state_fn, g_ref):
    li = lax.broadcasted_iota(jnp.int32, (rows, rows), 0)
    si = lax.broadcasted_iota(jnp.int32, (rows, rows), 1)
    causal = same_seq & (si <= li)
    lane = lax.broadcasted_iota(jnp.int32, (rows, LANES), 1)
    first_half = lane < SSM_HEAD_DIM

    dt = jax.nn.softplus(dt_ref[...] + dtb_ref[...])
    a_neg = -jnp.exp(alog_ref[...])
    d_a = dt * a_neg
    acs = jnp.dot(causal.astype(F32), d_a, precision=HIGHEST, preferred_element_type=F32)
    a_end = jnp.dot(same_seq.astype(F32), d_a, precision=HIGHEST, preferred_element_type=F32)
    log_dt = jnp.log(dt)
    seg_rows = (acs - log_dt).T
    log_w_state = log_dt + a_end - acs
    cd_rows = jnp.exp(a_end)
    e_acs_tab, w_state_tab = jnp.exp(acs), jnp.exp(log_w_state)

    def head_cols(m, j):
        return (jnp.broadcast_to(m[:, 2 * j:2 * j + 1], (rows, LANES)),
                jnp.broadcast_to(m[:, 2 * j + 1:2 * j + 2], (rows, LANES)))

    pairs_per_group = SSM_HEADS // SSM_GROUPS // 2
    for g in range(SSM_GROUPS):
        b_g = bc_ref[:, g * SSM_STATE:(g + 1) * SSM_STATE].astype(BF16)
        c_g = bc_ref[:, GN + g * SSM_STATE:GN + (g + 1) * SSM_STATE].astype(BF16)
        cb = _dot_nt(c_g, b_g)
        yoff = yoff_fn(g, c_g)
        for jj in range(pairs_per_group):
            j = g * pairs_per_group + jj
            ps = slice(j * LANES, (j + 1) * LANES)
            xs_pair = xs_ref[:, ps]
            xs_bf16 = xs_pair.astype(BF16)
            acs_cols = head_cols(acs, j)
            parts = []
            for hh in range(2):
                h = 2 * j + hh
                col = acs_cols[hh] if rows == LANES else acs[:, h:h + 1]
                seg = col - seg_rows[h:h + 1, :]
                decay_dt = jnp.exp(jnp.where(causal, seg, -jnp.inf))
                m = (cb * decay_dt).astype(BF16)
                parts.append(jnp.dot(m, xs_bf16, preferred_element_type=F32))
            yd = jnp.where(first_half, parts[0], parts[1])
            if rows == LANES:
                e_acs = jnp.exp(jnp.where(first_half, *acs_cols))
                w_state = jnp.exp(jnp.where(first_half, *head_cols(log_w_state, j)))
            else:
                e_acs = jnp.where(first_half, *head_cols(e_acs_tab, j))
                w_state = jnp.where(first_half, *head_cols(w_state_tab, j))
            xd_ref[:, ps] = (xs_pair * w_state).astype(xd_ref.dtype)
            y_ref[:, ps] = yd + yoff[:, jj * LANES:(jj + 1) * LANES] * e_acs + dskip_ref[:, ps] * xs_pair
        state_fn(g, cd_rows)

    for g in range(SSM_GROUPS):
        gs = slice(g * GROUP_W, (g + 1) * GROUP_W)
        gg = y_ref[:, gs] * zs_ref[:, gs]
        ms = jnp.mean(gg * gg, axis=-1, keepdims=True)
        g_ref[:, gs] = (gg * lax.rsqrt(ms + RMS_EPS) * nw_ref[:, gs]).astype(BF16)


def _ssd_prompt_kernel(zs_ref, xs_raw_ref, bc_raw_ref, dt_ref, cw_ref, cb_ref, dtb_ref, alog_ref, dskip_ref, nw_ref,
                       g_ref, st_ref, ext_ref, xc_ref, xd_ref, y_ref, stt_ref):
    c = pl.program_id(1)
    rows = SSD_CHUNK

    @pl.when(c == 0)
    def _():
        ext_ref[0:SUBLANES, :] = jnp.zeros((SUBLANES, CONV_DIM), F32)
        stt_ref[...] = jnp.zeros(stt_ref.shape, F32)

    @pl.when(c != 0)
    def _():
        ext_ref[0:SUBLANES, :] = ext_ref[rows:rows + SUBLANES, :]

    ext_ref[SUBLANES:SUBLANES + rows, 0:D_INNER] = xs_raw_ref[...]
    ext_ref[SUBLANES:SUBLANES + rows, D_INNER:CONV_DIM] = bc_raw_ref[...]
    _conv_silu(ext_ref, SUBLANES, rows, cw_ref, cb_ref, xc_ref, 0)
    xs_ref = xc_ref.at[:, 0:D_INNER]
    bc_ref = xc_ref.at[:, D_INNER:CONV_DIM]

    def yoff_fn(g, c_g):
        return jnp.dot(c_g, stt_ref[:, g * GROUP_W:(g + 1) * GROUP_W].astype(BF16), preferred_element_type=F32)

    def state_fn(g, cd_rows):
        b_g = bc_ref[:, g * SSM_STATE:(g + 1) * SSM_STATE].astype(BF16)
        xd_g = xd_ref[:, g * GROUP_W:(g + 1) * GROUP_W]
        new = lax.dot_general(b_g, xd_g, (((0,), (0,)), ((), ())), preferred_element_type=F32)
        lane = lax.broadcasted_iota(jnp.int32, (SSM_STATE, LANES), 1)
        for jj in range(GROUP_W // LANES):
            j = g * (GROUP_W // LANES) + jj
            ps = slice(j * LANES, (j + 1) * LANES)
            cd = jnp.where(lane < SSM_HEAD_DIM,
                           jnp.broadcast_to(cd_rows[0:1, 2 * j:2 * j + 1], (SSM_STATE, LANES)),
                           jnp.broadcast_to(cd_rows[0:1, 2 * j + 1:2 * j + 2], (SSM_STATE, LANES)))
            stt_ref[:, ps] = stt_ref[:, ps] * cd + new[:, jj * LANES:(jj + 1) * LANES]

    same_seq = jnp.full((rows, rows), True)
    _ssd_core(rows, same_seq, zs_ref, dt_ref, dtb_ref, alog_ref, dskip_ref, nw_ref, xs_ref, bc_ref, xd_ref, y_ref,
              yoff_fn, state_fn, g_ref)

    @pl.when(c == pl.num_programs(1) - 1)
    def _():
        for j in range(D_INNER // LANES):
            st_ref[0, j * LANES:(j + 1) * LANES, :] = stt_ref[:, j * LANES:(j + 1) * LANES].T


def _ssd_prompt(u, dt_raw, conv_w, conv_b, dt_bias, a_log, dskip_cols, ssm_norm_w, batch, seq):
    nc = seq // SSD_CHUNK
    rows = SSD_CHUNK
    row_blk = lambda b, c: b * nc + c
    full = lambda shape: pl.BlockSpec(shape, lambda b, c: (0,) * len(shape))
    return pl.pallas_call(
        _ssd_prompt_kernel,
        grid=(batch, nc),
        in_specs=[pl.BlockSpec((rows, D_INNER), lambda b, c: (row_blk(b, c), _COL.Z // D_INNER)),
                  pl.BlockSpec((rows, D_INNER), lambda b, c: (row_blk(b, c), _COL.XS // D_INNER)),
                  pl.BlockSpec((rows, 2 * GN), lambda b, c: (row_blk(b, c), _COL.BC // (2 * GN))),
                  pl.BlockSpec((rows, DT_PAD), lambda b, c: (row_blk(b, c), 0)),
                  full((CONV_WIDTH, CONV_DIM)), full((1, CONV_DIM)), full((1, DT_PAD)), full((1, DT_PAD)),
                  full((1, D_INNER)), full((1, D_INNER))],
        out_specs=[pl.BlockSpec((rows, D_INNER), lambda b, c: (row_blk(b, c), 0)),
                   pl.BlockSpec((1, D_INNER, SSM_STATE), lambda b, c: (b, 0, 0))],
        out_shape=[jax.ShapeDtypeStruct((batch * seq, D_INNER), BF16),
                   jax.ShapeDtypeStruct((batch, D_INNER, SSM_STATE), F32)],
        scratch_shapes=[pltpu.VMEM((rows + SUBLANES, CONV_DIM), F32),
                        pltpu.VMEM((rows, CONV_DIM), F32),
                        pltpu.VMEM((rows, D_INNER), BF16),
                        pltpu.VMEM((rows, D_INNER), F32),
                        pltpu.VMEM((SSM_STATE, D_INNER), F32)],
        compiler_params=_cparams(2),
        name="ssd_prompt",
    )(u, u, u, dt_raw, conv_w, conv_b, dt_bias, a_log, dskip_cols, ssm_norm_w)


SAMPLE_BT = 4


def _ssd_sample_kernel(zs_ref, xs_ref, bc_ref, dt_ref, st_ref, dtb_ref, alog_ref, dskip_ref, nw_ref,
                       g_ref, nst_ref, xd_ref, y_ref, yoff_ref, seq_len):
    bt = SAMPLE_BT
    rows = bt * seq_len
    heads_per_group = SSM_HEADS // SSM_GROUPS

    for b in range(bt):
        rs = slice(b * seq_len, (b + 1) * seq_len)
        for g in range(SSM_GROUPS):
            c_bg = bc_ref[rs, GN + g * SSM_STATE:GN + (g + 1) * SSM_STATE].astype(BF16)
            s_bg = st_ref[b, g * GROUP_W:(g + 1) * GROUP_W, :].astype(BF16)
            yoff_ref[rs, g * GROUP_W:(g + 1) * GROUP_W] = _dot_nt(c_bg, s_bg)

    def yoff_fn(g, c_g):
        return yoff_ref[:, g * GROUP_W:(g + 1) * GROUP_W]

    def state_fn(g, cd_rows):
        for b in range(bt):
            rs = slice(b * seq_len, (b + 1) * seq_len)
            b_bg = bc_ref[rs, g * SSM_STATE:(g + 1) * SSM_STATE].astype(BF16)
            xd_bg = xd_ref[rs, g * GROUP_W:(g + 1) * GROUP_W].astype(BF16)
            new = lax.dot_general(xd_bg, b_bg, (((0,), (0,)), ((), ())), preferred_element_type=F32)
            for hh in range(heads_per_group):
                h = g * heads_per_group + hh
                hs = slice(h * SSM_HEAD_DIM, (h + 1) * SSM_HEAD_DIM)
                cd = jnp.broadcast_to(cd_rows[b * seq_len:b * seq_len + 1, h:h + 1], (SSM_HEAD_DIM, SSM_STATE))
                nst_ref[b, hs, :] = st_ref[b, hs, :] * cd + new[hh * SSM_HEAD_DIM:(hh + 1) * SSM_HEAD_DIM, :]

    li = lax.broadcasted_iota(jnp.int32, (rows, rows), 0)
    si = lax.broadcasted_iota(jnp.int32, (rows, rows), 1)
    same_seq = (li // seq_len) == (si // seq_len)
    _ssd_core(rows, same_seq, zs_ref, dt_ref, dtb_ref, alog_ref, dskip_ref, nw_ref, xs_ref, bc_ref, xd_ref, y_ref,
              yoff_fn, state_fn, g_ref)


def _ssd_sample(u, dt_raw, xc, state, dt_bias, a_log, dskip_cols, ssm_norm_w, row0, batch, seq):
    bt = SAMPLE_BT
    rows = bt * seq
    assert row0 % rows == 0
    blk0 = row0 // rows
    full = lambda shape: pl.BlockSpec(shape, lambda i: (0,) * len(shape))
    return pl.pallas_call(
        functools.partial(_ssd_sample_kernel, seq_len=seq),
        grid=(batch // bt,),
        in_specs=[pl.BlockSpec((rows, D_INNER), lambda i: (blk0 + i, _COL.Z // D_INNER)),
                  pl.BlockSpec((rows, D_INNER), lambda i: (i, 0)),
                  pl.BlockSpec((rows, 2 * GN), lambda i: (i, D_INNER // (2 * GN))),
                  pl.BlockSpec((rows, DT_PAD), lambda i: (blk0 + i, 0)),
                  pl.BlockSpec((bt, D_INNER, SSM_STATE), lambda i: (i, 0, 0)),
                  full((1, DT_PAD)), full((1, DT_PAD)), full((1, D_INNER)), full((1, D_INNER))],
        out_specs=[pl.BlockSpec((rows, D_INNER), lambda i: (i, 0)),
                   pl.BlockSpec((bt, D_INNER, SSM_STATE), lambda i: (i, 0, 0))],
        out_shape=[jax.ShapeDtypeStruct((batch * seq, D_INNER), BF16),
                   jax.ShapeDtypeStruct((batch, D_INNER, SSM_STATE), F32)],
        scratch_shapes=[pltpu.VMEM((rows, D_INNER), F32),
                        pltpu.VMEM((rows, D_INNER), F32),
                        pltpu.VMEM((rows, D_INNER), F32)],
        compiler_params=_cparams(1),
        name="ssd_sample",
    )(u, xc, xc, dt_raw, state, dt_bias, a_log, dskip_cols, ssm_norm_w)


ATTN_SCALE = HEAD_DIM ** -0.5


def _attn_prompt_kernel(sink_ref, q_ref, kp_ref, ko_ref, vp_ref, vo_ref, ga_ref, bias_ref, o_ref):
    n = pl.program_id(1)
    no_prev = jnp.where(n == 0, jnp.inf, 0.0)

    def scores(g):
        ks = slice(g * HEAD_DIM, (g + 1) * HEAD_DIM)
        kk = jnp.concatenate([kp_ref[:, ks], ko_ref[:, ks]], axis=0).astype(BF16)
        vv_t = jnp.concatenate([vp_ref[:, ks], vo_ref[:, ks]], axis=0).T.astype(BF16)
        s_list = []
        for r in range(REP):
            h = g * REP + r
            q = (q_ref[:, h * HEAD_DIM:(h + 1) * HEAD_DIM] * ATTN_SCALE).astype(BF16)
            s_list.append(_dot_nt(kk, q))
        return vv_t, s_list

    def finish(g, vv_t, s_list):
        probs = []
        for r in range(REP):
            h = g * REP + r
            s_prev = s_list[r][0:WINDOW] + bias_ref[h, 0:WINDOW, :]
            s_own = s_list[r][WINDOW:2 * WINDOW] + bias_ref[h, WINDOW:2 * WINDOW, :]
            sink = sink_ref[h]
            m = jnp.maximum(jnp.maximum(jnp.max(s_prev, axis=0, keepdims=True) - no_prev,
                                        jnp.max(s_own, axis=0, keepdims=True)), sink)
            p = jnp.concatenate([jnp.exp(s_prev - (m + no_prev)), jnp.exp(s_own - m)], axis=0)
            denom = jnp.sum(p, axis=0, keepdims=True) + jnp.exp(sink - m)
            probs.append((p.astype(BF16), denom))
        o_t = [jnp.dot(vv_t, p, preferred_element_type=F32) / denom for p, denom in probs]
        for rp in range(REP // 2):
            o_pair = jnp.concatenate(o_t[2 * rp:2 * rp + 2], axis=0).T
            ps = slice((g * REP + 2 * rp) * HEAD_DIM, (g * REP + 2 * rp + 2) * HEAD_DIM)
            o_ref[:, ps] = (o_pair * ga_ref[:, ps]).astype(BF16)

    pending = scores(0)
    for g in range(KV_HEADS):
        cur = pending
        if g + 1 < KV_HEADS:
            pending = scores(g + 1)
        finish(g, *cur)


def _attn_prompt(u, sinks, bias, batch, seq):
    nb = seq // WINDOW
    row = lambda b, n: b * nb + n
    prev = lambda b, n: jnp.maximum(b * nb + n - 1, 0)
    return pl.pallas_call(
        _attn_prompt_kernel,
        grid=(batch, nb),
        in_specs=[pl.BlockSpec(memory_space=pltpu.SMEM),
                  pl.BlockSpec((WINDOW, ATTN_WIDTH), lambda b, n: (row(b, n), _COL.Q // ATTN_WIDTH)),
                  pl.BlockSpec((WINDOW, KV_WIDTH), lambda b, n: (prev(b, n), _COL.K // KV_WIDTH)),
                  pl.BlockSpec((WINDOW, KV_WIDTH), lambda b, n: (row(b, n), _COL.K // KV_WIDTH)),
                  pl.BlockSpec((WINDOW, KV_WIDTH), lambda b, n: (prev(b, n), _COL.V // KV_WIDTH)),
                  pl.BlockSpec((WINDOW, KV_WIDTH), lambda b, n: (row(b, n), _COL.V // KV_WIDTH)),
                  pl.BlockSpec((WINDOW, ATTN_WIDTH), lambda b, n: (row(b, n), _COL.GA // ATTN_WIDTH)),
                  pl.BlockSpec((Q_HEADS, 2 * WINDOW, WINDOW), lambda b, n: (0, 0, 0))],
        out_specs=pl.BlockSpec((WINDOW, ATTN_WIDTH), lambda b, n: (row(b, n), 0)),
        out_shape=jax.ShapeDtypeStruct((batch * seq, ATTN_WIDTH), BF16),
        compiler_params=_cparams(2),
        name="attn_prompt",
    )(sinks, u, u, u, u, u, u, bias)


def _attn_sample_kernel(q_ref, ga_ref, kvt_ref, ck_ref, cv_ref, bias_ref, o_ref, nk_ref, nv_ref,
                        s_ref, p_ref, seq_len):
    bt = SAMPLE_BT
    w_buf = ck_ref.shape[-1]
    keep = w_buf - seq_len
    unit_rows = REP * seq_len
    seq_rows = KV_HEADS * unit_rows
    lane0 = (pl.program_id(0) * bt * seq_len) % kvt_ref.shape[1]
    units = [(b, g) for b in range(bt) for g in range(KV_HEADS)]

    lane = lax.broadcasted_iota(jnp.int32, (KV_WIDTH, w_buf), 1)
    for c_ref, n_ref, row0 in ((ck_ref, nk_ref, 0), (cv_ref, nv_ref, KV_WIDTH)):
        fresh = kvt_ref[row0:row0 + KV_WIDTH, :]
        for b in range(bt):
            new_shift = (keep + w_buf - (lane0 + b * seq_len)) % w_buf
            old = c_ref[b].reshape(KV_WIDTH, w_buf)
            new = jnp.where(lane < keep, pltpu.roll(old, keep, axis=1), pltpu.roll(fresh, new_shift, axis=1))
            n_ref[b] = new.reshape(KV_HEADS, HEAD_DIM, w_buf)

    def both(c_ref, n_ref, b, g):
        return jnp.concatenate([c_ref[b, g], n_ref[b, g]], axis=1).astype(BF16)

    for u, (b, g) in enumerate(units):
        rs = slice(b * seq_len, (b + 1) * seq_len)
        qs = jnp.concatenate([q_ref[rs, (g * REP + r) * HEAD_DIM:(g * REP + r + 1) * HEAD_DIM] for r in range(REP)],
                             axis=0)
        s_ref[u * unit_rows:(u + 1) * unit_rows, :] = jnp.dot(
            (qs * ATTN_SCALE).astype(BF16), both(ck_ref, nk_ref, b, g), preferred_element_type=F32)

    col = lax.broadcasted_iota(jnp.int32, (seq_rows, 2 * w_buf), 1)
    not_sink = (col != seq_len).astype(F32)
    for b in range(bt):
        rows_b = slice(b * seq_rows, (b + 1) * seq_rows)
        s = s_ref[rows_b, :] * not_sink + bias_ref[...]
        m = jnp.max(s, axis=-1, keepdims=True)
        p = jnp.exp(s - m)
        inv = 1.0 / jnp.sum(p, axis=-1, keepdims=True)
        p_ref[rows_b, :] = (p * inv * not_sink).astype(BF16)

    for u, (b, g) in enumerate(units):
        o = _dot_nt(p_ref[u * unit_rows:(u + 1) * unit_rows, :], both(cv_ref, nv_ref, b, g))
        rs = slice(b * seq_len, (b + 1) * seq_len)
        for r in range(REP):
            hs = slice((g * REP + r) * HEAD_DIM, (g * REP + r + 1) * HEAD_DIM)
            o_ref[rs, hs] = (o[r * seq_len:(r + 1) * seq_len, :] * ga_ref[rs, hs]).astype(BF16)


def _attn_sample(u, kv_t, cache_k, cache_v, bias, row0, batch, seq):
    bt = SAMPLE_BT
    rows = bt * seq
    w_buf = cache_k.shape[-1]
    assert w_buf == LANES and row0 % rows == 0 and LANES % rows == 0
    blk0 = row0 // rows
    score_rows = bt * Q_HEADS * seq
    cache_spec = pl.BlockSpec((bt, KV_HEADS, HEAD_DIM, w_buf), lambda i: (i, 0, 0, 0))
    return pl.pallas_call(
        functools.partial(_attn_sample_kernel, seq_len=seq),
        grid=(batch // bt,),
        in_specs=[pl.BlockSpec((rows, ATTN_WIDTH), lambda i: (blk0 + i, _COL.Q // ATTN_WIDTH)),
                  pl.BlockSpec((rows, ATTN_WIDTH), lambda i: (blk0 + i, _COL.GA // ATTN_WIDTH)),
                  pl.BlockSpec((2 * KV_WIDTH, LANES), lambda i: (0, i * rows // LANES)),
                  cache_spec, cache_spec,
                  pl.BlockSpec(bias.shape, lambda i: (0, 0))],
        out_specs=[pl.BlockSpec((rows, ATTN_WIDTH), lambda i: (i, 0)), cache_spec, cache_spec],
        out_shape=[jax.ShapeDtypeStruct((batch * seq, ATTN_WIDTH), BF16),
                   jax.ShapeDtypeStruct(cache_k.shape, F32),
                   jax.ShapeDtypeStruct(cache_v.shape, F32)],
        scratch_shapes=[pltpu.VMEM((score_rows, 2 * w_buf), F32),
                        pltpu.VMEM((score_rows, 2 * w_buf), BF16)],
        compiler_params=_cparams(1),
        name="attn_sample",
    )(u, u, kv_t, cache_k, cache_v, bias)


MERGE_TN = 512
EPI_SPLIT = 4


def _merge_kernel(g_ref, og_ref, ms_ref, ma_ref, ws_ref, wa_ref, o_ref, *stage_refs):
    rows = o_ref.shape[0] // EPI_SPLIT
    ssm_refs, attn_refs = stage_refs[:EPI_SPLIT], stage_refs[EPI_SPLIT:]

    def gate_chunk(n):
        rs = slice(n * rows, (n + 1) * rows)
        o_ref[rs, :] = (_sigmoid(ms_ref[rs, :]) * ssm_refs[n][...]
                        + _sigmoid(ma_ref[rs, :]) * attn_refs[n][...]).astype(BF16)

    for n in range(EPI_SPLIT):
        rs = slice(n * rows, (n + 1) * rows)
        ssm_refs[n][...] = jnp.dot(g_ref[rs, :], ws_ref[...], preferred_element_type=F32)
        attn_refs[n][...] = jnp.dot(og_ref[rs, :], wa_ref[...], preferred_element_type=F32)
        if n > 0:
            gate_chunk(n - 1)
    gate_chunk(EPI_SPLIT - 1)


def _merge(g, og, u, w_ssm, w_attn, row0, tm):
    t = g.shape[0]
    tn = MERGE_TN
    assert row0 % tm == 0
    blk0 = row0 // tm
    return pl.pallas_call(
        _merge_kernel,
        grid=(t // tm, D_MODEL // tn),
        in_specs=[pl.BlockSpec((tm, D_INNER), lambda i, j: (i, 0)),
                  pl.BlockSpec((tm, ATTN_WIDTH), lambda i, j: (i, 0)),
                  pl.BlockSpec((tm, tn), lambda i, j: (blk0 + i, _COL.MS // tn + j)),
                  pl.BlockSpec((tm, tn), lambda i, j: (blk0 + i, _COL.MA // tn + j)),
                  pl.BlockSpec((D_INNER, tn), lambda i, j: (0, j)),
                  pl.BlockSpec((ATTN_WIDTH, tn), lambda i, j: (0, j))],
        out_specs=pl.BlockSpec((tm, tn), lambda i, j: (i, j)),
        out_shape=jax.ShapeDtypeStruct((t, D_MODEL), BF16),
        scratch_shapes=[pltpu.VMEM((tm // EPI_SPLIT, tn), F32) for _ in range(2 * EPI_SPLIT)],
        compiler_params=_cparams(2, BIG_VMEM_LIMIT),
        name="merge",
    )(g, og, u, u, w_ssm, w_attn)


def _out_proj_kernel(m_ref, x_ref, wo_ref, fw_ref, o_ref):
    xn = x_ref[...] + jnp.dot(m_ref[...], wo_ref[...], preferred_element_type=F32)
    ms = jnp.mean(xn * xn, axis=-1, keepdims=True)
    o_ref[...] = xn * lax.rsqrt(ms + RMS_EPS) * fw_ref[...]


def _out_proj(m, x2d, w_out, final_w, tm):
    t = m.shape[0]
    return pl.pallas_call(
        _out_proj_kernel,
        grid=(t // tm,),
        in_specs=[pl.BlockSpec((tm, D_MODEL), lambda i: (i, 0)),
                  pl.BlockSpec((tm, D_MODEL), lambda i: (i, 0)),
                  pl.BlockSpec((D_MODEL, D_MODEL), lambda i: (0, 0)),
                  pl.BlockSpec((1, D_MODEL), lambda i: (0, 0))],
        out_specs=pl.BlockSpec((tm, D_MODEL), lambda i: (i, 0)),
        out_shape=jax.ShapeDtypeStruct((t, D_MODEL), F32),
        compiler_params=_cparams(1),
        name="out_proj",
    )(m, x2d, w_out, final_w)


def _pad_lanes(v):
    return jnp.pad(v.reshape(1, -1), ((0, 0), (0, DT_PAD - v.shape[-1])))


def kernel(x_prompt, x_sample, cache_k, cache_v, state_conv, state_ssm, norm_w, w_in, conv_w, conv_b, dt_bias, a_log,
           d_skip, ssm_norm_w, w_ssm_branch, attn_sinks, w_attn_branch, w_out, rel_bias, final_norm_w):
    assert w_in.shape[0] == 1, "single-layer kernel"
    batch, seq, _ = x_prompt.shape
    dec_batch, dec_seq, _ = x_sample.shape
    w_buf = cache_k.shape[2]
    t_p, t_s = batch * seq, dec_batch * dec_seq
    assert seq % SSD_CHUNK == 0 and seq % WINDOW == 0 and dec_seq == SUBLANES and w_buf == WINDOW
    assert dec_batch % SAMPLE_BT == 0 and (t_p + t_s) % IN_TM == 0

    w_t = jnp.transpose(w_in[0])
    sc_t = jnp.transpose(state_conv[0], (1, 0, 2))
    ck_t = jnp.transpose(cache_k[0], (0, 2, 3, 1))
    cv_t = jnp.transpose(cache_v[0], (0, 2, 3, 1))

    nw = norm_w[0].reshape(1, D_MODEL)
    cw, cb = conv_w[0], conv_b[0].reshape(1, CONV_DIM)
    dtb, alog = _pad_lanes(dt_bias[0]), _pad_lanes(a_log[0])
    dskip_cols = jnp.repeat(d_skip[0], SSM_HEAD_DIM).reshape(1, D_INNER)
    snw = ssm_norm_w[0].reshape(1, D_INNER)
    w_ssm = w_ssm_branch[0].astype(BF16)
    w_attn = w_attn_branch[0].astype(BF16)
    wo = w_out[0].astype(BF16)
    fw = final_norm_w.reshape(1, D_MODEL)
    sinks = attn_sinks[0]

    xp2 = x_prompt.reshape(t_p, D_MODEL)
    xs2 = x_sample.reshape(t_s, D_MODEL)

    band_bias_t = _band_bias_t(rel_bias)
    dec_bias = _decode_bias(rel_bias, sinks, dec_seq, w_buf)

    h, dt_raw = _norm_dt(xp2, xs2, nw, w_t)
    u = _in_proj(h, w_t)
    kv_t = _kv_t(h, w_t, t_p, t_s)

    g_p, st_p = _ssd_prompt(u, dt_raw, cw, cb, dtb, alog, dskip_cols, snw, batch, seq)
    og_p = _attn_prompt(u, sinks, band_bias_t, batch, seq)
    m_p = _merge(g_p, og_p, u, w_ssm, w_attn, 0, tm=1024)
    y_p = _out_proj(m_p, xp2, wo, fw, tm=512)

    xc_s, nsc_t = _conv_sample(u, sc_t, cw, cb, t_p, dec_batch, dec_seq)
    g_s, st_s = _ssd_sample(u, dt_raw, xc_s, state_ssm[0].reshape(dec_batch, D_INNER, SSM_STATE),
                            dtb, alog, dskip_cols, snw, t_p, dec_batch, dec_seq)
    og_s, nk_t, nv_t = _attn_sample(u, kv_t, ck_t, cv_t, dec_bias, t_p, dec_batch, dec_seq)
    m_s = _merge(g_s, og_s, u, w_ssm, w_attn, t_p, tm=1024)
    y_s = _out_proj(m_s, xs2, wo, fw, tm=512)

    def seq_tail(n_rows, col0, width):
        return jnp.stack([u[(b + 1) * seq - n_rows:(b + 1) * seq, col0:col0 + width] for b in range(batch)])

    k_p = seq_tail(WINDOW, _COL.K, KV_WIDTH).reshape(1, batch, WINDOW, KV_HEADS, HEAD_DIM)
    v_p = seq_tail(WINDOW, _COL.V, KV_WIDTH).reshape(1, batch, WINDOW, KV_HEADS, HEAD_DIM)
    conv_p = seq_tail(CONV_HALO, _COL.XS, CONV_DIM)[None]
    conv_s = jnp.transpose(nsc_t, (1, 0, 2))[None]
    ssm_p = st_p.reshape(1, batch, SSM_HEADS, SSM_HEAD_DIM, SSM_STATE)
    ssm_s = st_s.reshape(1, dec_batch, SSM_HEADS, SSM_HEAD_DIM, SSM_STATE)
    k_s = jnp.transpose(nk_t, (0, 3, 1, 2))[None]
    v_s = jnp.transpose(nv_t, (0, 3, 1, 2))[None]
    return (y_p.reshape(batch, seq, D_MODEL), y_s.reshape(dec_batch, dec_seq, D_MODEL),
            k_p, v_p, conv_p, ssm_p, k_s, v_s, conv_s, ssm_s)
```

```python
import functools
import math

import numpy as np
import jax
import jax.numpy as jnp
from jax import lax
from jax.experimental import pallas as pl
from jax.experimental.pallas import tpu as pltpu

F32 = jnp.float32
BF16 = jnp.bfloat16
HIGHEST = lax.Precision.HIGHEST

D_MODEL = 2048
D_INNER = 4096
SSM_HEAD_DIM = 64
SSM_HEADS = 64
SSM_GROUPS = 8
SSM_STATE = 128
CONV_WIDTH = 4
CONV_HALO = CONV_WIDTH - 1
GN = SSM_GROUPS * SSM_STATE
CONV_DIM = D_INNER + 2 * GN
SSD_CHUNK = 128
GROUP_W = D_INNER // SSM_GROUPS
HEAD_DIM = 64
Q_HEADS = 32
KV_HEADS = 8
REP = Q_HEADS // KV_HEADS
ATTN_WIDTH = Q_HEADS * HEAD_DIM
KV_WIDTH = KV_HEADS * HEAD_DIM
WINDOW = 128
N_BUCKETS = 32
MAX_EXACT = N_BUCKETS // 2
RMS_EPS = 1e-6
ATTN_SCALE = HEAD_DIM ** -0.5

_OFF_Z = 0
_OFF_XBC = _OFF_Z + D_INNER
_OFF_DT = _OFF_XBC + CONV_DIM
_OFF_Q = _OFF_DT + SSM_HEADS
_OFF_K = _OFF_Q + ATTN_WIDTH
_OFF_V = _OFF_K + KV_WIDTH
_OFF_GA = _OFF_V + KV_WIDTH
_OFF_MS = _OFF_GA + ATTN_WIDTH
_OFF_MA = _OFF_MS + D_MODEL
_IN_COLS = _OFF_MA + D_MODEL

LANES = 128
SUBLANES = 8
DT_PAD = LANES


class _COL:
    Z = 0
    XS = Z + D_INNER
    BC = XS + D_INNER
    Q = BC + 2 * GN
    GA = Q + ATTN_WIDTH
    MS = GA + ATTN_WIDTH
    MA = MS + D_MODEL
    K = MA + D_MODEL
    V = K + KV_WIDTH
    TOTAL = V + KV_WIDTH


VMEM_LIMIT = 52 * 1024 * 1024
BIG_VMEM_LIMIT = 57 * 1024 * 1024


def _cparams(n_grid, limit=VMEM_LIMIT):
    return pltpu.CompilerParams(dimension_semantics=("arbitrary",) * n_grid, vmem_limit_bytes=limit)


def _silu(v):
    return v * (1.0 / (1.0 + jnp.exp(-v)))


def _sigmoid(v):
    return 1.0 / (1.0 + jnp.exp(-v))


def _dot_nt(a, b):
    return lax.dot_general(a, b, (((1,), (1,)), ((), ())), preferred_element_type=F32)


NORM_TM = 512


def _norm_dt_kernel(xp_ref, xs_ref, nw_ref, wdt_ref, h_ref, dt_ref, n_prompt_tiles):
    i = pl.program_id(0)

    def emit(x_ref):
        xf = x_ref[...]
        ms = jnp.mean(xf * xf, axis=-1, keepdims=True)
        h = (xf * lax.rsqrt(ms + RMS_EPS) * nw_ref[...]).astype(BF16)
        h_ref[...] = h
        dt_ref[...] = _dot_nt(h, wdt_ref[...].astype(BF16))

    @pl.when(i < n_prompt_tiles)
    def _():
        emit(xp_ref)

    @pl.when(i >= n_prompt_tiles)
    def _():
        emit(xs_ref)


def _norm_dt(xp2, xs2, norm_w, w_t):
    tp, ts = xp2.shape[0], xs2.shape[0]
    n_p, n_s = tp // NORM_TM, ts // NORM_TM
    return pl.pallas_call(
        functools.partial(_norm_dt_kernel, n_prompt_tiles=n_p),
        grid=(n_p + n_s,),
        in_specs=[pl.BlockSpec((NORM_TM, D_MODEL), lambda i: (jnp.minimum(i, n_p - 1), 0)),
                  pl.BlockSpec((NORM_TM, D_MODEL), lambda i: (jnp.maximum(i - n_p, 0), 0)),
                  pl.BlockSpec((1, D_MODEL), lambda i: (0, 0)),
                  pl.BlockSpec((DT_PAD, D_MODEL), lambda i: (_OFF_DT // DT_PAD, 0))],
        out_specs=[pl.BlockSpec((NORM_TM, D_MODEL), lambda i: (i, 0)),
                   pl.BlockSpec((NORM_TM, DT_PAD), lambda i: (i, 0))],
        out_shape=[jax.ShapeDtypeStruct((tp + ts, D_MODEL), BF16),
                   jax.ShapeDtypeStruct((tp + ts, DT_PAD), F32)],
        compiler_params=_cparams(1),
        name="norm_dt",
    )(xp2, xs2, norm_w, w_t)


IN_TM = 1536
IN_TN = 1024
IN_CHUNK = 512
assert _COL.TOTAL % IN_TN == 0 and IN_TN == 2 * IN_CHUNK and _OFF_DT % IN_CHUNK == 0


def _src_row(c):
    straight = _OFF_DT // IN_CHUNK
    jj = c - straight
    q_t, kv_t = ATTN_WIDTH // IN_CHUNK, 2 * KV_WIDTH // IN_CHUNK
    gate_t = (ATTN_WIDTH + 2 * D_MODEL) // IN_CHUNK
    shifted = jnp.where(jj < q_t, jj, jnp.where(jj < q_t + gate_t, jj + kv_t, jj - gate_t))
    return pl.multiple_of(jnp.where(c < straight, c * IN_CHUNK, _OFF_Q + shifted * IN_CHUNK), SSM_HEADS)


IN_NSPLIT = 4
IN_MW = IN_TM // IN_NSPLIT
J_Z1 = _COL.XS // IN_TN
J_Q0, J_Q1 = _COL.Q // IN_TN, _COL.GA // IN_TN
J_GA0, J_GA1 = _COL.GA // IN_TN, _COL.MS // IN_TN
assert _COL.XS % IN_TN == 0 and _COL.Q % IN_TN == 0 and _COL.GA % IN_TN == 0 and _COL.MS % IN_TN == 0
assert math.log2(HEAD_DIM) % 2 == 0, "the folded attention scale must be a power of two"


def _in_proj_kernel(h_ref, w0_ref, w1_ref, o_ref, wbf_ref, *stage_refs):
    j = pl.program_id(0)

    @pl.when(pl.program_id(1) == 0)
    def _():
        scale = jnp.where((j >= J_Q0) & (j < J_Q1), ATTN_SCALE, 1.0)
        wbf_ref[0:IN_CHUNK, :] = (w0_ref[...] * scale).astype(BF16)
        wbf_ref[IN_CHUNK:IN_TN, :] = (w1_ref[...] * scale).astype(BF16)

    is_silu_tile = (j < J_Z1) | ((j >= J_GA0) & (j < J_GA1))

    @pl.when(is_silu_tile)
    def _():
        def silu_chunk(n):
            o_ref[n * IN_MW:(n + 1) * IN_MW, :] = _silu(stage_refs[n][...])

        for n in range(IN_NSPLIT):
            stage_refs[n][...] = _dot_nt(h_ref[n * IN_MW:(n + 1) * IN_MW, :], wbf_ref[...])
            if n > 0:
                silu_chunk(n - 1)
        silu_chunk(IN_NSPLIT - 1)

    @pl.when(jnp.logical_not(is_silu_tile))
    def _():
        o_ref[...] = _dot_nt(h_ref[...], wbf_ref[...])


def _in_proj(h, w_t):
    t = h.shape[0]
    return pl.pallas_call(
        _in_proj_kernel,
        grid=(_COL.TOTAL // IN_TN, t // IN_TM),
        in_specs=[pl.BlockSpec((IN_TM, D_MODEL), lambda j, i: (i, 0)),
                  pl.BlockSpec((pl.Element(IN_CHUNK), pl.Element(D_MODEL)), lambda j, i: (_src_row(2 * j), 0)),
                  pl.BlockSpec((pl.Element(IN_CHUNK), pl.Element(D_MODEL)), lambda j, i: (_src_row(2 * j + 1), 0))],
        out_specs=pl.BlockSpec((IN_TM, IN_TN), lambda j, i: (i, j)),
        out_shape=jax.ShapeDtypeStruct((t, _COL.TOTAL), F32),
        scratch_shapes=[pltpu.VMEM((IN_TN, D_MODEL), BF16)]
        + [pltpu.VMEM((IN_MW, IN_TN), F32) for _ in range(IN_NSPLIT)],
        compiler_params=_cparams(2, BIG_VMEM_LIMIT),
        name="in_proj",
    )(h, w_t, w_t)


def _kv_t_kernel(w_ref, h_ref, o_ref):
    o_ref[...] = _dot_nt(w_ref[...].astype(BF16), h_ref[...])


def _kv_t(h, w_t, row0, rows):
    assert row0 % rows == 0
    return pl.pallas_call(
        _kv_t_kernel,
        grid=(1,),
        in_specs=[pl.BlockSpec((pl.Element(2 * KV_WIDTH), pl.Element(D_MODEL)), lambda i: (_OFF_K, 0)),
                  pl.BlockSpec((rows, D_MODEL), lambda i: (row0 // rows, 0))],
        out_specs=pl.BlockSpec((2 * KV_WIDTH, rows), lambda i: (0, 0)),
        out_shape=jax.ShapeDtypeStruct((2 * KV_WIDTH, rows), F32),
        compiler_params=_cparams(1),
        name="kv_t",
    )(w_t, h)


def _bucket_of_dist(dist):
    n = np.maximum(dist, 0)
    nf = np.maximum(n, 1).astype(np.float32)
    large = MAX_EXACT + (np.log(nf / MAX_EXACT) / math.log(WINDOW / MAX_EXACT)
                         * (N_BUCKETS - MAX_EXACT)).astype(np.int32)
    large = np.minimum(large, N_BUCKETS - 1)
    bucket = np.where(n < MAX_EXACT, n, large)
    return np.where((dist >= 0) & (dist <= WINDOW), bucket, -1).astype(np.int32)


def _bias_of_bucket(bk, table_ref, h):
    acc = jnp.full(bk.shape, -jnp.inf, F32)
    for b in range(N_BUCKETS):
        acc = jnp.where(bk == b, table_ref[b, h], acc)
    return acc


BAND_BIAS_HEADS = 8


def _band_bias_kernel(table_ref, bucket_ref, o_ref):
    for hh in range(BAND_BIAS_HEADS):
        o_ref[hh] = _bias_of_bucket(bucket_ref[...], table_ref, pl.program_id(0) * BAND_BIAS_HEADS + hh)


def _band_bias_t(rel_bias):
    kj = np.arange(2 * WINDOW)[:, None]
    qi = np.arange(WINDOW)[None, :]
    bucket = jnp.asarray(_bucket_of_dist(qi + WINDOW - kj))
    return pl.pallas_call(
        _band_bias_kernel,
        grid=(Q_HEADS // BAND_BIAS_HEADS,),
        in_specs=[pl.BlockSpec(memory_space=pltpu.SMEM),
                  pl.BlockSpec((2 * WINDOW, WINDOW), lambda h: (0, 0))],
        out_specs=pl.BlockSpec((BAND_BIAS_HEADS, 2 * WINDOW, WINDOW), lambda h: (h, 0, 0)),
        out_shape=jax.ShapeDtypeStruct((Q_HEADS, 2 * WINDOW, WINDOW), F32),
        compiler_params=_cparams(1),
        name="band_bias",
    )(rel_bias, bucket)


def _decode_bias_kernel(table_ref, sink_ref, bucket_ref, o_ref, seq):
    g = pl.program_id(0)
    sink_lane = lax.broadcasted_iota(jnp.int32, bucket_ref.shape, 1) == seq
    for r in range(REP):
        h = g * REP + r
        bias = _bias_of_bucket(bucket_ref[...], table_ref, h)
        o_ref[r * seq:(r + 1) * seq, :] = jnp.where(sink_lane, sink_ref[h], bias)


def _decode_bias(rel_bias, sinks, seq, w_buf):
    l = np.arange(seq)[:, None]
    j = np.arange(w_buf)[None, :]
    dist_old = np.where((j < seq), l + w_buf - j, -1)
    dist_new = np.where(j < w_buf - seq, l + w_buf - (j + seq), l - (j - (w_buf - seq)))
    bucket = jnp.asarray(np.concatenate([_bucket_of_dist(dist_old), _bucket_of_dist(dist_new)], axis=1))
    return pl.pallas_call(
        functools.partial(_decode_bias_kernel, seq=seq),
        grid=(KV_HEADS,),
        in_specs=[pl.BlockSpec(memory_space=pltpu.SMEM), pl.BlockSpec(memory_space=pltpu.SMEM),
                  pl.BlockSpec(bucket.shape, lambda g: (0, 0))],
        out_specs=pl.BlockSpec((REP * seq, 2 * w_buf), lambda g: (g, 0)),
        out_shape=jax.ShapeDtypeStruct((Q_HEADS * seq, 2 * w_buf), F32),
        compiler_params=_cparams(1),
        name="decode_bias",
    )(rel_bias, sinks, bucket)


CONV_STRIP = 512


def _conv_silu(ext_ref, base, rows, cw_ref, cb_ref, xc_ref, out_row):
    for c0 in range(0, CONV_DIM, CONV_STRIP):
        cs = slice(c0, c0 + CONV_STRIP)
        acc = cb_ref[:, cs] + cw_ref[3:4, cs] * ext_ref[base:base + rows, cs]
        for k in range(1, CONV_WIDTH):
            acc = acc + cw_ref[3 - k:4 - k, cs] * ext_ref[base - k:base - k + rows, cs]
        xc_ref[out_row:out_row + rows, cs] = _silu(acc)


CONV_S_SEQS = 16


def _conv_sample_kernel(xs_ref, bc_ref, sc_ref, cw_ref, cb_ref, xc_ref, nsc_ref, ext_ref, seq_len):
    slab = SUBLANES + seq_len
    for b in range(CONV_S_SEQS):
        base = b * slab + SUBLANES
        for k in range(CONV_HALO):
            ext_ref[base - CONV_HALO + k:base - CONV_HALO + k + 1, :] = sc_ref[k, b:b + 1, :]
        ext_ref[base:base + seq_len, 0:D_INNER] = xs_ref[b * seq_len:(b + 1) * seq_len, :]
        ext_ref[base:base + seq_len, D_INNER:CONV_DIM] = bc_ref[b * seq_len:(b + 1) * seq_len, :]
        for k in range(CONV_HALO):
            row = base + seq_len - CONV_HALO + k
            nsc_ref[k, b:b + 1, :] = ext_ref[row:row + 1, :]
    for c0 in range(0, CONV_DIM, CONV_STRIP):
        cs = slice(c0, c0 + CONV_STRIP)
        bias = jnp.broadcast_to(cb_ref[:, cs], (seq_len, CONV_STRIP))
        taps = [jnp.broadcast_to(cw_ref[CONV_HALO - k:CONV_WIDTH - k, cs], (seq_len, CONV_STRIP))
                for k in range(CONV_WIDTH)]
        for b in range(CONV_S_SEQS):
            base = b * slab + SUBLANES
            acc = bias
            for k in range(CONV_WIDTH):
                acc = acc + taps[k] * ext_ref[base - k:base - k + seq_len, cs]
            xc_ref[b * seq_len:(b + 1) * seq_len, cs] = _silu(acc)


def _conv_sample(u, sc_t, conv_w, conv_b, row0, batch, seq):
    assert seq == SUBLANES and batch % CONV_S_SEQS == 0
    rows = CONV_S_SEQS * seq
    assert row0 % rows == 0
    blk0 = row0 // rows
    full = lambda shape: pl.BlockSpec(shape, lambda i: (0,) * len(shape))
    return pl.pallas_call(
        functools.partial(_conv_sample_kernel, seq_len=seq),
        grid=(batch // CONV_S_SEQS,),
        in_specs=[pl.BlockSpec((rows, D_INNER), lambda i: (blk0 + i, _COL.XS // D_INNER)),
                  pl.BlockSpec((rows, 2 * GN), lambda i: (blk0 + i, _COL.BC // (2 * GN))),
                  pl.BlockSpec((CONV_HALO, CONV_S_SEQS, CONV_DIM), lambda i: (0, i, 0)),
                  full((CONV_WIDTH, CONV_DIM)), full((1, CONV_DIM))],
        out_specs=[pl.BlockSpec((rows, CONV_DIM), lambda i: (i, 0)),
                   pl.BlockSpec((CONV_HALO, CONV_S_SEQS, CONV_DIM), lambda i: (0, i, 0))],
        out_shape=[jax.ShapeDtypeStruct((batch * seq, CONV_DIM), F32),
                   jax.ShapeDtypeStruct((CONV_HALO, batch, CONV_DIM), F32)],
        scratch_shapes=[pltpu.VMEM((CONV_S_SEQS * (SUBLANES + seq), CONV_DIM), F32)],
        compiler_params=_cparams(1),
        name="conv_sample",
    )(u, u, sc_t, conv_w, conv_b)


def _ssd_core(rows, same_seq, zs_ref, dt_ref, dtb_ref, alog_ref, dskip_ref, nw_ref, xs_ref, bc_ref, xd_ref, y_ref,
              yoff_fn, state_fn, g_ref):
    li = lax.broadcasted_iota(jnp.int32, (rows, rows), 0)
    si = lax.broadcasted_iota(jnp.int32, (rows, rows), 1)
    causal = same_seq & (si <= li)
    lane = lax.broadcasted_iota(jnp.int32, (rows, LANES), 1)
    first_half = lane < SSM_HEAD_DIM

    dt = jax.nn.softplus(dt_ref[...] + dtb_ref[...])
    a_neg = -jnp.exp(alog_ref[...])
    d_a = dt * a_neg
    acs = jnp.dot(causal.astype(F32), d_a, precision=HIGHEST, preferred_element_type=F32)
    a_end = jnp.dot(same_seq.astype(F32), d_a, precision=HIGHEST, preferred_element_type=F32)
    log_dt = jnp.log(dt)
    seg_rows = (acs - log_dt).T
    log_w_state = log_dt + a_end - acs
    cd_rows = jnp.exp(a_end)
    e_acs_tab, w_state_tab = jnp.exp(acs), jnp.exp(log_w_state)

    def head_cols(m, j):
        return (jnp.broadcast_to(m[:, 2 * j:2 * j + 1], (rows, LANES)),
                jnp.broadcast_to(m[:, 2 * j + 1:2 * j + 2], (rows, LANES)))

    pairs_per_group = SSM_HEADS // SSM_GROUPS // 2
    for g in range(SSM_GROUPS):
        b_g = bc_ref[:, g * SSM_STATE:(g + 1) * SSM_STATE].astype(BF16)
        c_g = bc_ref[:, GN + g * SSM_STATE:GN + (g + 1) * SSM_STATE].astype(BF16)
        cb = _dot_nt(c_g, b_g)
        yoff = yoff_fn(g, c_g)
        for jj in range(pairs_per_group):
            j = g * pairs_per_group + jj
            ps = slice(j * LANES, (j + 1) * LANES)
            xs_pair = xs_ref[:, ps]
            xs_bf16 = xs_pair.astype(BF16)
            acs_cols = head_cols(acs, j)
            parts = []
            for hh in range(2):
                h = 2 * j + hh
                col = acs_cols[hh] if rows == LANES else acs[:, h:h + 1]
                seg = col - seg_rows[h:h + 1, :]
                decay_dt = jnp.exp(jnp.where(causal, seg, -jnp.inf))
                m = (cb * decay_dt).astype(BF16)
                parts.append(jnp.dot(m, xs_bf16, preferred_element_type=F32))
            yd = jnp.where(first_half, parts[0], parts[1])
            if rows == LANES:
                e_acs = jnp.exp(jnp.where(first_half, *acs_cols))
                w_state = jnp.exp(jnp.where(first_half, *head_cols(log_w_state, j)))
            else:
                e_acs = jnp.where(first_half, *head_cols(e_acs_tab, j))
                w_state = jnp.where(first_half, *head_cols(w_state_tab, j))
            xd_ref[:, ps] = (xs_pair * w_state).astype(xd_ref.dtype)
            y_ref[:, ps] = yd + yoff[:, jj * LANES:(jj + 1) * LANES] * e_acs + dskip_ref[:, ps] * xs_pair
        state_fn(g, cd_rows)

    for g in range(SSM_GROUPS):
        gs = slice(g * GROUP_W, (g + 1) * GROUP_W)
        gg = y_ref[:, gs] * zs_ref[:, gs]
        ms = jnp.mean(gg * gg, axis=-1, keepdims=True)
        g_ref[:, gs] = (gg * lax.rsqrt(ms + RMS_EPS) * nw_ref[:, gs]).astype(BF16)


def _ssd_prompt_kernel(zs_ref, xs_raw_ref, bc_raw_ref, dt_ref, cw_ref, cb_ref, dtb_ref, alog_ref, dskip_ref, nw_ref,
                       g_ref, st_ref, ext_ref, xc_ref, xd_ref, y_ref, stt_ref):
    c = pl.program_id(1)
    rows = SSD_CHUNK

    @pl.when(c == 0)
    def _():
        ext_ref[0:SUBLANES, :] = jnp.zeros((SUBLANES, CONV_DIM), F32)
        stt_ref[...] = jnp.zeros(stt_ref.shape, F32)

    @pl.when(c != 0)
    def _():
        ext_ref[0:SUBLANES, :] = ext_ref[rows:rows + SUBLANES, :]

    ext_ref[SUBLANES:SUBLANES + rows, 0:D_INNER] = xs_raw_ref[...]
    ext_ref[SUBLANES:SUBLANES + rows, D_INNER:CONV_DIM] = bc_raw_ref[...]
    _conv_silu(ext_ref, SUBLANES, rows, cw_ref, cb_ref, xc_ref, 0)
    xs_ref = xc_ref.at[:, 0:D_INNER]
    bc_ref = xc_ref.at[:, D_INNER:CONV_DIM]

    def yoff_fn(g, c_g):
        return jnp.dot(c_g, stt_ref[:, g * GROUP_W:(g + 1) * GROUP_W].astype(BF16), preferred_element_type=F32)

    def state_fn(g, cd_rows):
        b_g = bc_ref[:, g * SSM_STATE:(g + 1) * SSM_STATE].astype(BF16)
        xd_g = xd_ref[:, g * GROUP_W:(g + 1) * GROUP_W]
        new = lax.dot_general(b_g, xd_g, (((0,), (0,)), ((), ())), preferred_element_type=F32)
        lane = lax.broadcasted_iota(jnp.int32, (SSM_STATE, LANES), 1)
        for jj in range(GROUP_W // LANES):
            j = g * (GROUP_W // LANES) + jj
            ps = slice(j * LANES, (j + 1) * LANES)
            cd = jnp.where(lane < SSM_HEAD_DIM,
                           jnp.broadcast_to(cd_rows[0:1, 2 * j:2 * j + 1], (SSM_STATE, LANES)),
                           jnp.broadcast_to(cd_rows[0:1, 2 * j + 1:2 * j + 2], (SSM_STATE, LANES)))
            stt_ref[:, ps] = stt_ref[:, ps] * cd + new[:, jj * LANES:(jj + 1) * LANES]

    same_seq = jnp.full((rows, rows), True)
    _ssd_core(rows, same_seq, zs_ref, dt_ref, dtb_ref, alog_ref, dskip_ref, nw_ref, xs_ref, bc_ref, xd_ref, y_ref,
              yoff_fn, state_fn, g_ref)

    @pl.when(c == pl.num_programs(1) - 1)
    def _():
        for j in range(D_INNER // LANES):
            st_ref[0, j * LANES:(j + 1) * LANES, :] = stt_ref[:, j * LANES:(j + 1) * LANES].T


def _ssd_prompt(u, dt_raw, conv_w, conv_b, dt_bias, a_log, dskip_cols, ssm_norm_w, batch, seq):
    nc = seq // SSD_CHUNK
    rows = SSD_CHUNK
    row_blk = lambda b, c: b * nc + c
    full = lambda shape: pl.BlockSpec(shape, lambda b, c: (0,) * len(shape))
    return pl.pallas_call(
        _ssd_prompt_kernel,
        grid=(batch, nc),
        in_specs=[pl.BlockSpec((rows, D_INNER), lambda b, c: (row_blk(b, c), _COL.Z // D_INNER)),
                  pl.BlockSpec((rows, D_INNER), lambda b, c: (row_blk(b, c), _COL.XS // D_INNER)),
                  pl.BlockSpec((rows, 2 * GN), lambda b, c: (row_blk(b, c), _COL.BC // (2 * GN))),
                  pl.BlockSpec((rows, DT_PAD), lambda b, c: (row_blk(b, c), 0)),
                  full((CONV_WIDTH, CONV_DIM)), full((1, CONV_DIM)), full((1, DT_PAD)), full((1, DT_PAD)),
                  full((1, D_INNER)), full((1, D_INNER))],
        out_specs=[pl.BlockSpec((rows, D_INNER), lambda b, c: (row_blk(b, c), 0)),
                   pl.BlockSpec((1, D_INNER, SSM_STATE), lambda b, c: (b, 0, 0))],
        out_shape=[jax.ShapeDtypeStruct((batch * seq, D_INNER), BF16),
                   jax.ShapeDtypeStruct((batch, D_INNER, SSM_STATE), F32)],
        scratch_shapes=[pltpu.VMEM((rows + SUBLANES, CONV_DIM), F32),
                        pltpu.VMEM((rows, CONV_DIM), F32),
                        pltpu.VMEM((rows, D_INNER), BF16),
                        pltpu.VMEM((rows, D_INNER), F32),
                        pltpu.VMEM((SSM_STATE, D_INNER), F32)],
        compiler_params=_cparams(2),
        name="ssd_prompt",
    )(u, u, u, dt_raw, conv_w, conv_b, dt_bias, a_log, dskip_cols, ssm_norm_w)


SAMPLE_BT = 4


def _ssd_sample_kernel(zs_ref, xs_ref, bc_ref, dt_ref, st_ref, dtb_ref, alog_ref, dskip_ref, nw_ref,
                       g_ref, nst_ref, xd_ref, y_ref, yoff_ref, seq_len):
    bt = SAMPLE_BT
    rows = bt * seq_len
    heads_per_group = SSM_HEADS // SSM_GROUPS

    for b in range(bt):
        rs = slice(b * seq_len, (b + 1) * seq_len)
        for g in range(SSM_GROUPS):
            c_bg = bc_ref[rs, GN + g * SSM_STATE:GN + (g + 1) * SSM_STATE].astype(BF16)
            s_bg = st_ref[b, g * GROUP_W:(g + 1) * GROUP_W, :].astype(BF16)
            yoff_ref[rs, g * GROUP_W:(g + 1) * GROUP_W] = _dot_nt(c_bg, s_bg)

    def yoff_fn(g, c_g):
        return yoff_ref[:, g * GROUP_W:(g + 1) * GROUP_W]

    def state_fn(g, cd_rows):
        for b in range(bt):
            rs = slice(b * seq_len, (b + 1) * seq_len)
            b_bg = bc_ref[rs, g * SSM_STATE:(g + 1) * SSM_STATE].astype(BF16)
            xd_bg = xd_ref[rs, g * GROUP_W:(g + 1) * GROUP_W].astype(BF16)
            new = lax.dot_general(xd_bg, b_bg, (((0,), (0,)), ((), ())), preferred_element_type=F32)
            for hh in range(heads_per_group):
                h = g * heads_per_group + hh
                hs = slice(h * SSM_HEAD_DIM, (h + 1) * SSM_HEAD_DIM)
                cd = jnp.broadcast_to(cd_rows[b * seq_len:b * seq_len + 1, h:h + 1], (SSM_HEAD_DIM, SSM_STATE))
                nst_ref[b, hs, :] = st_ref[b, hs, :] * cd + new[hh * SSM_HEAD_DIM:(hh + 1) * SSM_HEAD_DIM, :]

    li = lax.broadcasted_iota(jnp.int32, (rows, rows), 0)
    si = lax.broadcasted_iota(jnp.int32, (rows, rows), 1)
    same_seq = (li // seq_len) == (si // seq_len)
    _ssd_core(rows, same_seq, zs_ref, dt_ref, dtb_ref, alog_ref, dskip_ref, nw_ref, xs_ref, bc_ref, xd_ref, y_ref,
              yoff_fn, state_fn, g_ref)


def _ssd_sample(u, dt_raw, xc, state, dt_bias, a_log, dskip_cols, ssm_norm_w, row0, batch, seq):
    bt = SAMPLE_BT
    rows = bt * seq
    assert row0 % rows == 0
    blk0 = row0 // rows
    full = lambda shape: pl.BlockSpec(shape, lambda i: (0,) * len(shape))
    return pl.pallas_call(
        functools.partial(_ssd_sample_kernel, seq_len=seq),
        grid=(batch // bt,),
        in_specs=[pl.BlockSpec((rows, D_INNER), lambda i: (blk0 + i, _COL.Z // D_INNER)),
                  pl.BlockSpec((rows, D_INNER), lambda i: (i, 0)),
                  pl.BlockSpec((rows, 2 * GN), lambda i: (i, D_INNER // (2 * GN))),
                  pl.BlockSpec((rows, DT_PAD), lambda i: (blk0 + i, 0)),
                  pl.BlockSpec((bt, D_INNER, SSM_STATE), lambda i: (i, 0, 0)),
                  full((1, DT_PAD)), full((1, DT_PAD)), full((1, D_INNER)), full((1, D_INNER))],
        out_specs=[pl.BlockSpec((rows, D_INNER), lambda i: (i, 0)),
                   pl.BlockSpec((bt, D_INNER, SSM_STATE), lambda i: (i, 0, 0))],
        out_shape=[jax.ShapeDtypeStruct((batch * seq, D_INNER), BF16),
                   jax.ShapeDtypeStruct((batch, D_INNER, SSM_STATE), F32)],
        scratch_shapes=[pltpu.VMEM((rows, D_INNER), F32),
                        pltpu.VMEM((rows, D_INNER), F32),
                        pltpu.VMEM((rows, D_INNER), F32)],
        compiler_params=_cparams(1),
        name="ssd_sample",
    )(u, xc, xc, dt_raw, state, dt_bias, a_log, dskip_cols, ssm_norm_w)


ONES_ROWS = 16


def _attn_prompt_kernel(sink_ref, q_ref, kp_ref, ko_ref, vp_ref, vo_ref, ga_ref, bias_ref, o_ref):
    n = pl.program_id(1)
    no_prev = jnp.where(n == 0, jnp.inf, 0.0)

    def scores(g):
        ks = slice(g * HEAD_DIM, (g + 1) * HEAD_DIM)
        kk = jnp.concatenate([kp_ref[:, ks], ko_ref[:, ks]], axis=0).astype(BF16)
        vv_t = jnp.concatenate([jnp.concatenate([vp_ref[:, ks], vo_ref[:, ks]], axis=0).T,
                                jnp.ones((ONES_ROWS, 2 * WINDOW), F32)], axis=0).astype(BF16)
        s_list = []
        for r in range(REP):
            h = g * REP + r
            q = q_ref[:, h * HEAD_DIM:(h + 1) * HEAD_DIM].astype(BF16)
            s_list.append(_dot_nt(kk, q))
        return vv_t, s_list

    def finish(g, vv_t, s_list):
        probs = []
        for r in range(REP):
            h = g * REP + r
            s_prev = s_list[r][0:WINDOW] + bias_ref[h, 0:WINDOW, :]
            s_own = s_list[r][WINDOW:2 * WINDOW] + bias_ref[h, WINDOW:2 * WINDOW, :]
            sink = sink_ref[h]
            m = jnp.maximum(jnp.maximum(jnp.max(s_prev, axis=0, keepdims=True) - no_prev,
                                        jnp.max(s_own, axis=0, keepdims=True)), sink)
            p = jnp.concatenate([jnp.exp(s_prev - (m + no_prev)), jnp.exp(s_own - m)], axis=0)
            probs.append((p.astype(BF16), jnp.exp(sink - m)))
        o_t = []
        for p, p_sink in probs:
            pv = jnp.dot(vv_t, p, preferred_element_type=F32)
            o_t.append(pv[0:HEAD_DIM] / (pv[HEAD_DIM:HEAD_DIM + 1] + p_sink))
        for rp in range(REP // 2):
            o_pair = jnp.concatenate(o_t[2 * rp:2 * rp + 2], axis=0).T
            ps = slice((g * REP + 2 * rp) * HEAD_DIM, (g * REP + 2 * rp + 2) * HEAD_DIM)
            o_ref[:, ps] = (o_pair * ga_ref[:, ps]).astype(BF16)

    pending = scores(0)
    for g in range(KV_HEADS):
        cur = pending
        if g + 1 < KV_HEADS:
            pending = scores(g + 1)
        finish(g, *cur)


def _attn_prompt(u, sinks, bias, batch, seq):
    nb = seq // WINDOW
    row = lambda b, n: b * nb + n
    prev = lambda b, n: jnp.maximum(b * nb + n - 1, 0)
    return pl.pallas_call(
        _attn_prompt_kernel,
        grid=(batch, nb),
        in_specs=[pl.BlockSpec(memory_space=pltpu.SMEM),
                  pl.BlockSpec((WINDOW, ATTN_WIDTH), lambda b, n: (row(b, n), _COL.Q // ATTN_WIDTH)),
                  pl.BlockSpec((WINDOW, KV_WIDTH), lambda b, n: (prev(b, n), _COL.K // KV_WIDTH)),
                  pl.BlockSpec((WINDOW, KV_WIDTH), lambda b, n: (row(b, n), _COL.K // KV_WIDTH)),
                  pl.BlockSpec((WINDOW, KV_WIDTH), lambda b, n: (prev(b, n), _COL.V // KV_WIDTH)),
                  pl.BlockSpec((WINDOW, KV_WIDTH), lambda b, n: (row(b, n), _COL.V // KV_WIDTH)),
                  pl.BlockSpec((WINDOW, ATTN_WIDTH), lambda b, n: (row(b, n), _COL.GA // ATTN_WIDTH)),
                  pl.BlockSpec((Q_HEADS, 2 * WINDOW, WINDOW), lambda b, n: (0, 0, 0))],
        out_specs=pl.BlockSpec((WINDOW, ATTN_WIDTH), lambda b, n: (row(b, n), 0)),
        out_shape=jax.ShapeDtypeStruct((batch * seq, ATTN_WIDTH), BF16),
        compiler_params=_cparams(2),
        name="attn_prompt",
    )(sinks, u, u, u, u, u, u, bias)


def _attn_sample_kernel(q_ref, ga_ref, kvt_ref, ck_ref, cv_ref, bias_ref, o_ref, nk_ref, nv_ref,
                        s_ref, p_ref, seq_len):
    bt = SAMPLE_BT
    w_buf = ck_ref.shape[-1]
    keep = w_buf - seq_len
    unit_rows = REP * seq_len
    seq_rows = KV_HEADS * unit_rows
    lane0 = (pl.program_id(0) * bt * seq_len) % kvt_ref.shape[1]
    units = [(b, g) for b in range(bt) for g in range(KV_HEADS)]

    lane = lax.broadcasted_iota(jnp.int32, (KV_WIDTH, w_buf), 1)
    for c_ref, n_ref, row0 in ((ck_ref, nk_ref, 0), (cv_ref, nv_ref, KV_WIDTH)):
        fresh = kvt_ref[row0:row0 + KV_WIDTH, :]
        for b in range(bt):
            new_shift = (keep + w_buf - (lane0 + b * seq_len)) % w_buf
            old = c_ref[b].reshape(KV_WIDTH, w_buf)
            new = jnp.where(lane < keep, pltpu.roll(old, keep, axis=1), pltpu.roll(fresh, new_shift, axis=1))
            n_ref[b] = new.reshape(KV_HEADS, HEAD_DIM, w_buf)

    def both(c_ref, n_ref, b, g):
        return jnp.concatenate([c_ref[b, g], n_ref[b, g]], axis=1).astype(BF16)

    for u, (b, g) in enumerate(units):
        rs = slice(b * seq_len, (b + 1) * seq_len)
        qs = jnp.concatenate([q_ref[rs, (g * REP + r) * HEAD_DIM:(g * REP + r + 1) * HEAD_DIM] for r in range(REP)],
                             axis=0)
        s_ref[u * unit_rows:(u + 1) * unit_rows, :] = jnp.dot(
            qs.astype(BF16), both(ck_ref, nk_ref, b, g), preferred_element_type=F32)

    col = lax.broadcasted_iota(jnp.int32, (seq_rows, 2 * w_buf), 1)
    not_sink = (col != seq_len).astype(F32)
    for b in range(bt):
        rows_b = slice(b * seq_rows, (b + 1) * seq_rows)
        s = s_ref[rows_b, :] * not_sink + bias_ref[...]
        m = jnp.max(s, axis=-1, keepdims=True)
        p = jnp.exp(s - m)
        inv = 1.0 / jnp.sum(p, axis=-1, keepdims=True)
        p_ref[rows_b, :] = (p * inv * not_sink).astype(BF16)

    for u, (b, g) in enumerate(units):
        o = _dot_nt(p_ref[u * unit_rows:(u + 1) * unit_rows, :], both(cv_ref, nv_ref, b, g))
        rs = slice(b * seq_len, (b + 1) * seq_len)
        for r in range(REP):
            hs = slice((g * REP + r) * HEAD_DIM, (g * REP + r + 1) * HEAD_DIM)
            o_ref[rs, hs] = (o[r * seq_len:(r + 1) * seq_len, :] * ga_ref[rs, hs]).astype(BF16)


def _attn_sample(u, kv_t, cache_k, cache_v, bias, row0, batch, seq):
    bt = SAMPLE_BT
    rows = bt * seq
    w_buf = cache_k.shape[-1]
    assert w_buf == LANES and row0 % rows == 0 and LANES % rows == 0
    blk0 = row0 // rows
    score_rows = bt * Q_HEADS * seq
    cache_spec = pl.BlockSpec((bt, KV_HEADS, HEAD_DIM, w_buf), lambda i: (i, 0, 0, 0))
    return pl.pallas_call(
        functools.partial(_attn_sample_kernel, seq_len=seq),
        grid=(batch // bt,),
        in_specs=[pl.BlockSpec((rows, ATTN_WIDTH), lambda i: (blk0 + i, _COL.Q // ATTN_WIDTH)),
                  pl.BlockSpec((rows, ATTN_WIDTH), lambda i: (blk0 + i, _COL.GA // ATTN_WIDTH)),
                  pl.BlockSpec((2 * KV_WIDTH, LANES), lambda i: (0, i * rows // LANES)),
                  cache_spec, cache_spec,
                  pl.BlockSpec(bias.shape, lambda i: (0, 0))],
        out_specs=[pl.BlockSpec((rows, ATTN_WIDTH), lambda i: (i, 0)), cache_spec, cache_spec],
        out_shape=[jax.ShapeDtypeStruct((batch * seq, ATTN_WIDTH), BF16),
                   jax.ShapeDtypeStruct(cache_k.shape, F32),
                   jax.ShapeDtypeStruct(cache_v.shape, F32)],
        scratch_shapes=[pltpu.VMEM((score_rows, 2 * w_buf), F32),
                        pltpu.VMEM((score_rows, 2 * w_buf), BF16)],
        compiler_params=_cparams(1),
        name="attn_sample",
    )(u, u, kv_t, cache_k, cache_v, bias)


MERGE_TN = 512
EPI_SPLIT = 4


def _merge_kernel(g_ref, og_ref, ms_ref, ma_ref, ws_ref, wa_ref, o_ref, *stage_refs):
    rows = o_ref.shape[0] // EPI_SPLIT
    ssm_refs, attn_refs = stage_refs[:EPI_SPLIT], stage_refs[EPI_SPLIT:]

    def gate_chunk(n):
        rs = slice(n * rows, (n + 1) * rows)
        o_ref[rs, :] = (_sigmoid(ms_ref[rs, :]) * ssm_refs[n][...]
                        + _sigmoid(ma_ref[rs, :]) * attn_refs[n][...]).astype(BF16)

    for n in range(EPI_SPLIT):
        rs = slice(n * rows, (n + 1) * rows)
        ssm_refs[n][...] = jnp.dot(g_ref[rs, :], ws_ref[...], preferred_element_type=F32)
        attn_refs[n][...] = jnp.dot(og_ref[rs, :], wa_ref[...], preferred_element_type=F32)
        if n > 0:
            gate_chunk(n - 1)
    gate_chunk(EPI_SPLIT - 1)


def _merge(g, og, u, w_ssm, w_attn, row0, tm):
    t = g.shape[0]
    tn = MERGE_TN
    assert row0 % tm == 0
    blk0 = row0 // tm
    return pl.pallas_call(
        _merge_kernel,
        grid=(t // tm, D_MODEL // tn),
        in_specs=[pl.BlockSpec((tm, D_INNER), lambda i, j: (i, 0)),
                  pl.BlockSpec((tm, ATTN_WIDTH), lambda i, j: (i, 0)),
                  pl.BlockSpec((tm, tn), lambda i, j: (blk0 + i, _COL.MS // tn + j)),
                  pl.BlockSpec((tm, tn), lambda i, j: (blk0 + i, _COL.MA // tn + j)),
                  pl.BlockSpec((D_INNER, tn), lambda i, j: (0, j)),
                  pl.BlockSpec((ATTN_WIDTH, tn), lambda i, j: (0, j))],
        out_specs=pl.BlockSpec((tm, tn), lambda i, j: (i, j)),
        out_shape=jax.ShapeDtypeStruct((t, D_MODEL), BF16),
        scratch_shapes=[pltpu.VMEM((tm // EPI_SPLIT, tn), F32) for _ in range(2 * EPI_SPLIT)],
        compiler_params=_cparams(2, BIG_VMEM_LIMIT),
        name="merge",
    )(g, og, u, u, w_ssm, w_attn)


def _out_proj_kernel(m_ref, x_ref, wo_ref, fw_ref, o_ref):
    xn = x_ref[...] + jnp.dot(m_ref[...], wo_ref[...], preferred_element_type=F32)
    ms = jnp.mean(xn * xn, axis=-1, keepdims=True)
    o_ref[...] = xn * lax.rsqrt(ms + RMS_EPS) * fw_ref[...]


def _out_proj(m, x2d, w_out, final_w, tm):
    t = m.shape[0]
    return pl.pallas_call(
        _out_proj_kernel,
        grid=(t // tm,),
        in_specs=[pl.BlockSpec((tm, D_MODEL), lambda i: (i, 0)),
                  pl.BlockSpec((tm, D_MODEL), lambda i: (i, 0)),
                  pl.BlockSpec((D_MODEL, D_MODEL), lambda i: (0, 0)),
                  pl.BlockSpec((1, D_MODEL), lambda i: (0, 0))],
        out_specs=pl.BlockSpec((tm, D_MODEL), lambda i: (i, 0)),
        out_shape=jax.ShapeDtypeStruct((t, D_MODEL), F32),
        compiler_params=_cparams(1),
        name="out_proj",
    )(m, x2d, w_out, final_w)


def _pad_lanes(v):
    return jnp.pad(v.reshape(1, -1), ((0, 0), (0, DT_PAD - v.shape[-1])))


def kernel(x_prompt, x_sample, cache_k, cache_v, state_conv, state_ssm, norm_w, w_in, conv_w, conv_b, dt_bias, a_log,
           d_skip, ssm_norm_w, w_ssm_branch, attn_sinks, w_attn_branch, w_out, rel_bias, final_norm_w):
    assert w_in.shape[0] == 1, "single-layer kernel"
    batch, seq, _ = x_prompt.shape
    dec_batch, dec_seq, _ = x_sample.shape
    w_buf = cache_k.shape[2]
    t_p, t_s = batch * seq, dec_batch * dec_seq
    assert seq % SSD_CHUNK == 0 and seq % WINDOW == 0 and dec_seq == SUBLANES and w_buf == WINDOW
    assert dec_batch % SAMPLE_BT == 0 and (t_p + t_s) % IN_TM == 0

    w_t = jnp.transpose(w_in[0])
    sc_t = jnp.transpose(state_conv[0], (1, 0, 2))
    ck_t = jnp.transpose(cache_k[0], (0, 2, 3, 1))
    cv_t = jnp.transpose(cache_v[0], (0, 2, 3, 1))

    nw = norm_w[0].reshape(1, D_MODEL)
    cw, cb = conv_w[0], conv_b[0].reshape(1, CONV_DIM)
    dtb, alog = _pad_lanes(dt_bias[0]), _pad_lanes(a_log[0])
    dskip_cols = jnp.repeat(d_skip[0], SSM_HEAD_DIM).reshape(1, D_INNER)
    snw = ssm_norm_w[0].reshape(1, D_INNER)
    w_ssm = w_ssm_branch[0].astype(BF16)
    w_attn = w_attn_branch[0].astype(BF16)
    wo = w_out[0].astype(BF16)
    fw = final_norm_w.reshape(1, D_MODEL)
    sinks = attn_sinks[0]

    xp2 = x_prompt.reshape(t_p, D_MODEL)
    xs2 = x_sample.reshape(t_s, D_MODEL)

    band_bias_t = _band_bias_t(rel_bias)
    dec_bias = _decode_bias(rel_bias, sinks, dec_seq, w_buf)

    h, dt_raw = _norm_dt(xp2, xs2, nw, w_t)
    u = _in_proj(h, w_t)
    kv_t = _kv_t(h, w_t, t_p, t_s)

    g_p, st_p = _ssd_prompt(u, dt_raw, cw, cb, dtb, alog, dskip_cols, snw, batch, seq)
    og_p = _attn_prompt(u, sinks, band_bias_t, batch, seq)
    m_p = _merge(g_p, og_p, u, w_ssm, w_attn, 0, tm=1024)
    y_p = _out_proj(m_p, xp2, wo, fw, tm=512)

    xc_s, nsc_t = _conv_sample(u, sc_t, cw, cb, t_p, dec_batch, dec_seq)
    g_s, st_s = _ssd_sample(u, dt_raw, xc_s, state_ssm[0].reshape(dec_batch, D_INNER, SSM_STATE),
                            dtb, alog, dskip_cols, snw, t_p, dec_batch, dec_seq)
    og_s, nk_t, nv_t = _attn_sample(u, kv_t, ck_t, cv_t, dec_bias, t_p, dec_batch, dec_seq)
    m_s = _merge(g_s, og_s, u, w_ssm, w_attn, t_p, tm=1024)
    y_s = _out_proj(m_s, xs2, wo, fw, tm=512)

    def seq_tail(n_rows, col0, width):
        return jnp.stack([u[(b + 1) * seq - n_rows:(b + 1) * seq, col0:col0 + width] for b in range(batch)])

    k_p = seq_tail(WINDOW, _COL.K, KV_WIDTH).reshape(1, batch, WINDOW, KV_HEADS, HEAD_DIM)
    v_p = seq_tail(WINDOW, _COL.V, KV_WIDTH).reshape(1, batch, WINDOW, KV_HEADS, HEAD_DIM)
    conv_p = seq_tail(CONV_HALO, _COL.XS, CONV_DIM)[None]
    conv_s = jnp.transpose(nsc_t, (1, 0, 2))[None]
    ssm_p = st_p.reshape(1, batch, SSM_HEADS, SSM_HEAD_DIM, SSM_STATE)
    ssm_s = st_s.reshape(1, dec_batch, SSM_HEADS, SSM_HEAD_DIM, SSM_STATE)
    k_s = jnp.transpose(nk_t, (0, 3, 1, 2))[None]
    v_s = jnp.transpose(nv_t, (0, 3, 1, 2))[None]
    return (y_p.reshape(batch, seq, D_MODEL), y_s.reshape(dec_batch, dec_seq, D_MODEL),
            k_p, v_p, conv_p, ssm_p, k_s, v_s, conv_s, ssm_s)
```

```python
import functools
import math

import numpy as np
import jax
import jax.numpy as jnp
from jax import lax
from jax.experimental import pallas as pl
from jax.experimental.pallas import tpu as pltpu

F32 = jnp.float32
BF16 = jnp.bfloat16
HIGHEST = lax.Precision.HIGHEST

D_MODEL = 2048
D_INNER = 4096
SSM_HEAD_DIM = 64
SSM_HEADS = 64
SSM_GROUPS = 8
SSM_STATE = 128
CONV_WIDTH = 4
CONV_HALO = CONV_WIDTH - 1
GN = SSM_GROUPS * SSM_STATE
CONV_DIM = D_INNER + 2 * GN
SSD_CHUNK = 128
GROUP_W = D_INNER // SSM_GROUPS
HEAD_DIM = 64
Q_HEADS = 32
KV_HEADS = 8
REP = Q_HEADS // KV_HEADS
ATTN_WIDTH = Q_HEADS * HEAD_DIM
KV_WIDTH = KV_HEADS * HEAD_DIM
WINDOW = 128
N_BUCKETS = 32
MAX_EXACT = N_BUCKETS // 2
RMS_EPS = 1e-6
ATTN_SCALE = HEAD_DIM ** -0.5

_OFF_Z = 0
_OFF_XBC = _OFF_Z + D_INNER
_OFF_DT = _OFF_XBC + CONV_DIM
_OFF_Q = _OFF_DT + SSM_HEADS
_OFF_K = _OFF_Q + ATTN_WIDTH
_OFF_V = _OFF_K + KV_WIDTH
_OFF_GA = _OFF_V + KV_WIDTH
_OFF_MS = _OFF_GA + ATTN_WIDTH
_OFF_MA = _OFF_MS + D_MODEL
_IN_COLS = _OFF_MA + D_MODEL

LANES = 128
SUBLANES = 8
DT_PAD = LANES


class _COL:
    Z = 0
    XS = Z + D_INNER
    BC = XS + D_INNER
    Q = BC + 2 * GN
    GA = Q + ATTN_WIDTH
    MS = GA + ATTN_WIDTH
    MA = MS + D_MODEL
    K = MA + D_MODEL
    V = K + KV_WIDTH
    TOTAL = V + KV_WIDTH


VMEM_LIMIT = 52 * 1024 * 1024
BIG_VMEM_LIMIT = 57 * 1024 * 1024


def _cparams(n_grid, limit=VMEM_LIMIT):
    return pltpu.CompilerParams(dimension_semantics=("arbitrary",) * n_grid, vmem_limit_bytes=limit)


LOG2E = math.log2(math.e)
NEG_LOG2E = -LOG2E


def _sigmoid(v):
    return 1.0 / (1.0 + jnp.exp2(v * NEG_LOG2E))


def _silu(v):
    return v * _sigmoid(v)


def _dot_nt(a, b):
    return lax.dot_general(a, b, (((1,), (1,)), ((), ())), preferred_element_type=F32)


NORM_TM = 512


def _norm_dt_kernel(xp_ref, xs_ref, nw_ref, wdt_ref, h_ref, dt_ref, n_prompt_tiles):
    i = pl.program_id(0)

    def emit(x_ref):
        xf = x_ref[...]
        ms = jnp.mean(xf * xf, axis=-1, keepdims=True)
        h = (xf * lax.rsqrt(ms + RMS_EPS) * nw_ref[...]).astype(BF16)
        h_ref[...] = h
        dt_ref[...] = _dot_nt(h, wdt_ref[...].astype(BF16))

    @pl.when(i < n_prompt_tiles)
    def _():
        emit(xp_ref)

    @pl.when(i >= n_prompt_tiles)
    def _():
        emit(xs_ref)


def _norm_dt(xp2, xs2, norm_w, w_t):
    tp, ts = xp2.shape[0], xs2.shape[0]
    n_p, n_s = tp // NORM_TM, ts // NORM_TM
    return pl.pallas_call(
        functools.partial(_norm_dt_kernel, n_prompt_tiles=n_p),
        grid=(n_p + n_s,),
        in_specs=[pl.BlockSpec((NORM_TM, D_MODEL), lambda i: (jnp.minimum(i, n_p - 1), 0)),
                  pl.BlockSpec((NORM_TM, D_MODEL), lambda i: (jnp.maximum(i - n_p, 0), 0)),
                  pl.BlockSpec((1, D_MODEL), lambda i: (0, 0)),
                  pl.BlockSpec((DT_PAD, D_MODEL), lambda i: (_OFF_DT // DT_PAD, 0))],
        out_specs=[pl.BlockSpec((NORM_TM, D_MODEL), lambda i: (i, 0)),
                   pl.BlockSpec((NORM_TM, DT_PAD), lambda i: (i, 0))],
        out_shape=[jax.ShapeDtypeStruct((tp + ts, D_MODEL), BF16),
                   jax.ShapeDtypeStruct((tp + ts, DT_PAD), F32)],
        compiler_params=_cparams(1),
        name="norm_dt",
    )(xp2, xs2, norm_w, w_t)


IN_TM = 1536
IN_TN = 1024
IN_CHUNK = 512
assert _COL.TOTAL % IN_TN == 0 and IN_TN == 2 * IN_CHUNK and _OFF_DT % IN_CHUNK == 0


def _src_row(c):
    straight = _OFF_DT // IN_CHUNK
    jj = c - straight
    q_t, kv_t = ATTN_WIDTH // IN_CHUNK, 2 * KV_WIDTH // IN_CHUNK
    gate_t = (ATTN_WIDTH + 2 * D_MODEL) // IN_CHUNK
    shifted = jnp.where(jj < q_t, jj, jnp.where(jj < q_t + gate_t, jj + kv_t, jj - gate_t))
    return pl.multiple_of(jnp.where(c < straight, c * IN_CHUNK, _OFF_Q + shifted * IN_CHUNK), SSM_HEADS)


IN_NSPLIT = 4
IN_MW = IN_TM // IN_NSPLIT
J_Z1 = _COL.XS // IN_TN
J_Q0, J_Q1 = _COL.Q // IN_TN, _COL.GA // IN_TN
J_GA0, J_GA1 = _COL.GA // IN_TN, _COL.MS // IN_TN
assert _COL.XS % IN_TN == 0 and _COL.Q % IN_TN == 0 and _COL.GA % IN_TN == 0 and _COL.MS % IN_TN == 0
assert math.log2(HEAD_DIM) % 2 == 0, "the folded attention scale must be a power of two"


def _in_proj_kernel(h_ref, w0_ref, w1_ref, o_ref, wbf_ref, *stage_refs):
    j = pl.program_id(0)

    @pl.when(pl.program_id(1) == 0)
    def _():
        scale = jnp.where((j >= J_Q0) & (j < J_Q1), ATTN_SCALE, 1.0)
        wbf_ref[0:IN_CHUNK, :] = (w0_ref[...] * scale).astype(BF16)
        wbf_ref[IN_CHUNK:IN_TN, :] = (w1_ref[...] * scale).astype(BF16)

    is_silu_tile = (j < J_Z1) | ((j >= J_GA0) & (j < J_GA1))

    @pl.when(is_silu_tile)
    def _():
        def silu_chunk(n):
            o_ref[n * IN_MW:(n + 1) * IN_MW, :] = _silu(stage_refs[n][...])

        for n in range(IN_NSPLIT):
            stage_refs[n][...] = _dot_nt(h_ref[n * IN_MW:(n + 1) * IN_MW, :], wbf_ref[...])
            if n > 0:
                silu_chunk(n - 1)
        silu_chunk(IN_NSPLIT - 1)

    @pl.when(jnp.logical_not(is_silu_tile))
    def _():
        o_ref[...] = _dot_nt(h_ref[...], wbf_ref[...])


def _in_proj(h, w_t):
    t = h.shape[0]
    return pl.pallas_call(
        _in_proj_kernel,
        grid=(_COL.TOTAL // IN_TN, t // IN_TM),
        in_specs=[pl.BlockSpec((IN_TM, D_MODEL), lambda j, i: (i, 0)),
                  pl.BlockSpec((pl.Element(IN_CHUNK), pl.Element(D_MODEL)), lambda j, i: (_src_row(2 * j), 0)),
                  pl.BlockSpec((pl.Element(IN_CHUNK), pl.Element(D_MODEL)), lambda j, i: (_src_row(2 * j + 1), 0))],
        out_specs=pl.BlockSpec((IN_TM, IN_TN), lambda j, i: (i, j)),
        out_shape=jax.ShapeDtypeStruct((t, _COL.TOTAL), F32),
        scratch_shapes=[pltpu.VMEM((IN_TN, D_MODEL), BF16)]
        + [pltpu.VMEM((IN_MW, IN_TN), F32) for _ in range(IN_NSPLIT)],
        compiler_params=_cparams(2, BIG_VMEM_LIMIT),
        name="in_proj",
    )(h, w_t, w_t)


def _kv_t_kernel(w_ref, h_ref, o_ref):
    o_ref[...] = _dot_nt(w_ref[...].astype(BF16), h_ref[...])


def _kv_t(h, w_t, row0, rows):
    assert row0 % rows == 0
    return pl.pallas_call(
        _kv_t_kernel,
        grid=(1,),
        in_specs=[pl.BlockSpec((pl.Element(2 * KV_WIDTH), pl.Element(D_MODEL)), lambda i: (_OFF_K, 0)),
                  pl.BlockSpec((rows, D_MODEL), lambda i: (row0 // rows, 0))],
        out_specs=pl.BlockSpec((2 * KV_WIDTH, rows), lambda i: (0, 0)),
        out_shape=jax.ShapeDtypeStruct((2 * KV_WIDTH, rows), F32),
        compiler_params=_cparams(1),
        name="kv_t",
    )(w_t, h)


def _bucket_of_dist(dist):
    n = np.maximum(dist, 0)
    nf = np.maximum(n, 1).astype(np.float32)
    large = MAX_EXACT + (np.log(nf / MAX_EXACT) / math.log(WINDOW / MAX_EXACT)
                         * (N_BUCKETS - MAX_EXACT)).astype(np.int32)
    large = np.minimum(large, N_BUCKETS - 1)
    bucket = np.where(n < MAX_EXACT, n, large)
    return np.where((dist >= 0) & (dist <= WINDOW), bucket, -1).astype(np.int32)


def _bias_of_bucket(bk, table_ref, h):
    acc = jnp.full(bk.shape, -jnp.inf, F32)
    for b in range(N_BUCKETS):
        acc = jnp.where(bk == b, table_ref[b, h], acc)
    return acc


BAND_BIAS_HEADS = 8


def _band_bias_kernel(table_ref, bucket_ref, o_ref):
    for hh in range(BAND_BIAS_HEADS):
        o_ref[hh] = _bias_of_bucket(bucket_ref[...], table_ref, pl.program_id(0) * BAND_BIAS_HEADS + hh)


def _band_bias_t(rel_bias):
    kj = np.arange(2 * WINDOW)[:, None]
    qi = np.arange(WINDOW)[None, :]
    bucket = jnp.asarray(_bucket_of_dist(qi + WINDOW - kj))
    return pl.pallas_call(
        _band_bias_kernel,
        grid=(Q_HEADS // BAND_BIAS_HEADS,),
        in_specs=[pl.BlockSpec(memory_space=pltpu.SMEM),
                  pl.BlockSpec((2 * WINDOW, WINDOW), lambda h: (0, 0))],
        out_specs=pl.BlockSpec((BAND_BIAS_HEADS, 2 * WINDOW, WINDOW), lambda h: (h, 0, 0)),
        out_shape=jax.ShapeDtypeStruct((Q_HEADS, 2 * WINDOW, WINDOW), F32),
        compiler_params=_cparams(1),
        name="band_bias",
    )(rel_bias, bucket)


def _decode_bias_kernel(table_ref, sink_ref, bucket_ref, o_ref, seq):
    g = pl.program_id(0)
    sink_lane = lax.broadcasted_iota(jnp.int32, bucket_ref.shape, 1) == seq
    for r in range(REP):
        h = g * REP + r
        bias = _bias_of_bucket(bucket_ref[...], table_ref, h)
        o_ref[r * seq:(r + 1) * seq, :] = jnp.where(sink_lane, sink_ref[h], bias)


def _decode_bias(rel_bias, sinks, seq, w_buf):
    l = np.arange(seq)[:, None]
    j = np.arange(w_buf)[None, :]
    dist_old = np.where((j < seq), l + w_buf - j, -1)
    dist_new = np.where(j < w_buf - seq, l + w_buf - (j + seq), l - (j - (w_buf - seq)))
    bucket = jnp.asarray(np.concatenate([_bucket_of_dist(dist_old), _bucket_of_dist(dist_new)], axis=1))
    return pl.pallas_call(
        functools.partial(_decode_bias_kernel, seq=seq),
        grid=(KV_HEADS,),
        in_specs=[pl.BlockSpec(memory_space=pltpu.SMEM), pl.BlockSpec(memory_space=pltpu.SMEM),
                  pl.BlockSpec(bucket.shape, lambda g: (0, 0))],
        out_specs=pl.BlockSpec((REP * seq, 2 * w_buf), lambda g: (g, 0)),
        out_shape=jax.ShapeDtypeStruct((Q_HEADS * seq, 2 * w_buf), F32),
        compiler_params=_cparams(1),
        name="decode_bias",
    )(rel_bias, sinks, bucket)


CONV_STRIP = 512


def _conv_silu(ext_ref, base, rows, cw_ref, cb_ref, xc_ref, out_row):
    n_tiles = rows // SUBLANES
    sub = lax.broadcasted_iota(jnp.int32, (SUBLANES, CONV_STRIP), 0)
    from_prev = [sub < k for k in range(CONV_WIDTH)]
    for c0 in range(0, CONV_DIM, CONV_STRIP):
        cs = slice(c0, c0 + CONV_STRIP)
        bias = jnp.broadcast_to(cb_ref[:, cs], (SUBLANES, CONV_STRIP))
        taps = [jnp.broadcast_to(cw_ref[CONV_HALO - k:CONV_WIDTH - k, cs], (SUBLANES, CONV_STRIP))
                for k in range(CONV_WIDTH)]
        prev = ext_ref[base - SUBLANES:base, cs]
        prev_rot = [None] + [pltpu.roll(prev, k, axis=0) for k in range(1, CONV_WIDTH)]
        for t in range(n_tiles):
            cur = ext_ref[base + t * SUBLANES:base + (t + 1) * SUBLANES, cs]
            acc = bias + taps[0] * cur
            cur_rot = [None]
            for k in range(1, CONV_WIDTH):
                cur_rot.append(pltpu.roll(cur, k, axis=0))
                acc = acc + taps[k] * jnp.where(from_prev[k], prev_rot[k], cur_rot[k])
            xc_ref[out_row + t * SUBLANES:out_row + (t + 1) * SUBLANES, cs] = _silu(acc)
            prev_rot = cur_rot


CONV_S_SEQS = 16


def _conv_sample_kernel(xs_ref, bc_ref, sc_ref, cw_ref, cb_ref, xc_ref, nsc_ref, ext_ref, seq_len):
    slab = SUBLANES + seq_len
    for b in range(CONV_S_SEQS):
        base = b * slab + SUBLANES
        for k in range(CONV_HALO):
            ext_ref[base - CONV_HALO + k:base - CONV_HALO + k + 1, :] = sc_ref[k, b:b + 1, :]
        ext_ref[base:base + seq_len, 0:D_INNER] = xs_ref[b * seq_len:(b + 1) * seq_len, :]
        ext_ref[base:base + seq_len, D_INNER:CONV_DIM] = bc_ref[b * seq_len:(b + 1) * seq_len, :]
        for k in range(CONV_HALO):
            row = base + seq_len - CONV_HALO + k
            nsc_ref[k, b:b + 1, :] = ext_ref[row:row + 1, :]
    sub = lax.broadcasted_iota(jnp.int32, (SUBLANES, CONV_STRIP), 0)
    for c0 in range(0, CONV_DIM, CONV_STRIP):
        cs = slice(c0, c0 + CONV_STRIP)
        bias = jnp.broadcast_to(cb_ref[:, cs], (seq_len, CONV_STRIP))
        taps = [jnp.broadcast_to(cw_ref[CONV_HALO - k:CONV_WIDTH - k, cs], (seq_len, CONV_STRIP))
                for k in range(CONV_WIDTH)]
        for b in range(CONV_S_SEQS):
            base = b * slab + SUBLANES
            prev = ext_ref[base - SUBLANES:base, cs]
            cur = ext_ref[base:base + seq_len, cs]
            acc = bias + taps[0] * cur
            for k in range(1, CONV_WIDTH):
                acc = acc + taps[k] * jnp.where(sub < k, pltpu.roll(prev, k, axis=0), pltpu.roll(cur, k, axis=0))
            xc_ref[b * seq_len:(b + 1) * seq_len, cs] = _silu(acc)


def _conv_sample(u, sc_t, conv_w, conv_b, row0, batch, seq):
    assert seq == SUBLANES and batch % CONV_S_SEQS == 0
    rows = CONV_S_SEQS * seq
    assert row0 % rows == 0
    blk0 = row0 // rows
    full = lambda shape: pl.BlockSpec(shape, lambda i: (0,) * len(shape))
    return pl.pallas_call(
        functools.partial(_conv_sample_kernel, seq_len=seq),
        grid=(batch // CONV_S_SEQS,),
        in_specs=[pl.BlockSpec((rows, D_INNER), lambda i: (blk0 + i, _COL.XS // D_INNER)),
                  pl.BlockSpec((rows, 2 * GN), lambda i: (blk0 + i, _COL.BC // (2 * GN))),
                  pl.BlockSpec((CONV_HALO, CONV_S_SEQS, CONV_DIM), lambda i: (0, i, 0)),
                  full((CONV_WIDTH, CONV_DIM)), full((1, CONV_DIM))],
        out_specs=[pl.BlockSpec((rows, CONV_DIM), lambda i: (i, 0)),
                   pl.BlockSpec((CONV_HALO, CONV_S_SEQS, CONV_DIM), lambda i: (0, i, 0))],
        out_shape=[jax.ShapeDtypeStruct((batch * seq, CONV_DIM), F32),
                   jax.ShapeDtypeStruct((CONV_HALO, batch, CONV_DIM), F32)],
        scratch_shapes=[pltpu.VMEM((CONV_S_SEQS * (SUBLANES + seq), CONV_DIM), F32)],
        compiler_params=_cparams(1),
        name="conv_sample",
    )(u, u, sc_t, conv_w, conv_b)


def _ssd_core(rows, same_seq, zs_ref, dt_ref, dtb_ref, alog_ref, dskip_ref, nw_ref, xs_ref, bc_ref, xd_ref, y_ref,
              yoff_fn, state_fn, g_ref):
    li = lax.broadcasted_iota(jnp.int32, (rows, rows), 0)
    si = lax.broadcasted_iota(jnp.int32, (rows, rows), 1)
    causal = same_seq & (si <= li)
    lane = lax.broadcasted_iota(jnp.int32, (rows, LANES), 1)
    first_half = lane < SSM_HEAD_DIM

    dt = jax.nn.softplus(dt_ref[...] + dtb_ref[...])
    a_neg = -jnp.exp(alog_ref[...])
    d_a = dt * a_neg
    acs = jnp.dot(causal.astype(F32), d_a, precision=HIGHEST, preferred_element_type=F32)
    a_end = jnp.dot(same_seq.astype(F32), d_a, precision=HIGHEST, preferred_element_type=F32)
    log_dt = jnp.log(dt)
    acs2 = acs * LOG2E
    seg_rows2 = ((acs - log_dt) * LOG2E).T
    log_w_state = log_dt + a_end - acs
    log_w_state2 = log_w_state * LOG2E
    cd_rows = jnp.exp(a_end)
    e_acs_tab, w_state_tab = jnp.exp(acs), jnp.exp(log_w_state)

    def head_cols(m, j):
        return (jnp.broadcast_to(m[:, 2 * j:2 * j + 1], (rows, LANES)),
                jnp.broadcast_to(m[:, 2 * j + 1:2 * j + 2], (rows, LANES)))

    pairs_per_group = SSM_HEADS // SSM_GROUPS // 2
    for g in range(SSM_GROUPS):
        b_g = bc_ref[:, g * SSM_STATE:(g + 1) * SSM_STATE].astype(BF16)
        c_g = bc_ref[:, GN + g * SSM_STATE:GN + (g + 1) * SSM_STATE].astype(BF16)
        cb = _dot_nt(c_g, b_g)
        yoff = yoff_fn(g, c_g)
        for jj in range(pairs_per_group):
            j = g * pairs_per_group + jj
            ps = slice(j * LANES, (j + 1) * LANES)
            xs_pair = xs_ref[:, ps]
            xs_bf16 = xs_pair.astype(BF16)
            acs_cols = head_cols(acs2, j)
            parts = []
            for hh in range(2):
                h = 2 * j + hh
                col = acs_cols[hh] if rows == LANES else acs2[:, h:h + 1]
                seg2 = col - seg_rows2[h:h + 1, :]
                decay_dt = jnp.exp2(jnp.where(causal, seg2, -jnp.inf))
                m = (cb * decay_dt).astype(BF16)
                parts.append(jnp.dot(m, xs_bf16, preferred_element_type=F32))
            yd = jnp.where(first_half, parts[0], parts[1])
            if rows == LANES:
                e_acs = jnp.exp2(jnp.where(first_half, *acs_cols))
                w_state = jnp.exp2(jnp.where(first_half, *head_cols(log_w_state2, j)))
            else:
                e_acs = jnp.where(first_half, *head_cols(e_acs_tab, j))
                w_state = jnp.where(first_half, *head_cols(w_state_tab, j))
            xd_ref[:, ps] = (xs_pair * w_state).astype(xd_ref.dtype)
            y_ref[:, ps] = yd + yoff[:, jj * LANES:(jj + 1) * LANES] * e_acs + dskip_ref[:, ps] * xs_pair
        state_fn(g, cd_rows)

    for g in range(SSM_GROUPS):
        gs = slice(g * GROUP_W, (g + 1) * GROUP_W)
        gg = y_ref[:, gs] * zs_ref[:, gs]
        ms = jnp.mean(gg * gg, axis=-1, keepdims=True)
        g_ref[:, gs] = (gg * lax.rsqrt(ms + RMS_EPS) * nw_ref[:, gs]).astype(BF16)


def _ssd_prompt_kernel(zs_ref, xs_raw_ref, bc_raw_ref, dt_ref, cw_ref, cb_ref, dtb_ref, alog_ref, dskip_ref, nw_ref,
                       g_ref, st_ref, ext_ref, xc_ref, xd_ref, y_ref, stt_ref):
    c = pl.program_id(1)
    rows = SSD_CHUNK

    @pl.when(c == 0)
    def _():
        ext_ref[0:SUBLANES, :] = jnp.zeros((SUBLANES, CONV_DIM), F32)
        stt_ref[...] = jnp.zeros(stt_ref.shape, F32)

    @pl.when(c != 0)
    def _():
        ext_ref[0:SUBLANES, :] = ext_ref[rows:rows + SUBLANES, :]

    ext_ref[SUBLANES:SUBLANES + rows, 0:D_INNER] = xs_raw_ref[...]
    ext_ref[SUBLANES:SUBLANES + rows, D_INNER:CONV_DIM] = bc_raw_ref[...]
    _conv_silu(ext_ref, SUBLANES, rows, cw_ref, cb_ref, xc_ref, 0)
    xs_ref = xc_ref.at[:, 0:D_INNER]
    bc_ref = xc_ref.at[:, D_INNER:CONV_DIM]

    def yoff_fn(g, c_g):
        return jnp.dot(c_g, stt_ref[:, g * GROUP_W:(g + 1) * GROUP_W].astype(BF16), preferred_element_type=F32)

    def state_fn(g, cd_rows):
        b_g = bc_ref[:, g * SSM_STATE:(g + 1) * SSM_STATE].astype(BF16)
        xd_g = xd_ref[:, g * GROUP_W:(g + 1) * GROUP_W]
        new = lax.dot_general(b_g, xd_g, (((0,), (0,)), ((), ())), preferred_element_type=F32)
        lane = lax.broadcasted_iota(jnp.int32, (SSM_STATE, LANES), 1)
        for jj in range(GROUP_W // LANES):
            j = g * (GROUP_W // LANES) + jj
            ps = slice(j * LANES, (j + 1) * LANES)
            cd = jnp.where(lane < SSM_HEAD_DIM,
                           jnp.broadcast_to(cd_rows[0:1, 2 * j:2 * j + 1], (SSM_STATE, LANES)),
                           jnp.broadcast_to(cd_rows[0:1, 2 * j + 1:2 * j + 2], (SSM_STATE, LANES)))
            stt_ref[:, ps] = stt_ref[:, ps] * cd + new[:, jj * LANES:(jj + 1) * LANES]

    same_seq = jnp.full((rows, rows), True)
    _ssd_core(rows, same_seq, zs_ref, dt_ref, dtb_ref, alog_ref, dskip_ref, nw_ref, xs_ref, bc_ref, xd_ref, y_ref,
              yoff_fn, state_fn, g_ref)

    @pl.when(c == pl.num_programs(1) - 1)
    def _():
        for j in range(D_INNER // LANES):
            st_ref[0, j * LANES:(j + 1) * LANES, :] = stt_ref[:, j * LANES:(j + 1) * LANES].T


def _ssd_prompt(u, dt_raw, conv_w, conv_b, dt_bias, a_log, dskip_cols, ssm_norm_w, batch, seq):
    nc = seq // SSD_CHUNK
    rows = SSD_CHUNK
    row_blk = lambda b, c: b * nc + c
    full = lambda shape: pl.BlockSpec(shape, lambda b, c: (0,) * len(shape))
    return pl.pallas_call(
        _ssd_prompt_kernel,
        grid=(batch, nc),
        in_specs=[pl.BlockSpec((rows, D_INNER), lambda b, c: (row_blk(b, c), _COL.Z // D_INNER)),
                  pl.BlockSpec((rows, D_INNER), lambda b, c: (row_blk(b, c), _COL.XS // D_INNER)),
                  pl.BlockSpec((rows, 2 * GN), lambda b, c: (row_blk(b, c), _COL.BC // (2 * GN))),
                  pl.BlockSpec((rows, DT_PAD), lambda b, c: (row_blk(b, c), 0)),
                  full((CONV_WIDTH, CONV_DIM)), full((1, CONV_DIM)), full((1, DT_PAD)), full((1, DT_PAD)),
                  full((1, D_INNER)), full((1, D_INNER))],
        out_specs=[pl.BlockSpec((rows, D_INNER), lambda b, c: (row_blk(b, c), 0)),
                   pl.BlockSpec((1, D_INNER, SSM_STATE), lambda b, c: (b, 0, 0))],
        out_shape=[jax.ShapeDtypeStruct((batch * seq, D_INNER), BF16),
                   jax.ShapeDtypeStruct((batch, D_INNER, SSM_STATE), F32)],
        scratch_shapes=[pltpu.VMEM((rows + SUBLANES, CONV_DIM), F32),
                        pltpu.VMEM((rows, CONV_DIM), F32),
                        pltpu.VMEM((rows, D_INNER), BF16),
                        pltpu.VMEM((rows, D_INNER), F32),
                        pltpu.VMEM((SSM_STATE, D_INNER), F32)],
        compiler_params=_cparams(2),
        name="ssd_prompt",
    )(u, u, u, dt_raw, conv_w, conv_b, dt_bias, a_log, dskip_cols, ssm_norm_w)


SAMPLE_BT = 4


def _ssd_sample_kernel(zs_ref, xs_ref, bc_ref, dt_ref, st_ref, dtb_ref, alog_ref, dskip_ref, nw_ref,
                       g_ref, nst_ref, xd_ref, y_ref, yoff_ref, seq_len):
    bt = SAMPLE_BT
    rows = bt * seq_len
    heads_per_group = SSM_HEADS // SSM_GROUPS

    for b in range(bt):
        rs = slice(b * seq_len, (b + 1) * seq_len)
        for g in range(SSM_GROUPS):
            c_bg = bc_ref[rs, GN + g * SSM_STATE:GN + (g + 1) * SSM_STATE].astype(BF16)
            s_bg = st_ref[b, g * GROUP_W:(g + 1) * GROUP_W, :].astype(BF16)
            yoff_ref[rs, g * GROUP_W:(g + 1) * GROUP_W] = _dot_nt(c_bg, s_bg)

    def yoff_fn(g, c_g):
        return yoff_ref[:, g * GROUP_W:(g + 1) * GROUP_W]

    def state_fn(g, cd_rows):
        for b in range(bt):
            rs = slice(b * seq_len, (b + 1) * seq_len)
            b_bg = bc_ref[rs, g * SSM_STATE:(g + 1) * SSM_STATE].astype(BF16)
            xd_bg = xd_ref[rs, g * GROUP_W:(g + 1) * GROUP_W].astype(BF16)
            new = lax.dot_general(xd_bg, b_bg, (((0,), (0,)), ((), ())), preferred_element_type=F32)
            for hh in range(heads_per_group):
                h = g * heads_per_group + hh
                hs = slice(h * SSM_HEAD_DIM, (h + 1) * SSM_HEAD_DIM)
                cd = jnp.broadcast_to(cd_rows[b * seq_len:b * seq_len + 1, h:h + 1], (SSM_HEAD_DIM, SSM_STATE))
                nst_ref[b, hs, :] = st_ref[b, hs, :] * cd + new[hh * SSM_HEAD_DIM:(hh + 1) * SSM_HEAD_DIM, :]

    li = lax.broadcasted_iota(jnp.int32, (rows, rows), 0)
    si = lax.broadcasted_iota(jnp.int32, (rows, rows), 1)
    same_seq = (li // seq_len) == (si // seq_len)
    _ssd_core(rows, same_seq, zs_ref, dt_ref, dtb_ref, alog_ref, dskip_ref, nw_ref, xs_ref, bc_ref, xd_ref, y_ref,
              yoff_fn, state_fn, g_ref)


def _ssd_sample(u, dt_raw, xc, state, dt_bias, a_log, dskip_cols, ssm_norm_w, row0, batch, seq):
    bt = SAMPLE_BT
    rows = bt * seq
    assert row0 % rows == 0
    blk0 = row0 // rows
    full = lambda shape: pl.BlockSpec(shape, lambda i: (0,) * len(shape))
    return pl.pallas_call(
        functools.partial(_ssd_sample_kernel, seq_len=seq),
        grid=(batch // bt,),
        in_specs=[pl.BlockSpec((rows, D_INNER), lambda i: (blk0 + i, _COL.Z // D_INNER)),
                  pl.BlockSpec((rows, D_INNER), lambda i: (i, 0)),
                  pl.BlockSpec((rows, 2 * GN), lambda i: (i, D_INNER // (2 * GN))),
                  pl.BlockSpec((rows, DT_PAD), lambda i: (blk0 + i, 0)),
                  pl.BlockSpec((bt, D_INNER, SSM_STATE), lambda i: (i, 0, 0)),
                  full((1, DT_PAD)), full((1, DT_PAD)), full((1, D_INNER)), full((1, D_INNER))],
        out_specs=[pl.BlockSpec((rows, D_INNER), lambda i: (i, 0)),
                   pl.BlockSpec((bt, D_INNER, SSM_STATE), lambda i: (i, 0, 0))],
        out_shape=[jax.ShapeDtypeStruct((batch * seq, D_INNER), BF16),
                   jax.ShapeDtypeStruct((batch, D_INNER, SSM_STATE), F32)],
        scratch_shapes=[pltpu.VMEM((rows, D_INNER), F32),
                        pltpu.VMEM((rows, D_INNER), F32),
                        pltpu.VMEM((rows, D_INNER), F32)],
        compiler_params=_cparams(1),
        name="ssd_sample",
    )(u, xc, xc, dt_raw, state, dt_bias, a_log, dskip_cols, ssm_norm_w)


ONES_ROWS = 16


def _attn_prompt_kernel(sink_ref, q_ref, kp_ref, ko_ref, vp_ref, vo_ref, ga_ref, bias_ref, o_ref):
    n = pl.program_id(1)
    no_prev = jnp.where(n == 0, jnp.inf, 0.0)

    def scores(g):
        ks = slice(g * HEAD_DIM, (g + 1) * HEAD_DIM)
        kk = jnp.concatenate([kp_ref[:, ks], ko_ref[:, ks]], axis=0).astype(BF16)
        vv_t = jnp.concatenate([jnp.concatenate([vp_ref[:, ks], vo_ref[:, ks]], axis=0).T,
                                jnp.ones((ONES_ROWS, 2 * WINDOW), F32)], axis=0).astype(BF16)
        s_list = []
        for r in range(REP):
            h = g * REP + r
            q = q_ref[:, h * HEAD_DIM:(h + 1) * HEAD_DIM].astype(BF16)
            s_list.append(_dot_nt(kk, q))
        return vv_t, s_list

    def finish(g, vv_t, s_list):
        probs = []
        for r in range(REP):
            h = g * REP + r
            s_prev = s_list[r][0:WINDOW] + bias_ref[h, 0:WINDOW, :]
            s_own = s_list[r][WINDOW:2 * WINDOW] + bias_ref[h, WINDOW:2 * WINDOW, :]
            sink = sink_ref[h]
            m = jnp.maximum(jnp.maximum(jnp.max(s_prev, axis=0, keepdims=True) - no_prev,
                                        jnp.max(s_own, axis=0, keepdims=True)), sink)
            p = jnp.concatenate([jnp.exp(s_prev - (m + no_prev)), jnp.exp(s_own - m)], axis=0)
            probs.append((p.astype(BF16), jnp.exp(sink - m)))
        o_t = []
        for p, p_sink in probs:
            pv = jnp.dot(vv_t, p, preferred_element_type=F32)
            o_t.append(pv[0:HEAD_DIM] / (pv[HEAD_DIM:HEAD_DIM + 1] + p_sink))
        for rp in range(REP // 2):
            o_pair = jnp.concatenate(o_t[2 * rp:2 * rp + 2], axis=0).T
            ps = slice((g * REP + 2 * rp) * HEAD_DIM, (g * REP + 2 * rp + 2) * HEAD_DIM)
            o_ref[:, ps] = (o_pair * ga_ref[:, ps]).astype(BF16)

    pending = scores(0)
    for g in range(KV_HEADS):
        cur = pending
        if g + 1 < KV_HEADS:
            pending = scores(g + 1)
        finish(g, *cur)


def _attn_prompt(u, sinks, bias, batch, seq):
    nb = seq // WINDOW
    row = lambda b, n: b * nb + n
    prev = lambda b, n: jnp.maximum(b * nb + n - 1, 0)
    return pl.pallas_call(
        _attn_prompt_kernel,
        grid=(batch, nb),
        in_specs=[pl.BlockSpec(memory_space=pltpu.SMEM),
                  pl.BlockSpec((WINDOW, ATTN_WIDTH), lambda b, n: (row(b, n), _COL.Q // ATTN_WIDTH)),
                  pl.BlockSpec((WINDOW, KV_WIDTH), lambda b, n: (prev(b, n), _COL.K // KV_WIDTH)),
                  pl.BlockSpec((WINDOW, KV_WIDTH), lambda b, n: (row(b, n), _COL.K // KV_WIDTH)),
                  pl.BlockSpec((WINDOW, KV_WIDTH), lambda b, n: (prev(b, n), _COL.V // KV_WIDTH)),
                  pl.BlockSpec((WINDOW, KV_WIDTH), lambda b, n: (row(b, n), _COL.V // KV_WIDTH)),
                  pl.BlockSpec((WINDOW, ATTN_WIDTH), lambda b, n: (row(b, n), _COL.GA // ATTN_WIDTH)),
                  pl.BlockSpec((Q_HEADS, 2 * WINDOW, WINDOW), lambda b, n: (0, 0, 0))],
        out_specs=pl.BlockSpec((WINDOW, ATTN_WIDTH), lambda b, n: (row(b, n), 0)),
        out_shape=jax.ShapeDtypeStruct((batch * seq, ATTN_WIDTH), BF16),
        compiler_params=_cparams(2),
        name="attn_prompt",
    )(sinks, u, u, u, u, u, u, bias)


def _attn_sample_kernel(q_ref, ga_ref, kvt_ref, ck_ref, cv_ref, bias_ref, o_ref, nk_ref, nv_ref,
                        s_ref, p_ref, seq_len):
    bt = SAMPLE_BT
    w_buf = ck_ref.shape[-1]
    keep = w_buf - seq_len
    unit_rows = REP * seq_len
    seq_rows = KV_HEADS * unit_rows
    lane0 = (pl.program_id(0) * bt * seq_len) % kvt_ref.shape[1]
    units = [(b, g) for b in range(bt) for g in range(KV_HEADS)]

    lane = lax.broadcasted_iota(jnp.int32, (KV_WIDTH, w_buf), 1)
    for c_ref, n_ref, row0 in ((ck_ref, nk_ref, 0), (cv_ref, nv_ref, KV_WIDTH)):
        fresh = kvt_ref[row0:row0 + KV_WIDTH, :]
        for b in range(bt):
            new_shift = (keep + w_buf - (lane0 + b * seq_len)) % w_buf
            old = c_ref[b].reshape(KV_WIDTH, w_buf)
            new = jnp.where(lane < keep, pltpu.roll(old, keep, axis=1), pltpu.roll(fresh, new_shift, axis=1))
            n_ref[b] = new.reshape(KV_HEADS, HEAD_DIM, w_buf)

    def both(c_ref, n_ref, b, g):
        return jnp.concatenate([c_ref[b, g], n_ref[b, g]], axis=1).astype(BF16)

    for u, (b, g) in enumerate(units):
        rs = slice(b * seq_len, (b + 1) * seq_len)
        qs = jnp.concatenate([q_ref[rs, (g * REP + r) * HEAD_DIM:(g * REP + r + 1) * HEAD_DIM] for r in range(REP)],
                             axis=0)
        s_ref[u * unit_rows:(u + 1) * unit_rows, :] = jnp.dot(
            qs.astype(BF16), both(ck_ref, nk_ref, b, g), preferred_element_type=F32)

    col = lax.broadcasted_iota(jnp.int32, (seq_rows, 2 * w_buf), 1)
    not_sink = (col != seq_len).astype(F32)
    for b in range(bt):
        rows_b = slice(b * seq_rows, (b + 1) * seq_rows)
        s = s_ref[rows_b, :] * not_sink + bias_ref[...]
        m = jnp.max(s, axis=-1, keepdims=True)
        p = jnp.exp(s - m)
        inv = 1.0 / jnp.sum(p, axis=-1, keepdims=True)
        p_ref[rows_b, :] = (p * inv * not_sink).astype(BF16)

    for u, (b, g) in enumerate(units):
        o = _dot_nt(p_ref[u * unit_rows:(u + 1) * unit_rows, :], both(cv_ref, nv_ref, b, g))
        rs = slice(b * seq_len, (b + 1) * seq_len)
        for r in range(REP):
            hs = slice((g * REP + r) * HEAD_DIM, (g * REP + r + 1) * HEAD_DIM)
            o_ref[rs, hs] = (o[r * seq_len:(r + 1) * seq_len, :] * ga_ref[rs, hs]).astype(BF16)


def _attn_sample(u, kv_t, cache_k, cache_v, bias, row0, batch, seq):
    bt = SAMPLE_BT
    rows = bt * seq
    w_buf = cache_k.shape[-1]
    assert w_buf == LANES and row0 % rows == 0 and LANES % rows == 0
    blk0 = row0 // rows
    score_rows = bt * Q_HEADS * seq
    cache_spec = pl.BlockSpec((bt, KV_HEADS, HEAD_DIM, w_buf), lambda i: (i, 0, 0, 0))
    return pl.pallas_call(
        functools.partial(_attn_sample_kernel, seq_len=seq),
        grid=(batch // bt,),
        in_specs=[pl.BlockSpec((rows, ATTN_WIDTH), lambda i: (blk0 + i, _COL.Q // ATTN_WIDTH)),
                  pl.BlockSpec((rows, ATTN_WIDTH), lambda i: (blk0 + i, _COL.GA // ATTN_WIDTH)),
                  pl.BlockSpec((2 * KV_WIDTH, LANES), lambda i: (0, i * rows // LANES)),
                  cache_spec, cache_spec,
                  pl.BlockSpec(bias.shape, lambda i: (0, 0))],
        out_specs=[pl.BlockSpec((rows, ATTN_WIDTH), lambda i: (i, 0)), cache_spec, cache_spec],
        out_shape=[jax.ShapeDtypeStruct((batch * seq, ATTN_WIDTH), BF16),
                   jax.ShapeDtypeStruct(cache_k.shape, F32),
                   jax.ShapeDtypeStruct(cache_v.shape, F32)],
        scratch_shapes=[pltpu.VMEM((score_rows, 2 * w_buf), F32),
                        pltpu.VMEM((score_rows, 2 * w_buf), BF16)],
        compiler_params=_cparams(1),
        name="attn_sample",
    )(u, u, kv_t, cache_k, cache_v, bias)


MERGE_TN = 512
EPI_SPLIT = 4


def _merge_kernel(g_ref, og_ref, ms_ref, ma_ref, ws_ref, wa_ref, o_ref, *stage_refs):
    rows = o_ref.shape[0] // EPI_SPLIT
    ssm_refs, attn_refs = stage_refs[:EPI_SPLIT], stage_refs[EPI_SPLIT:]

    def gate_chunk(n):
        rs = slice(n * rows, (n + 1) * rows)
        o_ref[rs, :] = (_sigmoid(ms_ref[rs, :]) * ssm_refs[n][...]
                        + _sigmoid(ma_ref[rs, :]) * attn_refs[n][...]).astype(BF16)

    for n in range(EPI_SPLIT):
        rs = slice(n * rows, (n + 1) * rows)
        ssm_refs[n][...] = jnp.dot(g_ref[rs, :], ws_ref[...], preferred_element_type=F32)
        attn_refs[n][...] = jnp.dot(og_ref[rs, :], wa_ref[...], preferred_element_type=F32)
        if n > 0:
            gate_chunk(n - 1)
    gate_chunk(EPI_SPLIT - 1)


def _merge(g, og, u, w_ssm, w_attn, row0, tm):
    t = g.shape[0]
    tn = MERGE_TN
    assert row0 % tm == 0
    blk0 = row0 // tm
    return pl.pallas_call(
        _merge_kernel,
        grid=(t // tm, D_MODEL // tn),
        in_specs=[pl.BlockSpec((tm, D_INNER), lambda i, j: (i, 0)),
                  pl.BlockSpec((tm, ATTN_WIDTH), lambda i, j: (i, 0)),
                  pl.BlockSpec((tm, tn), lambda i, j: (blk0 + i, _COL.MS // tn + j)),
                  pl.BlockSpec((tm, tn), lambda i, j: (blk0 + i, _COL.MA // tn + j)),
                  pl.BlockSpec((D_INNER, tn), lambda i, j: (0, j)),
                  pl.BlockSpec((ATTN_WIDTH, tn), lambda i, j: (0, j))],
        out_specs=pl.BlockSpec((tm, tn), lambda i, j: (i, j)),
        out_shape=jax.ShapeDtypeStruct((t, D_MODEL), BF16),
        scratch_shapes=[pltpu.VMEM((tm // EPI_SPLIT, tn), F32) for _ in range(2 * EPI_SPLIT)],
        compiler_params=_cparams(2, BIG_VMEM_LIMIT),
        name="merge",
    )(g, og, u, u, w_ssm, w_attn)


def _out_proj_kernel(m_ref, x_ref, wo_ref, fw_ref, o_ref):
    xn = x_ref[...] + jnp.dot(m_ref[...], wo_ref[...], preferred_element_type=F32)
    ms = jnp.mean(xn * xn, axis=-1, keepdims=True)
    o_ref[...] = xn * lax.rsqrt(ms + RMS_EPS) * fw_ref[...]


def _out_proj(m, x2d, w_out, final_w, tm):
    t = m.shape[0]
    return pl.pallas_call(
        _out_proj_kernel,
        grid=(t // tm,),
        in_specs=[pl.BlockSpec((tm, D_MODEL), lambda i: (i, 0)),
                  pl.BlockSpec((tm, D_MODEL), lambda i: (i, 0)),
                  pl.BlockSpec((D_MODEL, D_MODEL), lambda i: (0, 0)),
                  pl.BlockSpec((1, D_MODEL), lambda i: (0, 0))],
        out_specs=pl.BlockSpec((tm, D_MODEL), lambda i: (i, 0)),
        out_shape=jax.ShapeDtypeStruct((t, D_MODEL), F32),
        compiler_params=_cparams(1),
        name="out_proj",
    )(m, x2d, w_out, final_w)


def _pad_lanes(v):
    return jnp.pad(v.reshape(1, -1), ((0, 0), (0, DT_PAD - v.shape[-1])))


def kernel(x_prompt, x_sample, cache_k, cache_v, state_conv, state_ssm, norm_w, w_in, conv_w, conv_b, dt_bias, a_log,
           d_skip, ssm_norm_w, w_ssm_branch, attn_sinks, w_attn_branch, w_out, rel_bias, final_norm_w):
    assert w_in.shape[0] == 1, "single-layer kernel"
    batch, seq, _ = x_prompt.shape
    dec_batch, dec_seq, _ = x_sample.shape
    w_buf = cache_k.shape[2]
    t_p, t_s = batch * seq, dec_batch * dec_seq
    assert seq % SSD_CHUNK == 0 and seq % WINDOW == 0 and dec_seq == SUBLANES and w_buf == WINDOW
    assert dec_batch % SAMPLE_BT == 0 and (t_p + t_s) % IN_TM == 0

    w_t = jnp.transpose(w_in[0])
    sc_t = jnp.transpose(state_conv[0], (1, 0, 2))
    ck_t = jnp.transpose(cache_k[0], (0, 2, 3, 1))
    cv_t = jnp.transpose(cache_v[0], (0, 2, 3, 1))

    nw = norm_w[0].reshape(1, D_MODEL)
    cw, cb = conv_w[0], conv_b[0].reshape(1, CONV_DIM)
    dtb, alog = _pad_lanes(dt_bias[0]), _pad_lanes(a_log[0])
    dskip_cols = jnp.repeat(d_skip[0], SSM_HEAD_DIM).reshape(1, D_INNER)
    snw = ssm_norm_w[0].reshape(1, D_INNER)
    w_ssm = w_ssm_branch[0].astype(BF16)
    w_attn = w_attn_branch[0].astype(BF16)
    wo = w_out[0].astype(BF16)
    fw = final_norm_w.reshape(1, D_MODEL)
    sinks = attn_sinks[0]

    xp2 = x_prompt.reshape(t_p, D_MODEL)
    xs2 = x_sample.reshape(t_s, D_MODEL)

    band_bias_t = _band_bias_t(rel_bias)
    dec_bias = _decode_bias(rel_bias, sinks, dec_seq, w_buf)

    h, dt_raw = _norm_dt(xp2, xs2, nw, w_t)
    u = _in_proj(h, w_t)
    kv_t = _kv_t(h, w_t, t_p, t_s)

    g_p, st_p = _ssd_prompt(u, dt_raw, cw, cb, dtb, alog, dskip_cols, snw, batch, seq)
    og_p = _attn_prompt(u, sinks, band_bias_t, batch, seq)
    m_p = _merge(g_p, og_p, u, w_ssm, w_attn, 0, tm=1024)
    y_p = _out_proj(m_p, xp2, wo, fw, tm=512)

    xc_s, nsc_t = _conv_sample(u, sc_t, cw, cb, t_p, dec_batch, dec_seq)
    g_s, st_s = _ssd_sample(u, dt_raw, xc_s, state_ssm[0].reshape(dec_batch, D_INNER, SSM_STATE),
                            dtb, alog, dskip_cols, snw, t_p, dec_batch, dec_seq)
    og_s, nk_t, nv_t = _attn_sample(u, kv_t, ck_t, cv_t, dec_bias, t_p, dec_batch, dec_seq)
    m_s = _merge(g_s, og_s, u, w_ssm, w_attn, t_p, tm=1024)
    y_s = _out_proj(m_s, xs2, wo, fw, tm=512)

    def seq_tail(n_rows, col0, width):
        return jnp.stack([u[(b + 1) * seq - n_rows:(b + 1) * seq, col0:col0 + width] for b in range(batch)])

    k_p = seq_tail(WINDOW, _COL.K, KV_WIDTH).reshape(1, batch, WINDOW, KV_HEADS, HEAD_DIM)
    v_p = seq_tail(WINDOW, _COL.V, KV_WIDTH).reshape(1, batch, WINDOW, KV_HEADS, HEAD_DIM)
    conv_p = seq_tail(CONV_HALO, _COL.XS, CONV_DIM)[None]
    conv_s = jnp.transpose(nsc_t, (1, 0, 2))[None]
    ssm_p = st_p.reshape(1, batch, SSM_HEADS, SSM_HEAD_DIM, SSM_STATE)
    ssm_s = st_s.reshape(1, dec_batch, SSM_HEADS, SSM_HEAD_DIM, SSM_STATE)
    k_s = jnp.transpose(nk_t, (0, 3, 1, 2))[None]
    v_s = jnp.transpose(nv_t, (0, 3, 1, 2))[None]
    return (y_p.reshape(batch, seq, D_MODEL), y_s.reshape(dec_batch, dec_seq, D_MODEL),
            k_p, v_p, conv_p, ssm_p, k_s, v_s, conv_s, ssm_s)
```

```python
import functools
import math

import numpy as np
import jax
import jax.numpy as jnp
from jax import lax
from jax.experimental import pallas as pl
from jax.experimental.pallas import tpu as pltpu

F32 = jnp.float32
BF16 = jnp.bfloat16
HIGHEST = lax.Precision.HIGHEST

D_MODEL = 2048
D_INNER = 4096
SSM_HEAD_DIM = 64
SSM_HEADS = 64
SSM_GROUPS = 8
SSM_STATE = 128
CONV_WIDTH = 4
CONV_HALO = CONV_WIDTH - 1
GN = SSM_GROUPS * SSM_STATE
CONV_DIM = D_INNER + 2 * GN
SSD_CHUNK = 128
GROUP_W = D_INNER // SSM_GROUPS
HEAD_DIM = 64
Q_HEADS = 32
KV_HEADS = 8
REP = Q_HEADS // KV_HEADS
ATTN_WIDTH = Q_HEADS * HEAD_DIM
KV_WIDTH = KV_HEADS * HEAD_DIM
WINDOW = 128
N_BUCKETS = 32
MAX_EXACT = N_BUCKETS // 2
RMS_EPS = 1e-6
ATTN_SCALE = HEAD_DIM ** -0.5

_OFF_Z = 0
_OFF_XBC = _OFF_Z + D_INNER
_OFF_DT = _OFF_XBC + CONV_DIM
_OFF_Q = _OFF_DT + SSM_HEADS
_OFF_K = _OFF_Q + ATTN_WIDTH
_OFF_V = _OFF_K + KV_WIDTH
_OFF_GA = _OFF_V + KV_WIDTH
_OFF_MS = _OFF_GA + ATTN_WIDTH
_OFF_MA = _OFF_MS + D_MODEL
_IN_COLS = _OFF_MA + D_MODEL

LANES = 128
SUBLANES = 8
DT_PAD = LANES


class _COL:
    Z = 0
    XS = Z + D_INNER
    BC = XS + D_INNER
    Q = BC + 2 * GN
    GA = Q + ATTN_WIDTH
    MS = GA + ATTN_WIDTH
    MA = MS + D_MODEL
    K = MA + D_MODEL
    V = K + KV_WIDTH
    TOTAL = V + KV_WIDTH


VMEM_LIMIT = 52 * 1024 * 1024
BIG_VMEM_LIMIT = 57 * 1024 * 1024


def _cparams(n_grid, limit=VMEM_LIMIT):
    return pltpu.CompilerParams(dimension_semantics=("arbitrary",) * n_grid, vmem_limit_bytes=limit)


LOG2E = math.log2(math.e)
NEG_LOG2E = -LOG2E


def _sigmoid(v):
    return 1.0 / (1.0 + jnp.exp2(v * NEG_LOG2E))


def _silu(v):
    return v * _sigmoid(v)


def _dot_nt(a, b):
    return lax.dot_general(a, b, (((1,), (1,)), ((), ())), preferred_element_type=F32)


NORM_TM = 512


def _norm_dt_kernel(xp_ref, xs_ref, nw_ref, wdt_ref, h_ref, dt_ref, n_prompt_tiles):
    i = pl.program_id(0)

    def emit(x_ref):
        xf = x_ref[...]
        ms = jnp.mean(xf * xf, axis=-1, keepdims=True)
        h = (xf * lax.rsqrt(ms + RMS_EPS) * nw_ref[...]).astype(BF16)
        h_ref[...] = h
        dt_ref[...] = _dot_nt(h, wdt_ref[...].astype(BF16))

    @pl.when(i < n_prompt_tiles)
    def _():
        emit(xp_ref)

    @pl.when(i >= n_prompt_tiles)
    def _():
        emit(xs_ref)


def _norm_dt(xp2, xs2, norm_w, w_t):
    tp, ts = xp2.shape[0], xs2.shape[0]
    n_p, n_s = tp // NORM_TM, ts // NORM_TM
    return pl.pallas_call(
        functools.partial(_norm_dt_kernel, n_prompt_tiles=n_p),
        grid=(n_p + n_s,),
        in_specs=[pl.BlockSpec((NORM_TM, D_MODEL), lambda i: (jnp.minimum(i, n_p - 1), 0)),
                  pl.BlockSpec((NORM_TM, D_MODEL), lambda i: (jnp.maximum(i - n_p, 0), 0)),
                  pl.BlockSpec((1, D_MODEL), lambda i: (0, 0)),
                  pl.BlockSpec((DT_PAD, D_MODEL), lambda i: (_OFF_DT // DT_PAD, 0))],
        out_specs=[pl.BlockSpec((NORM_TM, D_MODEL), lambda i: (i, 0)),
                   pl.BlockSpec((NORM_TM, DT_PAD), lambda i: (i, 0))],
        out_shape=[jax.ShapeDtypeStruct((tp + ts, D_MODEL), BF16),
                   jax.ShapeDtypeStruct((tp + ts, DT_PAD), F32)],
        compiler_params=_cparams(1),
        name="norm_dt",
    )(xp2, xs2, norm_w, w_t)


IN_TM = 1536
IN_TN = 1024
IN_CHUNK = 512
assert _COL.TOTAL % IN_TN == 0 and IN_TN == 2 * IN_CHUNK and _OFF_DT % IN_CHUNK == 0


def _src_row(c):
    straight = _OFF_DT // IN_CHUNK
    jj = c - straight
    q_t, kv_t = ATTN_WIDTH // IN_CHUNK, 2 * KV_WIDTH // IN_CHUNK
    gate_t = (ATTN_WIDTH + 2 * D_MODEL) // IN_CHUNK
    shifted = jnp.where(jj < q_t, jj, jnp.where(jj < q_t + gate_t, jj + kv_t, jj - gate_t))
    return pl.multiple_of(jnp.where(c < straight, c * IN_CHUNK, _OFF_Q + shifted * IN_CHUNK), SSM_HEADS)


IN_NSPLIT = 4
IN_MW = IN_TM // IN_NSPLIT
J_Z1 = _COL.XS // IN_TN
J_Q0, J_Q1 = _COL.Q // IN_TN, _COL.GA // IN_TN
J_GA0, J_GA1 = _COL.GA // IN_TN, _COL.MS // IN_TN
assert _COL.XS % IN_TN == 0 and _COL.Q % IN_TN == 0 and _COL.GA % IN_TN == 0 and _COL.MS % IN_TN == 0
assert math.log2(HEAD_DIM) % 2 == 0, "the folded attention scale must be a power of two"


def _in_proj_kernel(h_ref, w0_ref, w1_ref, o_ref, wbf_ref, *stage_refs):
    j = pl.program_id(0)

    @pl.when(pl.program_id(1) == 0)
    def _():
        scale = jnp.where((j >= J_Q0) & (j < J_Q1), ATTN_SCALE, 1.0)
        wbf_ref[0:IN_CHUNK, :] = (w0_ref[...] * scale).astype(BF16)
        wbf_ref[IN_CHUNK:IN_TN, :] = (w1_ref[...] * scale).astype(BF16)

    is_silu_tile = (j < J_Z1) | ((j >= J_GA0) & (j < J_GA1))

    @pl.when(is_silu_tile)
    def _():
        def silu_chunk(n):
            o_ref[n * IN_MW:(n + 1) * IN_MW, :] = _silu(stage_refs[n][...])

        for n in range(IN_NSPLIT):
            stage_refs[n][...] = _dot_nt(h_ref[n * IN_MW:(n + 1) * IN_MW, :], wbf_ref[...])
            if n > 0:
                silu_chunk(n - 1)
        silu_chunk(IN_NSPLIT - 1)

    @pl.when(jnp.logical_not(is_silu_tile))
    def _():
        o_ref[...] = _dot_nt(h_ref[...], wbf_ref[...])


def _in_proj(h, w_t):
    t = h.shape[0]
    return pl.pallas_call(
        _in_proj_kernel,
        grid=(_COL.TOTAL // IN_TN, t // IN_TM),
        in_specs=[pl.BlockSpec((IN_TM, D_MODEL), lambda j, i: (i, 0)),
                  pl.BlockSpec((pl.Element(IN_CHUNK), pl.Element(D_MODEL)), lambda j, i: (_src_row(2 * j), 0)),
                  pl.BlockSpec((pl.Element(IN_CHUNK), pl.Element(D_MODEL)), lambda j, i: (_src_row(2 * j + 1), 0))],
        out_specs=pl.BlockSpec((IN_TM, IN_TN), lambda j, i: (i, j)),
        out_shape=jax.ShapeDtypeStruct((t, _COL.TOTAL), F32),
        scratch_shapes=[pltpu.VMEM((IN_TN, D_MODEL), BF16)]
        + [pltpu.VMEM((IN_MW, IN_TN), F32) for _ in range(IN_NSPLIT)],
        compiler_params=_cparams(2, BIG_VMEM_LIMIT),
        name="in_proj",
    )(h, w_t, w_t)


def _kv_t_kernel(w_ref, h_ref, o_ref):
    o_ref[...] = _dot_nt(w_ref[...].astype(BF16), h_ref[...])


def _kv_t(h, w_t, row0, rows):
    assert row0 % rows == 0
    return pl.pallas_call(
        _kv_t_kernel,
        grid=(1,),
        in_specs=[pl.BlockSpec((pl.Element(2 * KV_WIDTH), pl.Element(D_MODEL)), lambda i: (_OFF_K, 0)),
                  pl.BlockSpec((rows, D_MODEL), lambda i: (row0 // rows, 0))],
        out_specs=pl.BlockSpec((2 * KV_WIDTH, rows), lambda i: (0, 0)),
        out_shape=jax.ShapeDtypeStruct((2 * KV_WIDTH, rows), F32),
        compiler_params=_cparams(1),
        name="kv_t",
    )(w_t, h)


def _bucket_of_dist(dist):
    n = np.maximum(dist, 0)
    nf = np.maximum(n, 1).astype(np.float32)
    large = MAX_EXACT + (np.log(nf / MAX_EXACT) / math.log(WINDOW / MAX_EXACT)
                         * (N_BUCKETS - MAX_EXACT)).astype(np.int32)
    large = np.minimum(large, N_BUCKETS - 1)
    bucket = np.where(n < MAX_EXACT, n, large)
    return np.where((dist >= 0) & (dist <= WINDOW), bucket, -1).astype(np.int32)


def _bias_of_bucket(bk, table_ref, h):
    acc = jnp.full(bk.shape, -jnp.inf, F32)
    for b in range(N_BUCKETS):
        acc = jnp.where(bk == b, table_ref[b, h], acc)
    return acc


BAND_BIAS_HEADS = 8


def _band_bias_kernel(table_ref, bucket_ref, o_ref):
    for hh in range(BAND_BIAS_HEADS):
        o_ref[hh] = _bias_of_bucket(bucket_ref[...], table_ref, pl.program_id(0) * BAND_BIAS_HEADS + hh)


def _band_bias_t(rel_bias):
    kj = np.arange(2 * WINDOW)[:, None]
    qi = np.arange(WINDOW)[None, :]
    bucket = jnp.asarray(_bucket_of_dist(qi + WINDOW - kj))
    return pl.pallas_call(
        _band_bias_kernel,
        grid=(Q_HEADS // BAND_BIAS_HEADS,),
        in_specs=[pl.BlockSpec(memory_space=pltpu.SMEM),
                  pl.BlockSpec((2 * WINDOW, WINDOW), lambda h: (0, 0))],
        out_specs=pl.BlockSpec((BAND_BIAS_HEADS, 2 * WINDOW, WINDOW), lambda h: (h, 0, 0)),
        out_shape=jax.ShapeDtypeStruct((Q_HEADS, 2 * WINDOW, WINDOW), F32),
        compiler_params=_cparams(1),
        name="band_bias",
    )(rel_bias, bucket)


def _decode_bias_kernel(table_ref, sink_ref, bucket_ref, o_ref, seq):
    g = pl.program_id(0)
    sink_lane = lax.broadcasted_iota(jnp.int32, bucket_ref.shape, 1) == seq
    for r in range(REP):
        h = g * REP + r
        bias = _bias_of_bucket(bucket_ref[...], table_ref, h)
        o_ref[r * seq:(r + 1) * seq, :] = jnp.where(sink_lane, sink_ref[h], bias)


def _decode_bias(rel_bias, sinks, seq, w_buf):
    l = np.arange(seq)[:, None]
    j = np.arange(w_buf)[None, :]
    dist_old = np.where((j < seq), l + w_buf - j, -1)
    dist_new = np.where(j < w_buf - seq, l + w_buf - (j + seq), l - (j - (w_buf - seq)))
    bucket = jnp.asarray(np.concatenate([_bucket_of_dist(dist_old), _bucket_of_dist(dist_new)], axis=1))
    return pl.pallas_call(
        functools.partial(_decode_bias_kernel, seq=seq),
        grid=(KV_HEADS,),
        in_specs=[pl.BlockSpec(memory_space=pltpu.SMEM), pl.BlockSpec(memory_space=pltpu.SMEM),
                  pl.BlockSpec(bucket.shape, lambda g: (0, 0))],
        out_specs=pl.BlockSpec((REP * seq, 2 * w_buf), lambda g: (g, 0)),
        out_shape=jax.ShapeDtypeStruct((Q_HEADS * seq, 2 * w_buf), F32),
        compiler_params=_cparams(1),
        name="decode_bias",
    )(rel_bias, sinks, bucket)


CONV_STRIP = 512


def _conv_silu(ext_ref, base, rows, cw_ref, cb_ref, xc_ref, out_row):
    n_tiles = rows // SUBLANES
    sub = lax.broadcasted_iota(jnp.int32, (SUBLANES, CONV_STRIP), 0)
    from_prev = [sub < k for k in range(CONV_WIDTH)]
    for c0 in range(0, CONV_DIM, CONV_STRIP):
        cs = slice(c0, c0 + CONV_STRIP)
        bias = jnp.broadcast_to(cb_ref[:, cs], (SUBLANES, CONV_STRIP))
        taps = [jnp.broadcast_to(cw_ref[CONV_HALO - k:CONV_WIDTH - k, cs], (SUBLANES, CONV_STRIP))
                for k in range(CONV_WIDTH)]
        prev = ext_ref[base - SUBLANES:base, cs]
        prev_rot = [None] + [pltpu.roll(prev, k, axis=0) for k in range(1, CONV_WIDTH)]
        for t in range(n_tiles):
            cur = ext_ref[base + t * SUBLANES:base + (t + 1) * SUBLANES, cs]
            acc = bias + taps[0] * cur
            cur_rot = [None]
            for k in range(1, CONV_WIDTH):
                cur_rot.append(pltpu.roll(cur, k, axis=0))
                acc = acc + taps[k] * jnp.where(from_prev[k], prev_rot[k], cur_rot[k])
            xc_ref[out_row + t * SUBLANES:out_row + (t + 1) * SUBLANES, cs] = _silu(acc)
            prev_rot = cur_rot


CONV_S_SEQS = 16


def _conv_sample_kernel(xs_ref, bc_ref, sc_ref, cw_ref, cb_ref, xc_ref, nsc_ref, ext_ref, seq_len):
    slab = SUBLANES + seq_len
    for b in range(CONV_S_SEQS):
        base = b * slab + SUBLANES
        for k in range(CONV_HALO):
            ext_ref[base - CONV_HALO + k:base - CONV_HALO + k + 1, :] = sc_ref[k, b:b + 1, :]
        ext_ref[base:base + seq_len, 0:D_INNER] = xs_ref[b * seq_len:(b + 1) * seq_len, :]
        ext_ref[base:base + seq_len, D_INNER:CONV_DIM] = bc_ref[b * seq_len:(b + 1) * seq_len, :]
        for k in range(CONV_HALO):
            row = base + seq_len - CONV_HALO + k
            nsc_ref[k, b:b + 1, :] = ext_ref[row:row + 1, :]
    sub = lax.broadcasted_iota(jnp.int32, (SUBLANES, CONV_STRIP), 0)
    for c0 in range(0, CONV_DIM, CONV_STRIP):
        cs = slice(c0, c0 + CONV_STRIP)
        bias = jnp.broadcast_to(cb_ref[:, cs], (seq_len, CONV_STRIP))
        taps = [jnp.broadcast_to(cw_ref[CONV_HALO - k:CONV_WIDTH - k, cs], (seq_len, CONV_STRIP))
                for k in range(CONV_WIDTH)]
        for b in range(CONV_S_SEQS):
            base = b * slab + SUBLANES
            prev = ext_ref[base - SUBLANES:base, cs]
            cur = ext_ref[base:base + seq_len, cs]
            acc = bias + taps[0] * cur
            for k in range(1, CONV_WIDTH):
                acc = acc + taps[k] * jnp.where(sub < k, pltpu.roll(prev, k, axis=0), pltpu.roll(cur, k, axis=0))
            xc_ref[b * seq_len:(b + 1) * seq_len, cs] = _silu(acc)


def _conv_sample(u, sc_t, conv_w, conv_b, row0, batch, seq):
    assert seq == SUBLANES and batch % CONV_S_SEQS == 0
    rows = CONV_S_SEQS * seq
    assert row0 % rows == 0
    blk0 = row0 // rows
    full = lambda shape: pl.BlockSpec(shape, lambda i: (0,) * len(shape))
    return pl.pallas_call(
        functools.partial(_conv_sample_kernel, seq_len=seq),
        grid=(batch // CONV_S_SEQS,),
        in_specs=[pl.BlockSpec((rows, D_INNER), lambda i: (blk0 + i, _COL.XS // D_INNER)),
                  pl.BlockSpec((rows, 2 * GN), lambda i: (blk0 + i, _COL.BC // (2 * GN))),
                  pl.BlockSpec((CONV_HALO, CONV_S_SEQS, CONV_DIM), lambda i: (0, i, 0)),
                  full((CONV_WIDTH, CONV_DIM)), full((1, CONV_DIM))],
        out_specs=[pl.BlockSpec((rows, CONV_DIM), lambda i: (i, 0)),
                   pl.BlockSpec((CONV_HALO, CONV_S_SEQS, CONV_DIM), lambda i: (0, i, 0))],
        out_shape=[jax.ShapeDtypeStruct((batch * seq, CONV_DIM), F32),
                   jax.ShapeDtypeStruct((CONV_HALO, batch, CONV_DIM), F32)],
        scratch_shapes=[pltpu.VMEM((CONV_S_SEQS * (SUBLANES + seq), CONV_DIM), F32)],
        compiler_params=_cparams(1),
        name="conv_sample",
    )(u, u, sc_t, conv_w, conv_b)


def _ssd_core(rows, same_seq, zs_ref, dt_ref, dtb_ref, alog_ref, dskip_ref, nw_ref, xs_ref, bc_ref, xd_ref, y_ref,
              yoff_fn, state_fn, g_ref):
    li = lax.broadcasted_iota(jnp.int32, (rows, rows), 0)
    si = lax.broadcasted_iota(jnp.int32, (rows, rows), 1)
    causal = same_seq & (si <= li)
    causal_bias = jnp.where(causal, 0.0, -jnp.inf)
    lane = lax.broadcasted_iota(jnp.int32, (rows, LANES), 1)
    first_half = lane < SSM_HEAD_DIM

    dt = jax.nn.softplus(dt_ref[...] + dtb_ref[...])
    a_neg = -jnp.exp(alog_ref[...])
    d_a = dt * a_neg
    acs = jnp.dot(causal.astype(F32), d_a, precision=HIGHEST, preferred_element_type=F32)
    a_end = jnp.dot(same_seq.astype(F32), d_a, precision=HIGHEST, preferred_element_type=F32)
    log_dt = jnp.log(dt)
    acs2 = acs * LOG2E
    seg_rows2 = ((acs - log_dt) * LOG2E).T
    log_w_state = log_dt + a_end - acs
    log_w_state2 = log_w_state * LOG2E
    cd_rows = jnp.exp(a_end)
    e_acs_tab, w_state_tab = jnp.exp(acs), jnp.exp(log_w_state)

    def head_cols(m, j):
        return (jnp.broadcast_to(m[:, 2 * j:2 * j + 1], (rows, LANES)),
                jnp.broadcast_to(m[:, 2 * j + 1:2 * j + 2], (rows, LANES)))

    pairs_per_group = SSM_HEADS // SSM_GROUPS // 2
    for g in range(SSM_GROUPS):
        b_g = bc_ref[:, g * SSM_STATE:(g + 1) * SSM_STATE].astype(BF16)
        c_g = bc_ref[:, GN + g * SSM_STATE:GN + (g + 1) * SSM_STATE].astype(BF16)
        cb = _dot_nt(c_g, b_g)
        yoff = yoff_fn(g, c_g)
        for jj in range(pairs_per_group):
            j = g * pairs_per_group + jj
            ps = slice(j * LANES, (j + 1) * LANES)
            xs_pair = xs_ref[:, ps]
            xs_bf16 = xs_pair.astype(BF16)
            acs_cols = head_cols(acs2, j)
            parts = []
            for hh in range(2):
                h = 2 * j + hh
                col = acs_cols[hh] if rows == LANES else acs2[:, h:h + 1]
                seg2 = col - seg_rows2[h:h + 1, :]
                decay_dt = jnp.exp2(seg2 + causal_bias)
                m = (cb * decay_dt).astype(BF16)
                parts.append(jnp.dot(m, xs_bf16, preferred_element_type=F32))
            yd = jnp.where(first_half, parts[0], parts[1])
            if rows == LANES:
                e_acs = jnp.exp2(jnp.where(first_half, *acs_cols))
                w_state = jnp.exp2(jnp.where(first_half, *head_cols(log_w_state2, j)))
            else:
                e_acs = jnp.where(first_half, *head_cols(e_acs_tab, j))
                w_state = jnp.where(first_half, *head_cols(w_state_tab, j))
            xd_ref[:, ps] = (xs_pair * w_state).astype(xd_ref.dtype)
            y_ref[:, ps] = yd + yoff[:, jj * LANES:(jj + 1) * LANES] * e_acs + dskip_ref[:, ps] * xs_pair
        state_fn(g, cd_rows)

    for g in range(SSM_GROUPS):
        gs = slice(g * GROUP_W, (g + 1) * GROUP_W)
        gg = y_ref[:, gs] * zs_ref[:, gs]
        ms = jnp.mean(gg * gg, axis=-1, keepdims=True)
        g_ref[:, gs] = (gg * lax.rsqrt(ms + RMS_EPS) * nw_ref[:, gs]).astype(BF16)


def _ssd_prompt_kernel(zs_ref, xs_raw_ref, bc_raw_ref, dt_ref, cw_ref, cb_ref, dtb_ref, alog_ref, dskip_ref, nw_ref,
                       g_ref, st_ref, ext_ref, xc_ref, xd_ref, y_ref, stt_ref):
    c = pl.program_id(1)
    rows = SSD_CHUNK

    @pl.when(c == 0)
    def _():
        ext_ref[0:SUBLANES, :] = jnp.zeros((SUBLANES, CONV_DIM), F32)
        stt_ref[...] = jnp.zeros(stt_ref.shape, F32)

    @pl.when(c != 0)
    def _():
        ext_ref[0:SUBLANES, :] = ext_ref[rows:rows + SUBLANES, :]

    ext_ref[SUBLANES:SUBLANES + rows, 0:D_INNER] = xs_raw_ref[...]
    ext_ref[SUBLANES:SUBLANES + rows, D_INNER:CONV_DIM] = bc_raw_ref[...]
    _conv_silu(ext_ref, SUBLANES, rows, cw_ref, cb_ref, xc_ref, 0)
    xs_ref = xc_ref.at[:, 0:D_INNER]
    bc_ref = xc_ref.at[:, D_INNER:CONV_DIM]

    def yoff_fn(g, c_g):
        return jnp.dot(c_g, stt_ref[:, g * GROUP_W:(g + 1) * GROUP_W].astype(BF16), preferred_element_type=F32)

    def state_fn(g, cd_rows):
        b_g = bc_ref[:, g * SSM_STATE:(g + 1) * SSM_STATE].astype(BF16)
        xd_g = xd_ref[:, g * GROUP_W:(g + 1) * GROUP_W]
        new = lax.dot_general(b_g, xd_g, (((0,), (0,)), ((), ())), preferred_element_type=F32)
        lane = lax.broadcasted_iota(jnp.int32, (SSM_STATE, LANES), 1)
        for jj in range(GROUP_W // LANES):
            j = g * (GROUP_W // LANES) + jj
            ps = slice(j * LANES, (j + 1) * LANES)
            cd = jnp.where(lane < SSM_HEAD_DIM,
                           jnp.broadcast_to(cd_rows[0:1, 2 * j:2 * j + 1], (SSM_STATE, LANES)),
                           jnp.broadcast_to(cd_rows[0:1, 2 * j + 1:2 * j + 2], (SSM_STATE, LANES)))
            stt_ref[:, ps] = stt_ref[:, ps] * cd + new[:, jj * LANES:(jj + 1) * LANES]

    same_seq = jnp.full((rows, rows), True)
    _ssd_core(rows, same_seq, zs_ref, dt_ref, dtb_ref, alog_ref, dskip_ref, nw_ref, xs_ref, bc_ref, xd_ref, y_ref,
              yoff_fn, state_fn, g_ref)

    @pl.when(c == pl.num_programs(1) - 1)
    def _():
        for j in range(D_INNER // LANES):
            st_ref[0, j * LANES:(j + 1) * LANES, :] = stt_ref[:, j * LANES:(j + 1) * LANES].T


def _ssd_prompt(u, dt_raw, conv_w, conv_b, dt_bias, a_log, dskip_cols, ssm_norm_w, batch, seq):
    nc = seq // SSD_CHUNK
    rows = SSD_CHUNK
    row_blk = lambda b, c: b * nc + c
    full = lambda shape: pl.BlockSpec(shape, lambda b, c: (0,) * len(shape))
    return pl.pallas_call(
        _ssd_prompt_kernel,
        grid=(batch, nc),
        in_specs=[pl.BlockSpec((rows, D_INNER), lambda b, c: (row_blk(b, c), _COL.Z // D_INNER)),
                  pl.BlockSpec((rows, D_INNER), lambda b, c: (row_blk(b, c), _COL.XS // D_INNER)),
                  pl.BlockSpec((rows, 2 * GN), lambda b, c: (row_blk(b, c), _COL.BC // (2 * GN))),
                  pl.BlockSpec((rows, DT_PAD), lambda b, c: (row_blk(b, c), 0)),
                  full((CONV_WIDTH, CONV_DIM)), full((1, CONV_DIM)), full((1, DT_PAD)), full((1, DT_PAD)),
                  full((1, D_INNER)), full((1, D_INNER))],
        out_specs=[pl.BlockSpec((rows, D_INNER), lambda b, c: (row_blk(b, c), 0)),
                   pl.BlockSpec((1, D_INNER, SSM_STATE), lambda b, c: (b, 0, 0))],
        out_shape=[jax.ShapeDtypeStruct((batch * seq, D_INNER), BF16),
                   jax.ShapeDtypeStruct((batch, D_INNER, SSM_STATE), F32)],
        scratch_shapes=[pltpu.VMEM((rows + SUBLANES, CONV_DIM), F32),
                        pltpu.VMEM((rows, CONV_DIM), F32),
                        pltpu.VMEM((rows, D_INNER), BF16),
                        pltpu.VMEM((rows, D_INNER), F32),
                        pltpu.VMEM((SSM_STATE, D_INNER), F32)],
        compiler_params=_cparams(2),
        name="ssd_prompt",
    )(u, u, u, dt_raw, conv_w, conv_b, dt_bias, a_log, dskip_cols, ssm_norm_w)


SAMPLE_BT = 4


def _ssd_sample_kernel(zs_ref, xs_ref, bc_ref, dt_ref, st_ref, dtb_ref, alog_ref, dskip_ref, nw_ref,
                       g_ref, nst_ref, xd_ref, y_ref, yoff_ref, seq_len):
    bt = SAMPLE_BT
    rows = bt * seq_len
    heads_per_group = SSM_HEADS // SSM_GROUPS

    for b in range(bt):
        rs = slice(b * seq_len, (b + 1) * seq_len)
        for g in range(SSM_GROUPS):
            c_bg = bc_ref[rs, GN + g * SSM_STATE:GN + (g + 1) * SSM_STATE].astype(BF16)
            s_bg = st_ref[b, g * GROUP_W:(g + 1) * GROUP_W, :].astype(BF16)
            yoff_ref[rs, g * GROUP_W:(g + 1) * GROUP_W] = _dot_nt(c_bg, s_bg)

    def yoff_fn(g, c_g):
        return yoff_ref[:, g * GROUP_W:(g + 1) * GROUP_W]

    def state_fn(g, cd_rows):
        for b in range(bt):
            rs = slice(b * seq_len, (b + 1) * seq_len)
            b_bg = bc_ref[rs, g * SSM_STATE:(g + 1) * SSM_STATE].astype(BF16)
            xd_bg = xd_ref[rs, g * GROUP_W:(g + 1) * GROUP_W].astype(BF16)
            new = lax.dot_general(xd_bg, b_bg, (((0,), (0,)), ((), ())), preferred_element_type=F32)
            for hh in range(heads_per_group):
                h = g * heads_per_group + hh
                hs = slice(h * SSM_HEAD_DIM, (h + 1) * SSM_HEAD_DIM)
                cd = jnp.broadcast_to(cd_rows[b * seq_len:b * seq_len + 1, h:h + 1], (SSM_HEAD_DIM, SSM_STATE))
                nst_ref[b, hs, :] = st_ref[b, hs, :] * cd + new[hh * SSM_HEAD_DIM:(hh + 1) * SSM_HEAD_DIM, :]

    li = lax.broadcasted_iota(jnp.int32, (rows, rows), 0)
    si = lax.broadcasted_iota(jnp.int32, (rows, rows), 1)
    same_seq = (li // seq_len) == (si // seq_len)
    _ssd_core(rows, same_seq, zs_ref, dt_ref, dtb_ref, alog_ref, dskip_ref, nw_ref, xs_ref, bc_ref, xd_ref, y_ref,
              yoff_fn, state_fn, g_ref)


def _ssd_sample(u, dt_raw, xc, state, dt_bias, a_log, dskip_cols, ssm_norm_w, row0, batch, seq):
    bt = SAMPLE_BT
    rows = bt * seq
    assert row0 % rows == 0
    blk0 = row0 // rows
    full = lambda shape: pl.BlockSpec(shape, lambda i: (0,) * len(shape))
    return pl.pallas_call(
        functools.partial(_ssd_sample_kernel, seq_len=seq),
        grid=(batch // bt,),
        in_specs=[pl.BlockSpec((rows, D_INNER), lambda i: (blk0 + i, _COL.Z // D_INNER)),
                  pl.BlockSpec((rows, D_INNER), lambda i: (i, 0)),
                  pl.BlockSpec((rows, 2 * GN), lambda i: (i, D_INNER // (2 * GN))),
                  pl.BlockSpec((rows, DT_PAD), lambda i: (blk0 + i, 0)),
                  pl.BlockSpec((bt, D_INNER, SSM_STATE), lambda i: (i, 0, 0)),
                  full((1, DT_PAD)), full((1, DT_PAD)), full((1, D_INNER)), full((1, D_INNER))],
        out_specs=[pl.BlockSpec((rows, D_INNER), lambda i: (i, 0)),
                   pl.BlockSpec((bt, D_INNER, SSM_STATE), lambda i: (i, 0, 0))],
        out_shape=[jax.ShapeDtypeStruct((batch * seq, D_INNER), BF16),
                   jax.ShapeDtypeStruct((batch, D_INNER, SSM_STATE), F32)],
        scratch_shapes=[pltpu.VMEM((rows, D_INNER), F32),
                        pltpu.VMEM((rows, D_INNER), F32),
                        pltpu.VMEM((rows, D_INNER), F32)],
        compiler_params=_cparams(1),
        name="ssd_sample",
    )(u, xc, xc, dt_raw, state, dt_bias, a_log, dskip_cols, ssm_norm_w)


ATTN_BLOCKS = 2
ONES_ROWS = 16


def _attn_prompt_kernel(sink_ref, q_ref, kp_ref, ko_ref, vp_ref, vo_ref, ga_ref, bias_ref, o_ref):
    n = pl.program_id(1)
    no_prev = [jnp.where(n == 0, jnp.inf, 0.0)] + [0.0] * (ATTN_BLOCKS - 1)

    def rows(s):
        return slice(s * WINDOW, (s + 1) * WINDOW)

    def scores(s, g):
        ks = slice(g * HEAD_DIM, (g + 1) * HEAD_DIM)
        k_prev = kp_ref[:, ks] if s == 0 else ko_ref[rows(s - 1), ks]
        v_prev = vp_ref[:, ks] if s == 0 else vo_ref[rows(s - 1), ks]
        kk = jnp.concatenate([k_prev, ko_ref[rows(s), ks]], axis=0).astype(BF16)
        vv_t = jnp.concatenate([jnp.concatenate([v_prev, vo_ref[rows(s), ks]], axis=0).T,
                                jnp.ones((ONES_ROWS, 2 * WINDOW), F32)], axis=0).astype(BF16)
        s_list = []
        for r in range(REP):
            h = g * REP + r
            q = q_ref[rows(s), h * HEAD_DIM:(h + 1) * HEAD_DIM].astype(BF16)
            s_list.append(_dot_nt(kk, q))
        return vv_t, s_list

    def finish(s, g, vv_t, s_list):
        probs = []
        for r in range(REP):
            h = g * REP + r
            s_prev = s_list[r][0:WINDOW] + bias_ref[h, 0:WINDOW, :]
            s_own = s_list[r][WINDOW:2 * WINDOW] + bias_ref[h, WINDOW:2 * WINDOW, :]
            sink = sink_ref[h]
            m = jnp.maximum(jnp.maximum(jnp.max(s_prev, axis=0, keepdims=True) - no_prev[s],
                                        jnp.max(s_own, axis=0, keepdims=True)), sink)
            p = jnp.concatenate([jnp.exp(s_prev - (m + no_prev[s])), jnp.exp(s_own - m)], axis=0)
            probs.append((p.astype(BF16), jnp.exp(sink - m)))
        o_t = []
        for p, p_sink in probs:
            pv = jnp.dot(vv_t, p, preferred_element_type=F32)
            o_t.append(pv[0:HEAD_DIM] / (pv[HEAD_DIM:HEAD_DIM + 1] + p_sink))
        for rp in range(REP // 2):
            o_pair = jnp.concatenate(o_t[2 * rp:2 * rp + 2], axis=0).T
            ps = slice((g * REP + 2 * rp) * HEAD_DIM, (g * REP + 2 * rp + 2) * HEAD_DIM)
            o_ref[rows(s), ps] = (o_pair * ga_ref[rows(s), ps]).astype(BF16)

    units = [(s, g) for s in range(ATTN_BLOCKS) for g in range(KV_HEADS)]
    pending = scores(*units[0])
    for idx, unit in enumerate(units):
        cur = pending
        if idx + 1 < len(units):
            pending = scores(*units[idx + 1])
        finish(*unit, *cur)


def _attn_prompt(u, sinks, bias, batch, seq):
    rows = ATTN_BLOCKS * WINDOW
    assert seq % rows == 0
    ns = seq // rows
    row = lambda b, n: b * ns + n
    prev = lambda b, n: jnp.maximum((b * ns + n) * ATTN_BLOCKS - 1, 0)
    return pl.pallas_call(
        _attn_prompt_kernel,
        grid=(batch, ns),
        in_specs=[pl.BlockSpec(memory_space=pltpu.SMEM),
                  pl.BlockSpec((rows, ATTN_WIDTH), lambda b, n: (row(b, n), _COL.Q // ATTN_WIDTH)),
                  pl.BlockSpec((WINDOW, KV_WIDTH), lambda b, n: (prev(b, n), _COL.K // KV_WIDTH)),
                  pl.BlockSpec((rows, KV_WIDTH), lambda b, n: (row(b, n), _COL.K // KV_WIDTH)),
                  pl.BlockSpec((WINDOW, KV_WIDTH), lambda b, n: (prev(b, n), _COL.V // KV_WIDTH)),
                  pl.BlockSpec((rows, KV_WIDTH), lambda b, n: (row(b, n), _COL.V // KV_WIDTH)),
                  pl.BlockSpec((rows, ATTN_WIDTH), lambda b, n: (row(b, n), _COL.GA // ATTN_WIDTH)),
                  pl.BlockSpec((Q_HEADS, 2 * WINDOW, WINDOW), lambda b, n: (0, 0, 0))],
        out_specs=pl.BlockSpec((rows, ATTN_WIDTH), lambda b, n: (row(b, n), 0)),
        out_shape=jax.ShapeDtypeStruct((batch * seq, ATTN_WIDTH), BF16),
        compiler_params=_cparams(2),
        name="attn_prompt",
    )(sinks, u, u, u, u, u, u, bias)


def _attn_sample_kernel(q_ref, ga_ref, kvt_ref, ck_ref, cv_ref, bias_ref, o_ref, nk_ref, nv_ref,
                        s_ref, p_ref, seq_len):
    bt = SAMPLE_BT
    w_buf = ck_ref.shape[-1]
    keep = w_buf - seq_len
    unit_rows = REP * seq_len
    seq_rows = KV_HEADS * unit_rows
    lane0 = (pl.program_id(0) * bt * seq_len) % kvt_ref.shape[1]
    units = [(b, g) for b in range(bt) for g in range(KV_HEADS)]

    lane = lax.broadcasted_iota(jnp.int32, (KV_WIDTH, w_buf), 1)
    for c_ref, n_ref, row0 in ((ck_ref, nk_ref, 0), (cv_ref, nv_ref, KV_WIDTH)):
        fresh = kvt_ref[row0:row0 + KV_WIDTH, :]
        for b in range(bt):
            new_shift = (keep + w_buf - (lane0 + b * seq_len)) % w_buf
            old = c_ref[b].reshape(KV_WIDTH, w_buf)
            new = jnp.where(lane < keep, pltpu.roll(old, keep, axis=1), pltpu.roll(fresh, new_shift, axis=1))
            n_ref[b] = new.reshape(KV_HEADS, HEAD_DIM, w_buf)

    def both(c_ref, n_ref, b, g):
        return jnp.concatenate([c_ref[b, g], n_ref[b, g]], axis=1).astype(BF16)

    for u, (b, g) in enumerate(units):
        rs = slice(b * seq_len, (b + 1) * seq_len)
        qs = jnp.concatenate([q_ref[rs, (g * REP + r) * HEAD_DIM:(g * REP + r + 1) * HEAD_DIM] for r in range(REP)],
                             axis=0)
        s_ref[u * unit_rows:(u + 1) * unit_rows, :] = jnp.dot(
            qs.astype(BF16), both(ck_ref, nk_ref, b, g), preferred_element_type=F32)

    col = lax.broadcasted_iota(jnp.int32, (seq_rows, 2 * w_buf), 1)
    not_sink = (col != seq_len).astype(F32)
    for b in range(bt):
        rows_b = slice(b * seq_rows, (b + 1) * seq_rows)
        s = s_ref[rows_b, :] * not_sink + bias_ref[...]
        m = jnp.max(s, axis=-1, keepdims=True)
        p = jnp.exp(s - m)
        inv = 1.0 / jnp.sum(p, axis=-1, keepdims=True)
        p_ref[rows_b, :] = (p * inv * not_sink).astype(BF16)

    for u, (b, g) in enumerate(units):
        o = _dot_nt(p_ref[u * unit_rows:(u + 1) * unit_rows, :], both(cv_ref, nv_ref, b, g))
        rs = slice(b * seq_len, (b + 1) * seq_len)
        for r in range(REP):
            hs = slice((g * REP + r) * HEAD_DIM, (g * REP + r + 1) * HEAD_DIM)
            o_ref[rs, hs] = (o[r * seq_len:(r + 1) * seq_len, :] * ga_ref[rs, hs]).astype(BF16)


def _attn_sample(u, kv_t, cache_k, cache_v, bias, row0, batch, seq):
    bt = SAMPLE_BT
    rows = bt * seq
    w_buf = cache_k.shape[-1]
    assert w_buf == LANES and row0 % rows == 0 and LANES % rows == 0
    blk0 = row0 // rows
    score_rows = bt * Q_HEADS * seq
    cache_spec = pl.BlockSpec((bt, KV_HEADS, HEAD_DIM, w_buf), lambda i: (i, 0, 0, 0))
    return pl.pallas_call(
        functools.partial(_attn_sample_kernel, seq_len=seq),
        grid=(batch // bt,),
        in_specs=[pl.BlockSpec((rows, ATTN_WIDTH), lambda i: (blk0 + i, _COL.Q // ATTN_WIDTH)),
                  pl.BlockSpec((rows, ATTN_WIDTH), lambda i: (blk0 + i, _COL.GA // ATTN_WIDTH)),
                  pl.BlockSpec((2 * KV_WIDTH, LANES), lambda i: (0, i * rows // LANES)),
                  cache_spec, cache_spec,
                  pl.BlockSpec(bias.shape, lambda i: (0, 0))],
        out_specs=[pl.BlockSpec((rows, ATTN_WIDTH), lambda i: (i, 0)), cache_spec, cache_spec],
        out_shape=[jax.ShapeDtypeStruct((batch * seq, ATTN_WIDTH), BF16),
                   jax.ShapeDtypeStruct(cache_k.shape, F32),
                   jax.ShapeDtypeStruct(cache_v.shape, F32)],
        scratch_shapes=[pltpu.VMEM((score_rows, 2 * w_buf), F32),
                        pltpu.VMEM((score_rows, 2 * w_buf), BF16)],
        compiler_params=_cparams(1),
        name="attn_sample",
    )(u, u, kv_t, cache_k, cache_v, bias)


MERGE_TN = 512
EPI_SPLIT = 4


def _merge_kernel(g_ref, og_ref, ms_ref, ma_ref, ws_ref, wa_ref, o_ref, *stage_refs):
    rows = o_ref.shape[0] // EPI_SPLIT
    ssm_refs, attn_refs = stage_refs[:EPI_SPLIT], stage_refs[EPI_SPLIT:]

    def gate_chunk(n):
        rs = slice(n * rows, (n + 1) * rows)
        o_ref[rs, :] = (_sigmoid(ms_ref[rs, :]) * ssm_refs[n][...]
                        + _sigmoid(ma_ref[rs, :]) * attn_refs[n][...]).astype(BF16)

    for n in range(EPI_SPLIT):
        rs = slice(n * rows, (n + 1) * rows)
        ssm_refs[n][...] = jnp.dot(g_ref[rs, :], ws_ref[...], preferred_element_type=F32)
        attn_refs[n][...] = jnp.dot(og_ref[rs, :], wa_ref[...], preferred_element_type=F32)
        if n > 0:
            gate_chunk(n - 1)
    gate_chunk(EPI_SPLIT - 1)


def _merge(g, og, u, w_ssm, w_attn, row0, tm):
    t = g.shape[0]
    tn = MERGE_TN
    assert row0 % tm == 0
    blk0 = row0 // tm
    return pl.pallas_call(
        _merge_kernel,
        grid=(t // tm, D_MODEL // tn),
        in_specs=[pl.BlockSpec((tm, D_INNER), lambda i, j: (i, 0)),
                  pl.BlockSpec((tm, ATTN_WIDTH), lambda i, j: (i, 0)),
                  pl.BlockSpec((tm, tn), lambda i, j: (blk0 + i, _COL.MS // tn + j)),
                  pl.BlockSpec((tm, tn), lambda i, j: (blk0 + i, _COL.MA // tn + j)),
                  pl.BlockSpec((D_INNER, tn), lambda i, j: (0, j)),
                  pl.BlockSpec((ATTN_WIDTH, tn), lambda i, j: (0, j))],
        out_specs=pl.BlockSpec((tm, tn), lambda i, j: (i, j)),
        out_shape=jax.ShapeDtypeStruct((t, D_MODEL), BF16),
        scratch_shapes=[pltpu.VMEM((tm // EPI_SPLIT, tn), F32) for _ in range(2 * EPI_SPLIT)],
        compiler_params=_cparams(2, BIG_VMEM_LIMIT),
        name="merge",
    )(g, og, u, u, w_ssm, w_attn)


def _out_proj_kernel(m_ref, x_ref, wo_ref, fw_ref, o_ref):
    xn = x_ref[...] + jnp.dot(m_ref[...], wo_ref[...], preferred_element_type=F32)
    ms = jnp.mean(xn * xn, axis=-1, keepdims=True)
    o_ref[...] = xn * lax.rsqrt(ms + RMS_EPS) * fw_ref[...]


def _out_proj(m, x2d, w_out, final_w, tm):
    t = m.shape[0]
    return pl.pallas_call(
        _out_proj_kernel,
        grid=(t // tm,),
        in_specs=[pl.BlockSpec((tm, D_MODEL), lambda i: (i, 0)),
                  pl.BlockSpec((tm, D_MODEL), lambda i: (i, 0)),
                  pl.BlockSpec((D_MODEL, D_MODEL), lambda i: (0, 0)),
                  pl.BlockSpec((1, D_MODEL), lambda i: (0, 0))],
        out_specs=pl.BlockSpec((tm, D_MODEL), lambda i: (i, 0)),
        out_shape=jax.ShapeDtypeStruct((t, D_MODEL), F32),
        compiler_params=_cparams(1),
        name="out_proj",
    )(m, x2d, w_out, final_w)


def _pad_lanes(v):
    return jnp.pad(v.reshape(1, -1), ((0, 0), (0, DT_PAD - v.shape[-1])))


def kernel(x_prompt, x_sample, cache_k, cache_v, state_conv, state_ssm, norm_w, w_in, conv_w, conv_b, dt_bias, a_log,
           d_skip, ssm_norm_w, w_ssm_branch, attn_sinks, w_attn_branch, w_out, rel_bias, final_norm_w):
    assert w_in.shape[0] == 1, "single-layer kernel"
    batch, seq, _ = x_prompt.shape
    dec_batch, dec_seq, _ = x_sample.shape
    w_buf = cache_k.shape[2]
    t_p, t_s = batch * seq, dec_batch * dec_seq
    assert seq % SSD_CHUNK == 0 and seq % WINDOW == 0 and dec_seq == SUBLANES and w_buf == WINDOW
    assert dec_batch % SAMPLE_BT == 0 and (t_p + t_s) % IN_TM == 0

    w_t = jnp.transpose(w_in[0])
    sc_t = jnp.transpose(state_conv[0], (1, 0, 2))
    ck_t = jnp.transpose(cache_k[0], (0, 2, 3, 1))
    cv_t = jnp.transpose(cache_v[0], (0, 2, 3, 1))

    nw = norm_w[0].reshape(1, D_MODEL)
    cw, cb = conv_w[0], conv_b[0].reshape(1, CONV_DIM)
    dtb, alog = _pad_lanes(dt_bias[0]), _pad_lanes(a_log[0])
    dskip_cols = jnp.repeat(d_skip[0], SSM_HEAD_DIM).reshape(1, D_INNER)
    snw = ssm_norm_w[0].reshape(1, D_INNER)
    w_ssm = w_ssm_branch[0].astype(BF16)
    w_attn = w_attn_branch[0].astype(BF16)
    wo = w_out[0].astype(BF16)
    fw = final_norm_w.reshape(1, D_MODEL)
    sinks = attn_sinks[0]

    xp2 = x_prompt.reshape(t_p, D_MODEL)
    xs2 = x_sample.reshape(t_s, D_MODEL)

    band_bias_t = _band_bias_t(rel_bias)
    dec_bias = _decode_bias(rel_bias, sinks, dec_seq, w_buf)

    h, dt_raw = _norm_dt(xp2, xs2, nw, w_t)
    u = _in_proj(h, w_t)
    kv_t = _kv_t(h, w_t, t_p, t_s)

    g_p, st_p = _ssd_prompt(u, dt_raw, cw, cb, dtb, alog, dskip_cols, snw, batch, seq)
    og_p = _attn_prompt(u, sinks, band_bias_t, batch, seq)
    m_p = _merge(g_p, og_p, u, w_ssm, w_attn, 0, tm=1024)
    y_p = _out_proj(m_p, xp2, wo, fw, tm=512)

    xc_s, nsc_t = _conv_sample(u, sc_t, cw, cb, t_p, dec_batch, dec_seq)
    g_s, st_s = _ssd_sample(u, dt_raw, xc_s, state_ssm[0].reshape(dec_batch, D_INNER, SSM_STATE),
                            dtb, alog, dskip_cols, snw, t_p, dec_batch, dec_seq)
    og_s, nk_t, nv_t = _attn_sample(u, kv_t, ck_t, cv_t, dec_bias, t_p, dec_batch, dec_seq)
    m_s = _merge(g_s, og_s, u, w_ssm, w_attn, t_p, tm=1024)
    y_s = _out_proj(m_s, xs2, wo, fw, tm=512)

    def seq_tail(n_rows, col0, width):
        return jnp.stack([u[(b + 1) * seq - n_rows:(b + 1) * seq, col0:col0 + width] for b in range(batch)])

    k_p = seq_tail(WINDOW, _COL.K, KV_WIDTH).reshape(1, batch, WINDOW, KV_HEADS, HEAD_DIM)
    v_p = seq_tail(WINDOW, _COL.V, KV_WIDTH).reshape(1, batch, WINDOW, KV_HEADS, HEAD_DIM)
    conv_p = seq_tail(CONV_HALO, _COL.XS, CONV_DIM)[None]
    conv_s = jnp.transpose(nsc_t, (1, 0, 2))[None]
    ssm_p = st_p.reshape(1, batch, SSM_HEADS, SSM_HEAD_DIM, SSM_STATE)
    ssm_s = st_s.reshape(1, dec_batch, SSM_HEADS, SSM_HEAD_DIM, SSM_STATE)
    k_s = jnp.transpose(nk_t, (0, 3, 1, 2))[None]
    v_s = jnp.transpose(nv_t, (0, 3, 1, 2))[None]
    return (y_p.reshape(batch, seq, D_MODEL), y_s.reshape(dec_batch, dec_seq, D_MODEL),
            k_p, v_p, conv_p, ssm_p, k_s, v_s, conv_s, ssm_s)
```

```python
import functools
import math

import numpy as np
import jax
import jax.numpy as jnp
from jax import lax
from jax.experimental import pallas as pl
from jax.experimental.pallas import tpu as pltpu

F32 = jnp.float32
BF16 = jnp.bfloat16
HIGHEST = lax.Precision.HIGHEST

D_MODEL = 2048
D_INNER = 4096
SSM_HEAD_DIM = 64
SSM_HEADS = 64
SSM_GROUPS = 8
SSM_STATE = 128
CONV_WIDTH = 4
CONV_HALO = CONV_WIDTH - 1
GN = SSM_GROUPS * SSM_STATE
CONV_DIM = D_INNER + 2 * GN
SSD_CHUNK = 128
GROUP_W = D_INNER // SSM_GROUPS
HEAD_DIM = 64
Q_HEADS = 32
KV_HEADS = 8
REP = Q_HEADS // KV_HEADS
ATTN_WIDTH = Q_HEADS * HEAD_DIM
KV_WIDTH = KV_HEADS * HEAD_DIM
WINDOW = 128
N_BUCKETS = 32
MAX_EXACT = N_BUCKETS // 2
RMS_EPS = 1e-6
ATTN_SCALE = HEAD_DIM ** -0.5

_OFF_Z = 0
_OFF_XBC = _OFF_Z + D_INNER
_OFF_DT = _OFF_XBC + CONV_DIM
_OFF_Q = _OFF_DT + SSM_HEADS
_OFF_K = _OFF_Q + ATTN_WIDTH
_OFF_V = _OFF_K + KV_WIDTH
_OFF_GA = _OFF_V + KV_WIDTH
_OFF_MS = _OFF_GA + ATTN_WIDTH
_OFF_MA = _OFF_MS + D_MODEL
_IN_COLS = _OFF_MA + D_MODEL

LANES = 128
SUBLANES = 8
DT_PAD = LANES


class _COL:
    Z = 0
    XS = Z + D_INNER
    BC = XS + D_INNER
    Q = BC + 2 * GN
    GA = Q + ATTN_WIDTH
    MS = GA + ATTN_WIDTH
    MA = MS + D_MODEL
    K = MA + D_MODEL
    V = K + KV_WIDTH
    TOTAL = V + KV_WIDTH


VMEM_LIMIT = 52 * 1024 * 1024
BIG_VMEM_LIMIT = 57 * 1024 * 1024


def _cparams(n_grid, limit=VMEM_LIMIT):
    return pltpu.CompilerParams(dimension_semantics=("arbitrary",) * n_grid, vmem_limit_bytes=limit)


LOG2E = math.log2(math.e)
NEG_LOG2E = -LOG2E


def _sigmoid(v):
    return 1.0 / (1.0 + jnp.exp2(v * NEG_LOG2E))


def _silu(v):
    return v * _sigmoid(v)


def _dot_nt(a, b):
    return lax.dot_general(a, b, (((1,), (1,)), ((), ())), preferred_element_type=F32)


NORM_TM = 1024


def _norm_dt_kernel(xp_ref, xs_ref, nw_ref, wdt_ref, h_ref, dt_ref, n_prompt_tiles):
    i = pl.program_id(0)

    def emit(x_ref):
        xf = x_ref[...]
        ms = jnp.mean(xf * xf, axis=-1, keepdims=True)
        h = (xf * lax.rsqrt(ms + RMS_EPS) * nw_ref[...]).astype(BF16)
        h_ref[...] = h
        dt_ref[...] = _dot_nt(h, wdt_ref[...].astype(BF16))

    @pl.when(i < n_prompt_tiles)
    def _():
        emit(xp_ref)

    @pl.when(i >= n_prompt_tiles)
    def _():
        emit(xs_ref)


def _norm_dt(xp2, xs2, norm_w, w_t):
    tp, ts = xp2.shape[0], xs2.shape[0]
    n_p, n_s = tp // NORM_TM, ts // NORM_TM
    return pl.pallas_call(
        functools.partial(_norm_dt_kernel, n_prompt_tiles=n_p),
        grid=(n_p + n_s,),
        in_specs=[pl.BlockSpec((NORM_TM, D_MODEL), lambda i: (jnp.minimum(i, n_p - 1), 0)),
                  pl.BlockSpec((NORM_TM, D_MODEL), lambda i: (jnp.maximum(i - n_p, 0), 0)),
                  pl.BlockSpec((1, D_MODEL), lambda i: (0, 0)),
                  pl.BlockSpec((DT_PAD, D_MODEL), lambda i: (_OFF_DT // DT_PAD, 0))],
        out_specs=[pl.BlockSpec((NORM_TM, D_MODEL), lambda i: (i, 0)),
                   pl.BlockSpec((NORM_TM, DT_PAD), lambda i: (i, 0))],
        out_shape=[jax.ShapeDtypeStruct((tp + ts, D_MODEL), BF16),
                   jax.ShapeDtypeStruct((tp + ts, DT_PAD), F32)],
        compiler_params=_cparams(1),
        name="norm_dt",
    )(xp2, xs2, norm_w, w_t)


IN_TM = 1536
IN_TN = 1024
IN_CHUNK = 512
assert _COL.TOTAL % IN_TN == 0 and IN_TN == 2 * IN_CHUNK and _OFF_DT % IN_CHUNK == 0


def _src_row(c):
    straight = _OFF_DT // IN_CHUNK
    jj = c - straight
    q_t, kv_t = ATTN_WIDTH // IN_CHUNK, 2 * KV_WIDTH // IN_CHUNK
    gate_t = (ATTN_WIDTH + 2 * D_MODEL) // IN_CHUNK
    shifted = jnp.where(jj < q_t, jj, jnp.where(jj < q_t + gate_t, jj + kv_t, jj - gate_t))
    return pl.multiple_of(jnp.where(c < straight, c * IN_CHUNK, _OFF_Q + shifted * IN_CHUNK), SSM_HEADS)


IN_NSPLIT = 4
IN_MW = IN_TM // IN_NSPLIT
J_Z1 = _COL.XS // IN_TN
J_Q0, J_Q1 = _COL.Q // IN_TN, _COL.GA // IN_TN
J_GA0, J_GA1 = _COL.GA // IN_TN, _COL.MS // IN_TN
assert _COL.XS % IN_TN == 0 and _COL.Q % IN_TN == 0 and _COL.GA % IN_TN == 0 and _COL.MS % IN_TN == 0
assert math.log2(HEAD_DIM) % 2 == 0, "the folded attention scale must be a power of two"


def _in_proj_kernel(h_ref, w0_ref, w1_ref, o_ref, wbf_ref, *stage_refs):
    j = pl.program_id(0)

    @pl.when(pl.program_id(1) == 0)
    def _():
        scale = jnp.where((j >= J_Q0) & (j < J_Q1), ATTN_SCALE, 1.0)
        wbf_ref[0:IN_CHUNK, :] = (w0_ref[...] * scale).astype(BF16)
        wbf_ref[IN_CHUNK:IN_TN, :] = (w1_ref[...] * scale).astype(BF16)

    is_silu_tile = (j < J_Z1) | ((j >= J_GA0) & (j < J_GA1))

    @pl.when(is_silu_tile)
    def _():
        def silu_chunk(n):
            o_ref[n * IN_MW:(n + 1) * IN_MW, :] = _silu(stage_refs[n][...])

        for n in range(IN_NSPLIT):
            stage_refs[n][...] = _dot_nt(h_ref[n * IN_MW:(n + 1) * IN_MW, :], wbf_ref[...])
            if n > 0:
                silu_chunk(n - 1)
        silu_chunk(IN_NSPLIT - 1)

    @pl.when(jnp.logical_not(is_silu_tile))
    def _():
        o_ref[...] = _dot_nt(h_ref[...], wbf_ref[...])


def _in_proj(h, w_t):
    t = h.shape[0]
    return pl.pallas_call(
        _in_proj_kernel,
        grid=(_COL.TOTAL // IN_TN, t // IN_TM),
        in_specs=[pl.BlockSpec((IN_TM, D_MODEL), lambda j, i: (i, 0)),
                  pl.BlockSpec((pl.Element(IN_CHUNK), pl.Element(D_MODEL)), lambda j, i: (_src_row(2 * j), 0)),
                  pl.BlockSpec((pl.Element(IN_CHUNK), pl.Element(D_MODEL)), lambda j, i: (_src_row(2 * j + 1), 0))],
        out_specs=pl.BlockSpec((IN_TM, IN_TN), lambda j, i: (i, j)),
        out_shape=jax.ShapeDtypeStruct((t, _COL.TOTAL), F32),
        scratch_shapes=[pltpu.VMEM((IN_TN, D_MODEL), BF16)]
        + [pltpu.VMEM((IN_MW, IN_TN), F32) for _ in range(IN_NSPLIT)],
        compiler_params=_cparams(2, BIG_VMEM_LIMIT),
        name="in_proj",
    )(h, w_t, w_t)


def _kv_t_kernel(w_ref, h_ref, o_ref):
    o_ref[...] = _dot_nt(w_ref[...].astype(BF16), h_ref[...])


def _kv_t(h, w_t, row0, rows):
    assert row0 % rows == 0
    return pl.pallas_call(
        _kv_t_kernel,
        grid=(1,),
        in_specs=[pl.BlockSpec((pl.Element(2 * KV_WIDTH), pl.Element(D_MODEL)), lambda i: (_OFF_K, 0)),
                  pl.BlockSpec((rows, D_MODEL), lambda i: (row0 // rows, 0))],
        out_specs=pl.BlockSpec((2 * KV_WIDTH, rows), lambda i: (0, 0)),
        out_shape=jax.ShapeDtypeStruct((2 * KV_WIDTH, rows), F32),
        compiler_params=_cparams(1),
        name="kv_t",
    )(w_t, h)


def _bucket_of_dist(dist):
    n = np.maximum(dist, 0)
    nf = np.maximum(n, 1).astype(np.float32)
    large = MAX_EXACT + (np.log(nf / MAX_EXACT) / math.log(WINDOW / MAX_EXACT)
                         * (N_BUCKETS - MAX_EXACT)).astype(np.int32)
    large = np.minimum(large, N_BUCKETS - 1)
    bucket = np.where(n < MAX_EXACT, n, large)
    return np.where((dist >= 0) & (dist <= WINDOW), bucket, -1).astype(np.int32)


def _bias_of_bucket(bk, table_ref, h):
    acc = jnp.full(bk.shape, -jnp.inf, F32)
    for b in range(N_BUCKETS):
        acc = jnp.where(bk == b, table_ref[b, h], acc)
    return acc


BAND_BIAS_HEADS = 8


def _band_bias_kernel(table_ref, bucket_ref, o_ref):
    for hh in range(BAND_BIAS_HEADS):
        o_ref[hh] = _bias_of_bucket(bucket_ref[...], table_ref, pl.program_id(0) * BAND_BIAS_HEADS + hh)


def _band_bias_t(rel_bias):
    kj = np.arange(2 * WINDOW)[:, None]
    qi = np.arange(WINDOW)[None, :]
    bucket = jnp.asarray(_bucket_of_dist(qi + WINDOW - kj))
    return pl.pallas_call(
        _band_bias_kernel,
        grid=(Q_HEADS // BAND_BIAS_HEADS,),
        in_specs=[pl.BlockSpec(memory_space=pltpu.SMEM),
                  pl.BlockSpec((2 * WINDOW, WINDOW), lambda h: (0, 0))],
        out_specs=pl.BlockSpec((BAND_BIAS_HEADS, 2 * WINDOW, WINDOW), lambda h: (h, 0, 0)),
        out_shape=jax.ShapeDtypeStruct((Q_HEADS, 2 * WINDOW, WINDOW), F32),
        compiler_params=_cparams(1),
        name="band_bias",
    )(rel_bias, bucket)


def _decode_bias_kernel(table_ref, sink_ref, bucket_ref, o_ref, seq):
    g = pl.program_id(0)
    sink_lane = lax.broadcasted_iota(jnp.int32, bucket_ref.shape, 1) == seq
    for r in range(REP):
        h = g * REP + r
        bias = _bias_of_bucket(bucket_ref[...], table_ref, h)
        o_ref[r * seq:(r + 1) * seq, :] = jnp.where(sink_lane, sink_ref[h], bias)


def _decode_bias(rel_bias, sinks, seq, w_buf):
    l = np.arange(seq)[:, None]
    j = np.arange(w_buf)[None, :]
    dist_old = np.where((j < seq), l + w_buf - j, -1)
    dist_new = np.where(j < w_buf - seq, l + w_buf - (j + seq), l - (j - (w_buf - seq)))
    bucket = jnp.asarray(np.concatenate([_bucket_of_dist(dist_old), _bucket_of_dist(dist_new)], axis=1))
    return pl.pallas_call(
        functools.partial(_decode_bias_kernel, seq=seq),
        grid=(KV_HEADS,),
        in_specs=[pl.BlockSpec(memory_space=pltpu.SMEM), pl.BlockSpec(memory_space=pltpu.SMEM),
                  pl.BlockSpec(bucket.shape, lambda g: (0, 0))],
        out_specs=pl.BlockSpec((REP * seq, 2 * w_buf), lambda g: (g, 0)),
        out_shape=jax.ShapeDtypeStruct((Q_HEADS * seq, 2 * w_buf), F32),
        compiler_params=_cparams(1),
        name="decode_bias",
    )(rel_bias, sinks, bucket)


CONV_STRIP = 512


def _conv_silu(ext_ref, base, rows, cw_ref, cb_ref, xc_ref, out_row):
    n_tiles = rows // SUBLANES
    sub = lax.broadcasted_iota(jnp.int32, (SUBLANES, CONV_STRIP), 0)
    from_prev = [sub < k for k in range(CONV_WIDTH)]
    for c0 in range(0, CONV_DIM, CONV_STRIP):
        cs = slice(c0, c0 + CONV_STRIP)
        bias = jnp.broadcast_to(cb_ref[:, cs], (SUBLANES, CONV_STRIP))
        taps = [jnp.broadcast_to(cw_ref[CONV_HALO - k:CONV_WIDTH - k, cs], (SUBLANES, CONV_STRIP))
                for k in range(CONV_WIDTH)]
        prev = ext_ref[base - SUBLANES:base, cs]
        prev_rot = [None] + [pltpu.roll(prev, k, axis=0) for k in range(1, CONV_WIDTH)]
        for t in range(n_tiles):
            cur = ext_ref[base + t * SUBLANES:base + (t + 1) * SUBLANES, cs]
            acc = bias + taps[0] * cur
            cur_rot = [None]
            for k in range(1, CONV_WIDTH):
                cur_rot.append(pltpu.roll(cur, k, axis=0))
                acc = acc + taps[k] * jnp.where(from_prev[k], prev_rot[k], cur_rot[k])
            xc_ref[out_row + t * SUBLANES:out_row + (t + 1) * SUBLANES, cs] = _silu(acc)
            prev_rot = cur_rot


CONV_S_SEQS = 16


def _conv_sample_kernel(xs_ref, bc_ref, sc_ref, cw_ref, cb_ref, xc_ref, nsc_ref, ext_ref, seq_len):
    slab = SUBLANES + seq_len
    for b in range(CONV_S_SEQS):
        base = b * slab + SUBLANES
        for k in range(CONV_HALO):
            ext_ref[base - CONV_HALO + k:base - CONV_HALO + k + 1, :] = sc_ref[k, b:b + 1, :]
        ext_ref[base:base + seq_len, 0:D_INNER] = xs_ref[b * seq_len:(b + 1) * seq_len, :]
        ext_ref[base:base + seq_len, D_INNER:CONV_DIM] = bc_ref[b * seq_len:(b + 1) * seq_len, :]
        for k in range(CONV_HALO):
            row = base + seq_len - CONV_HALO + k
            nsc_ref[k, b:b + 1, :] = ext_ref[row:row + 1, :]
    sub = lax.broadcasted_iota(jnp.int32, (SUBLANES, CONV_STRIP), 0)
    for c0 in range(0, CONV_DIM, CONV_STRIP):
        cs = slice(c0, c0 + CONV_STRIP)
        bias = jnp.broadcast_to(cb_ref[:, cs], (seq_len, CONV_STRIP))
        taps = [jnp.broadcast_to(cw_ref[CONV_HALO - k:CONV_WIDTH - k, cs], (seq_len, CONV_STRIP))
                for k in range(CONV_WIDTH)]
        for b in range(CONV_S_SEQS):
            base = b * slab + SUBLANES
            prev = ext_ref[base - SUBLANES:base, cs]
            cur = ext_ref[base:base + seq_len, cs]
            acc = bias + taps[0] * cur
            for k in range(1, CONV_WIDTH):
                acc = acc + taps[k] * jnp.where(sub < k, pltpu.roll(prev, k, axis=0), pltpu.roll(cur, k, axis=0))
            xc_ref[b * seq_len:(b + 1) * seq_len, cs] = _silu(acc)


def _conv_sample(u, sc_t, conv_w, conv_b, row0, batch, seq):
    assert seq == SUBLANES and batch % CONV_S_SEQS == 0
    rows = CONV_S_SEQS * seq
    assert row0 % rows == 0
    blk0 = row0 // rows
    full = lambda shape: pl.BlockSpec(shape, lambda i: (0,) * len(shape))
    return pl.pallas_call(
        functools.partial(_conv_sample_kernel, seq_len=seq),
        grid=(batch // CONV_S_SEQS,),
        in_specs=[pl.BlockSpec((rows, D_INNER), lambda i: (blk0 + i, _COL.XS // D_INNER)),
                  pl.BlockSpec((rows, 2 * GN), lambda i: (blk0 + i, _COL.BC // (2 * GN))),
                  pl.BlockSpec((CONV_HALO, CONV_S_SEQS, CONV_DIM), lambda i: (0, i, 0)),
                  full((CONV_WIDTH, CONV_DIM)), full((1, CONV_DIM))],
        out_specs=[pl.BlockSpec((rows, CONV_DIM), lambda i: (i, 0)),
                   pl.BlockSpec((CONV_HALO, CONV_S_SEQS, CONV_DIM), lambda i: (0, i, 0))],
        out_shape=[jax.ShapeDtypeStruct((batch * seq, CONV_DIM), F32),
                   jax.ShapeDtypeStruct((CONV_HALO, batch, CONV_DIM), F32)],
        scratch_shapes=[pltpu.VMEM((CONV_S_SEQS * (SUBLANES + seq), CONV_DIM), F32)],
        compiler_params=_cparams(1),
        name="conv_sample",
    )(u, u, sc_t, conv_w, conv_b)


def _ssd_core(rows, same_seq, zs_ref, dt_ref, dtb_ref, alog_ref, dskip_ref, nw_ref, xs_ref, bc_ref, xd_ref, y_ref,
              yoff_fn, state_fn, g_ref):
    li = lax.broadcasted_iota(jnp.int32, (rows, rows), 0)
    si = lax.broadcasted_iota(jnp.int32, (rows, rows), 1)
    causal = same_seq & (si <= li)
    causal_bias = jnp.where(causal, 0.0, -jnp.inf)
    lane = lax.broadcasted_iota(jnp.int32, (rows, LANES), 1)
    first_half = lane < SSM_HEAD_DIM

    dt = jax.nn.softplus(dt_ref[...] + dtb_ref[...])
    a_neg = -jnp.exp(alog_ref[...])
    d_a = dt * a_neg
    acs = jnp.dot(causal.astype(F32), d_a, precision=HIGHEST, preferred_element_type=F32)
    a_end = jnp.dot(same_seq.astype(F32), d_a, precision=HIGHEST, preferred_element_type=F32)
    log_dt = jnp.log(dt)
    acs2 = acs * LOG2E
    seg_rows2 = ((acs - log_dt) * LOG2E).T
    log_w_state = log_dt + a_end - acs
    log_w_state2 = log_w_state * LOG2E
    cd_rows = jnp.exp(a_end)
    e_acs_tab, w_state_tab = jnp.exp(acs), jnp.exp(log_w_state)

    def head_cols(m, j):
        return (jnp.broadcast_to(m[:, 2 * j:2 * j + 1], (rows, LANES)),
                jnp.broadcast_to(m[:, 2 * j + 1:2 * j + 2], (rows, LANES)))

    pairs_per_group = SSM_HEADS // SSM_GROUPS // 2
    for g in range(SSM_GROUPS):
        b_g = bc_ref[:, g * SSM_STATE:(g + 1) * SSM_STATE].astype(BF16)
        c_g = bc_ref[:, GN + g * SSM_STATE:GN + (g + 1) * SSM_STATE].astype(BF16)
        cb = _dot_nt(c_g, b_g)
        yoff = yoff_fn(g, c_g)
        for jj in range(pairs_per_group):
            j = g * pairs_per_group + jj
            ps = slice(j * LANES, (j + 1) * LANES)
            xs_pair = xs_ref[:, ps]
            xs_bf16 = xs_pair.astype(BF16)
            acs_cols = head_cols(acs2, j)
            parts = []
            for hh in range(2):
                h = 2 * j + hh
                col = acs_cols[hh] if rows == LANES else acs2[:, h:h + 1]
                seg2 = col - seg_rows2[h:h + 1, :]
                decay_dt = jnp.exp2(seg2 + causal_bias)
                m = (cb * decay_dt).astype(BF16)
                parts.append(jnp.dot(m, xs_bf16, preferred_element_type=F32))
            yd = jnp.where(first_half, parts[0], parts[1])
            if rows == LANES:
                e_acs = jnp.exp2(jnp.where(first_half, *acs_cols))
                w_state = jnp.exp2(jnp.where(first_half, *head_cols(log_w_state2, j)))
            else:
                e_acs = jnp.where(first_half, *head_cols(e_acs_tab, j))
                w_state = jnp.where(first_half, *head_cols(w_state_tab, j))
            xd_ref[:, ps] = (xs_pair * w_state).astype(xd_ref.dtype)
            y_ref[:, ps] = yd + yoff[:, jj * LANES:(jj + 1) * LANES] * e_acs + dskip_ref[:, ps] * xs_pair
        state_fn(g, cd_rows)

    for g in range(SSM_GROUPS):
        gs = slice(g * GROUP_W, (g + 1) * GROUP_W)
        gg = y_ref[:, gs] * zs_ref[:, gs]
        ms = jnp.mean(gg * gg, axis=-1, keepdims=True)
        g_ref[:, gs] = (gg * lax.rsqrt(ms + RMS_EPS) * nw_ref[:, gs]).astype(BF16)


def _ssd_prompt_kernel(zs_ref, xs_raw_ref, bc_raw_ref, dt_ref, cw_ref, cb_ref, dtb_ref, alog_ref, dskip_ref, nw_ref,
                       g_ref, st_ref, ext_ref, xc_ref, xd_ref, y_ref, stt_ref):
    c = pl.program_id(1)
    rows = SSD_CHUNK

    @pl.when(c == 0)
    def _():
        ext_ref[0:SUBLANES, :] = jnp.zeros((SUBLANES, CONV_DIM), F32)
        stt_ref[...] = jnp.zeros(stt_ref.shape, F32)

    @pl.when(c != 0)
    def _():
        ext_ref[0:SUBLANES, :] = ext_ref[rows:rows + SUBLANES, :]

    ext_ref[SUBLANES:SUBLANES + rows, 0:D_INNER] = xs_raw_ref[...]
    ext_ref[SUBLANES:SUBLANES + rows, D_INNER:CONV_DIM] = bc_raw_ref[...]
    _conv_silu(ext_ref, SUBLANES, rows, cw_ref, cb_ref, xc_ref, 0)
    xs_ref = xc_ref.at[:, 0:D_INNER]
    bc_ref = xc_ref.at[:, D_INNER:CONV_DIM]

    def yoff_fn(g, c_g):
        return jnp.dot(c_g, stt_ref[:, g * GROUP_W:(g + 1) * GROUP_W].astype(BF16), preferred_element_type=F32)

    def state_fn(g, cd_rows):
        b_g = bc_ref[:, g * SSM_STATE:(g + 1) * SSM_STATE].astype(BF16)
        xd_g = xd_ref[:, g * GROUP_W:(g + 1) * GROUP_W]
        new = lax.dot_general(b_g, xd_g, (((0,), (0,)), ((), ())), preferred_element_type=F32)
        lane = lax.broadcasted_iota(jnp.int32, (SSM_STATE, LANES), 1)
        for jj in range(GROUP_W // LANES):
            j = g * (GROUP_W // LANES) + jj
            ps = slice(j * LANES, (j + 1) * LANES)
            cd = jnp.where(lane < SSM_HEAD_DIM,
                           jnp.broadcast_to(cd_rows[0:1, 2 * j:2 * j + 1], (SSM_STATE, LANES)),
                           jnp.broadcast_to(cd_rows[0:1, 2 * j + 1:2 * j + 2], (SSM_STATE, LANES)))
            stt_ref[:, ps] = stt_ref[:, ps] * cd + new[:, jj * LANES:(jj + 1) * LANES]

    same_seq = jnp.full((rows, rows), True)
    _ssd_core(rows, same_seq, zs_ref, dt_ref, dtb_ref, alog_ref, dskip_ref, nw_ref, xs_ref, bc_ref, xd_ref, y_ref,
              yoff_fn, state_fn, g_ref)

    @pl.when(c == pl.num_programs(1) - 1)
    def _():
        for j in range(D_INNER // LANES):
            st_ref[0, j * LANES:(j + 1) * LANES, :] = stt_ref[:, j * LANES:(j + 1) * LANES].T


def _ssd_prompt(u, dt_raw, conv_w, conv_b, dt_bias, a_log, dskip_cols, ssm_norm_w, batch, seq):
    nc = seq // SSD_CHUNK
    rows = SSD_CHUNK
    row_blk = lambda b, c: b * nc + c
    full = lambda shape: pl.BlockSpec(shape, lambda b, c: (0,) * len(shape))
    return pl.pallas_call(
        _ssd_prompt_kernel,
        grid=(batch, nc),
        in_specs=[pl.BlockSpec((rows, D_INNER), lambda b, c: (row_blk(b, c), _COL.Z // D_INNER)),
                  pl.BlockSpec((rows, D_INNER), lambda b, c: (row_blk(b, c), _COL.XS // D_INNER)),
                  pl.BlockSpec((rows, 2 * GN), lambda b, c: (row_blk(b, c), _COL.BC // (2 * GN))),
                  pl.BlockSpec((rows, DT_PAD), lambda b, c: (row_blk(b, c), 0)),
                  full((CONV_WIDTH, CONV_DIM)), full((1, CONV_DIM)), full((1, DT_PAD)), full((1, DT_PAD)),
                  full((1, D_INNER)), full((1, D_INNER))],
        out_specs=[pl.BlockSpec((rows, D_INNER), lambda b, c: (row_blk(b, c), 0)),
                   pl.BlockSpec((1, D_INNER, SSM_STATE), lambda b, c: (b, 0, 0))],
        out_shape=[jax.ShapeDtypeStruct((u.shape[0], D_INNER), BF16),
                   jax.ShapeDtypeStruct((batch, D_INNER, SSM_STATE), F32)],
        scratch_shapes=[pltpu.VMEM((rows + SUBLANES, CONV_DIM), F32),
                        pltpu.VMEM((rows, CONV_DIM), F32),
                        pltpu.VMEM((rows, D_INNER), BF16),
                        pltpu.VMEM((rows, D_INNER), F32),
                        pltpu.VMEM((SSM_STATE, D_INNER), F32)],
        compiler_params=_cparams(2),
        name="ssd_prompt",
    )(u, u, u, dt_raw, conv_w, conv_b, dt_bias, a_log, dskip_cols, ssm_norm_w)


SAMPLE_BT = 4


def _ssd_sample_kernel(zs_ref, xs_ref, bc_ref, dt_ref, st_ref, dtb_ref, alog_ref, dskip_ref, nw_ref, g_all_ref,
                       g_ref, nst_ref, xd_ref, y_ref, yoff_ref, seq_len):
    del g_all_ref
    bt = SAMPLE_BT
    rows = bt * seq_len
    heads_per_group = SSM_HEADS // SSM_GROUPS

    for b in range(bt):
        rs = slice(b * seq_len, (b + 1) * seq_len)
        for g in range(SSM_GROUPS):
            c_bg = bc_ref[rs, GN + g * SSM_STATE:GN + (g + 1) * SSM_STATE].astype(BF16)
            s_bg = st_ref[b, g * GROUP_W:(g + 1) * GROUP_W, :].astype(BF16)
            yoff_ref[rs, g * GROUP_W:(g + 1) * GROUP_W] = _dot_nt(c_bg, s_bg)

    def yoff_fn(g, c_g):
        return yoff_ref[:, g * GROUP_W:(g + 1) * GROUP_W]

    def state_fn(g, cd_rows):
        for b in range(bt):
            rs = slice(b * seq_len, (b + 1) * seq_len)
            b_bg = bc_ref[rs, g * SSM_STATE:(g + 1) * SSM_STATE].astype(BF16)
            xd_bg = xd_ref[rs, g * GROUP_W:(g + 1) * GROUP_W].astype(BF16)
            new = lax.dot_general(xd_bg, b_bg, (((0,), (0,)), ((), ())), preferred_element_type=F32)
            for hh in range(heads_per_group):
                h = g * heads_per_group + hh
                hs = slice(h * SSM_HEAD_DIM, (h + 1) * SSM_HEAD_DIM)
                cd = jnp.broadcast_to(cd_rows[b * seq_len:b * seq_len + 1, h:h + 1], (SSM_HEAD_DIM, SSM_STATE))
                nst_ref[b, hs, :] = st_ref[b, hs, :] * cd + new[hh * SSM_HEAD_DIM:(hh + 1) * SSM_HEAD_DIM, :]

    li = lax.broadcasted_iota(jnp.int32, (rows, rows), 0)
    si = lax.broadcasted_iota(jnp.int32, (rows, rows), 1)
    same_seq = (li // seq_len) == (si // seq_len)
    _ssd_core(rows, same_seq, zs_ref, dt_ref, dtb_ref, alog_ref, dskip_ref, nw_ref, xs_ref, bc_ref, xd_ref, y_ref,
              yoff_fn, state_fn, g_ref)


def _ssd_sample(u, dt_raw, xc, state, dt_bias, a_log, dskip_cols, ssm_norm_w, g_all, row0, batch, seq):
    bt = SAMPLE_BT
    rows = bt * seq
    assert row0 % rows == 0
    blk0 = row0 // rows
    full = lambda shape: pl.BlockSpec(shape, lambda i: (0,) * len(shape))
    return pl.pallas_call(
        functools.partial(_ssd_sample_kernel, seq_len=seq),
        grid=(batch // bt,),
        in_specs=[pl.BlockSpec((rows, D_INNER), lambda i: (blk0 + i, _COL.Z // D_INNER)),
                  pl.BlockSpec((rows, D_INNER), lambda i: (i, 0)),
                  pl.BlockSpec((rows, 2 * GN), lambda i: (i, D_INNER // (2 * GN))),
                  pl.BlockSpec((rows, DT_PAD), lambda i: (blk0 + i, 0)),
                  pl.BlockSpec((bt, D_INNER, SSM_STATE), lambda i: (i, 0, 0)),
                  full((1, DT_PAD)), full((1, DT_PAD)), full((1, D_INNER)), full((1, D_INNER)),
                  pl.BlockSpec(memory_space=pl.ANY)],
        out_specs=[pl.BlockSpec((rows, D_INNER), lambda i: (blk0 + i, 0)),
                   pl.BlockSpec((bt, D_INNER, SSM_STATE), lambda i: (i, 0, 0))],
        out_shape=[jax.ShapeDtypeStruct(g_all.shape, BF16),
                   jax.ShapeDtypeStruct((batch, D_INNER, SSM_STATE), F32)],
        scratch_shapes=[pltpu.VMEM((rows, D_INNER), F32),
                        pltpu.VMEM((rows, D_INNER), F32),
                        pltpu.VMEM((rows, D_INNER), F32)],
        input_output_aliases={9: 0},
        compiler_params=_cparams(1),
        name="ssd_sample",
    )(u, xc, xc, dt_raw, state, dt_bias, a_log, dskip_cols, ssm_norm_w, g_all)


ATTN_BLOCKS = 4
ONES_ROWS = 16


def _attn_prompt_kernel(sink_ref, q_ref, kp_ref, ko_ref, vp_ref, vo_ref, ga_ref, bias_ref, o_ref):
    n = pl.program_id(1)
    no_prev = [jnp.where(n == 0, jnp.inf, 0.0)] + [0.0] * (ATTN_BLOCKS - 1)

    def rows(s):
        return slice(s * WINDOW, (s + 1) * WINDOW)

    def scores(s, g):
        ks = slice(g * HEAD_DIM, (g + 1) * HEAD_DIM)
        k_prev = kp_ref[:, ks] if s == 0 else ko_ref[rows(s - 1), ks]
        v_prev = vp_ref[:, ks] if s == 0 else vo_ref[rows(s - 1), ks]
        kk = jnp.concatenate([k_prev, ko_ref[rows(s), ks]], axis=0).astype(BF16)
        vv_t = jnp.concatenate([jnp.concatenate([v_prev, vo_ref[rows(s), ks]], axis=0).T,
                                jnp.ones((ONES_ROWS, 2 * WINDOW), F32)], axis=0).astype(BF16)
        s_list = []
        for r in range(REP):
            h = g * REP + r
            q = q_ref[rows(s), h * HEAD_DIM:(h + 1) * HEAD_DIM].astype(BF16)
            s_list.append(_dot_nt(kk, q))
        return vv_t, s_list

    def finish(s, g, vv_t, s_list):
        probs = []
        for r in range(REP):
            h = g * REP + r
            s_prev = s_list[r][0:WINDOW] + bias_ref[h, 0:WINDOW, :]
            s_own = s_list[r][WINDOW:2 * WINDOW] + bias_ref[h, WINDOW:2 * WINDOW, :]
            sink = sink_ref[h]
            m = jnp.maximum(jnp.maximum(jnp.max(s_prev, axis=0, keepdims=True) - no_prev[s],
                                        jnp.max(s_own, axis=0, keepdims=True)), sink)
            p = jnp.concatenate([jnp.exp(s_prev - (m + no_prev[s])), jnp.exp(s_own - m)], axis=0)
            probs.append((p.astype(BF16), jnp.exp(sink - m)))
        o_t = []
        for p, p_sink in probs:
            pv = jnp.dot(vv_t, p, preferred_element_type=F32)
            o_t.append(pv[0:HEAD_DIM] / (pv[HEAD_DIM:HEAD_DIM + 1] + p_sink))
        for rp in range(REP // 2):
            o_pair = jnp.concatenate(o_t[2 * rp:2 * rp + 2], axis=0).T
            ps = slice((g * REP + 2 * rp) * HEAD_DIM, (g * REP + 2 * rp + 2) * HEAD_DIM)
            o_ref[rows(s), ps] = (o_pair * ga_ref[rows(s), ps]).astype(BF16)

    units = [(s, g) for s in range(ATTN_BLOCKS) for g in range(KV_HEADS)]
    pending = scores(*units[0])
    for idx, unit in enumerate(units):
        cur = pending
        if idx + 1 < len(units):
            pending = scores(*units[idx + 1])
        finish(*unit, *cur)


def _attn_prompt(u, sinks, bias, batch, seq):
    rows = ATTN_BLOCKS * WINDOW
    assert seq % rows == 0
    ns = seq // rows
    row = lambda b, n: b * ns + n
    prev = lambda b, n: jnp.maximum((b * ns + n) * ATTN_BLOCKS - 1, 0)
    return pl.pallas_call(
        _attn_prompt_kernel,
        grid=(batch, ns),
        in_specs=[pl.BlockSpec(memory_space=pltpu.SMEM),
                  pl.BlockSpec((rows, ATTN_WIDTH), lambda b, n: (row(b, n), _COL.Q // ATTN_WIDTH)),
                  pl.BlockSpec((WINDOW, KV_WIDTH), lambda b, n: (prev(b, n), _COL.K // KV_WIDTH)),
                  pl.BlockSpec((rows, KV_WIDTH), lambda b, n: (row(b, n), _COL.K // KV_WIDTH)),
                  pl.BlockSpec((WINDOW, KV_WIDTH), lambda b, n: (prev(b, n), _COL.V // KV_WIDTH)),
                  pl.BlockSpec((rows, KV_WIDTH), lambda b, n: (row(b, n), _COL.V // KV_WIDTH)),
                  pl.BlockSpec((rows, ATTN_WIDTH), lambda b, n: (row(b, n), _COL.GA // ATTN_WIDTH)),
                  pl.BlockSpec((Q_HEADS, 2 * WINDOW, WINDOW), lambda b, n: (0, 0, 0))],
        out_specs=pl.BlockSpec((rows, ATTN_WIDTH), lambda b, n: (row(b, n), 0)),
        out_shape=jax.ShapeDtypeStruct((u.shape[0], ATTN_WIDTH), BF16),
        compiler_params=_cparams(2),
        name="attn_prompt",
    )(sinks, u, u, u, u, u, u, bias)


def _attn_sample_kernel(q_ref, ga_ref, kvt_ref, ck_ref, cv_ref, bias_ref, og_all_ref, o_ref, nk_ref, nv_ref,
                        s_ref, p_ref, seq_len):
    del og_all_ref
    bt = SAMPLE_BT
    w_buf = ck_ref.shape[-1]
    keep = w_buf - seq_len
    unit_rows = REP * seq_len
    seq_rows = KV_HEADS * unit_rows
    lane0 = (pl.program_id(0) * bt * seq_len) % kvt_ref.shape[1]
    units = [(b, g) for b in range(bt) for g in range(KV_HEADS)]

    lane = lax.broadcasted_iota(jnp.int32, (KV_WIDTH, w_buf), 1)
    for c_ref, n_ref, row0 in ((ck_ref, nk_ref, 0), (cv_ref, nv_ref, KV_WIDTH)):
        fresh = kvt_ref[row0:row0 + KV_WIDTH, :]
        for b in range(bt):
            new_shift = (keep + w_buf - (lane0 + b * seq_len)) % w_buf
            old = c_ref[b].reshape(KV_WIDTH, w_buf)
            new = jnp.where(lane < keep, pltpu.roll(old, keep, axis=1), pltpu.roll(fresh, new_shift, axis=1))
            n_ref[b] = new.reshape(KV_HEADS, HEAD_DIM, w_buf)

    def both(c_ref, n_ref, b, g):
        return jnp.concatenate([c_ref[b, g], n_ref[b, g]], axis=1).astype(BF16)

    for u, (b, g) in enumerate(units):
        rs = slice(b * seq_len, (b + 1) * seq_len)
        qs = jnp.concatenate([q_ref[rs, (g * REP + r) * HEAD_DIM:(g * REP + r + 1) * HEAD_DIM] for r in range(REP)],
                             axis=0)
        s_ref[u * unit_rows:(u + 1) * unit_rows, :] = jnp.dot(
            qs.astype(BF16), both(ck_ref, nk_ref, b, g), preferred_element_type=F32)

    col = lax.broadcasted_iota(jnp.int32, (seq_rows, 2 * w_buf), 1)
    not_sink = (col != seq_len).astype(F32)
    for b in range(bt):
        rows_b = slice(b * seq_rows, (b + 1) * seq_rows)
        s = s_ref[rows_b, :] * not_sink + bias_ref[...]
        m = jnp.max(s, axis=-1, keepdims=True)
        p = jnp.exp(s - m)
        inv = 1.0 / jnp.sum(p, axis=-1, keepdims=True)
        p_ref[rows_b, :] = (p * inv * not_sink).astype(BF16)

    for u, (b, g) in enumerate(units):
        o = _dot_nt(p_ref[u * unit_rows:(u + 1) * unit_rows, :], both(cv_ref, nv_ref, b, g))
        rs = slice(b * seq_len, (b + 1) * seq_len)
        for r in range(REP):
            hs = slice((g * REP + r) * HEAD_DIM, (g * REP + r + 1) * HEAD_DIM)
            o_ref[rs, hs] = (o[r * seq_len:(r + 1) * seq_len, :] * ga_ref[rs, hs]).astype(BF16)


def _attn_sample(u, kv_t, cache_k, cache_v, bias, og_all, row0, batch, seq):
    bt = SAMPLE_BT
    rows = bt * seq
    w_buf = cache_k.shape[-1]
    assert w_buf == LANES and row0 % rows == 0 and LANES % rows == 0
    blk0 = row0 // rows
    score_rows = bt * Q_HEADS * seq
    cache_spec = pl.BlockSpec((bt, KV_HEADS, HEAD_DIM, w_buf), lambda i: (i, 0, 0, 0))
    return pl.pallas_call(
        functools.partial(_attn_sample_kernel, seq_len=seq),
        grid=(batch // bt,),
        in_specs=[pl.BlockSpec((rows, ATTN_WIDTH), lambda i: (blk0 + i, _COL.Q // ATTN_WIDTH)),
                  pl.BlockSpec((rows, ATTN_WIDTH), lambda i: (blk0 + i, _COL.GA // ATTN_WIDTH)),
                  pl.BlockSpec((2 * KV_WIDTH, LANES), lambda i: (0, i * rows // LANES)),
                  cache_spec, cache_spec,
                  pl.BlockSpec(bias.shape, lambda i: (0, 0)),
                  pl.BlockSpec(memory_space=pl.ANY)],
        out_specs=[pl.BlockSpec((rows, ATTN_WIDTH), lambda i: (blk0 + i, 0)), cache_spec, cache_spec],
        out_shape=[jax.ShapeDtypeStruct(og_all.shape, BF16),
                   jax.ShapeDtypeStruct(cache_k.shape, F32),
                   jax.ShapeDtypeStruct(cache_v.shape, F32)],
        scratch_shapes=[pltpu.VMEM((score_rows, 2 * w_buf), F32),
                        pltpu.VMEM((score_rows, 2 * w_buf), BF16)],
        input_output_aliases={6: 0},
        compiler_params=_cparams(1),
        name="attn_sample",
    )(u, u, kv_t, cache_k, cache_v, bias, og_all)


MERGE_TN = 512
EPI_SPLIT = 4


def _merge_kernel(g_ref, og_ref, ms_ref, ma_ref, ws_ref, wa_ref, o_ref, *stage_refs):
    rows = o_ref.shape[0] // EPI_SPLIT
    ssm_refs, attn_refs = stage_refs[:EPI_SPLIT], stage_refs[EPI_SPLIT:]

    def gate_chunk(n):
        rs = slice(n * rows, (n + 1) * rows)
        o_ref[rs, :] = (_sigmoid(ms_ref[rs, :]) * ssm_refs[n][...]
                        + _sigmoid(ma_ref[rs, :]) * attn_refs[n][...]).astype(BF16)

    for n in range(EPI_SPLIT):
        rs = slice(n * rows, (n + 1) * rows)
        ssm_refs[n][...] = jnp.dot(g_ref[rs, :], ws_ref[...], preferred_element_type=F32)
        attn_refs[n][...] = jnp.dot(og_ref[rs, :], wa_ref[...], preferred_element_type=F32)
        if n > 0:
            gate_chunk(n - 1)
    gate_chunk(EPI_SPLIT - 1)


def _merge(g, og, u, w_ssm, w_attn, row0, tm):
    t = g.shape[0]
    tn = MERGE_TN
    assert row0 % tm == 0
    blk0 = row0 // tm
    return pl.pallas_call(
        _merge_kernel,
        grid=(t // tm, D_MODEL // tn),
        in_specs=[pl.BlockSpec((tm, D_INNER), lambda i, j: (i, 0)),
                  pl.BlockSpec((tm, ATTN_WIDTH), lambda i, j: (i, 0)),
                  pl.BlockSpec((tm, tn), lambda i, j: (blk0 + i, _COL.MS // tn + j)),
                  pl.BlockSpec((tm, tn), lambda i, j: (blk0 + i, _COL.MA // tn + j)),
                  pl.BlockSpec((D_INNER, tn), lambda i, j: (0, j)),
                  pl.BlockSpec((ATTN_WIDTH, tn), lambda i, j: (0, j))],
        out_specs=pl.BlockSpec((tm, tn), lambda i, j: (i, j)),
        out_shape=jax.ShapeDtypeStruct((t, D_MODEL), BF16),
        scratch_shapes=[pltpu.VMEM((tm // EPI_SPLIT, tn), F32) for _ in range(2 * EPI_SPLIT)],
        compiler_params=_cparams(2, BIG_VMEM_LIMIT),
        name="merge",
    )(g, og, u, u, w_ssm, w_attn)


def _out_proj_kernel(m_ref, xp_ref, xs_ref, wo_ref, fw_ref, op_ref, os_ref, n_prompt_tiles):
    proj = jnp.dot(m_ref[...], wo_ref[...], preferred_element_type=F32)

    def emit(x_ref, o_ref):
        xn = x_ref[...] + proj
        ms = jnp.mean(xn * xn, axis=-1, keepdims=True)
        o_ref[...] = xn * lax.rsqrt(ms + RMS_EPS) * fw_ref[...]

    @pl.when(pl.program_id(0) < n_prompt_tiles)
    def _():
        emit(xp_ref, op_ref)

    @pl.when(pl.program_id(0) >= n_prompt_tiles)
    def _():
        emit(xs_ref, os_ref)


def _out_proj(m, xp2, xs2, w_out, final_w, tm):
    tp, ts = xp2.shape[0], xs2.shape[0]
    n_p, n_s = tp // tm, ts // tm
    p_blk = lambda i: (jnp.minimum(i, n_p - 1), 0)
    s_blk = lambda i: (jnp.maximum(i - n_p, 0), 0)
    return pl.pallas_call(
        functools.partial(_out_proj_kernel, n_prompt_tiles=n_p),
        grid=(n_p + n_s,),
        in_specs=[pl.BlockSpec((tm, D_MODEL), lambda i: (i, 0)),
                  pl.BlockSpec((tm, D_MODEL), p_blk),
                  pl.BlockSpec((tm, D_MODEL), s_blk),
                  pl.BlockSpec((D_MODEL, D_MODEL), lambda i: (0, 0), pipeline_mode=pl.Buffered(1)),
                  pl.BlockSpec((1, D_MODEL), lambda i: (0, 0))],
        out_specs=[pl.BlockSpec((tm, D_MODEL), p_blk), pl.BlockSpec((tm, D_MODEL), s_blk)],
        out_shape=[jax.ShapeDtypeStruct((tp, D_MODEL), F32), jax.ShapeDtypeStruct((ts, D_MODEL), F32)],
        compiler_params=_cparams(1, BIG_VMEM_LIMIT),
        name="out_proj",
    )(m, xp2, xs2, w_out, final_w)


def _pad_lanes(v):
    return jnp.pad(v.reshape(1, -1), ((0, 0), (0, DT_PAD - v.shape[-1])))


def kernel(x_prompt, x_sample, cache_k, cache_v, state_conv, state_ssm, norm_w, w_in, conv_w, conv_b, dt_bias, a_log,
           d_skip, ssm_norm_w, w_ssm_branch, attn_sinks, w_attn_branch, w_out, rel_bias, final_norm_w):
    assert w_in.shape[0] == 1, "single-layer kernel"
    batch, seq, _ = x_prompt.shape
    dec_batch, dec_seq, _ = x_sample.shape
    w_buf = cache_k.shape[2]
    t_p, t_s = batch * seq, dec_batch * dec_seq
    assert seq % SSD_CHUNK == 0 and seq % WINDOW == 0 and dec_seq == SUBLANES and w_buf == WINDOW
    assert dec_batch % SAMPLE_BT == 0 and (t_p + t_s) % IN_TM == 0

    w_t = jnp.transpose(w_in[0])
    sc_t = jnp.transpose(state_conv[0], (1, 0, 2))
    ck_t = jnp.transpose(cache_k[0], (0, 2, 3, 1))
    cv_t = jnp.transpose(cache_v[0], (0, 2, 3, 1))

    nw = norm_w[0].reshape(1, D_MODEL)
    cw, cb = conv_w[0], conv_b[0].reshape(1, CONV_DIM)
    dtb, alog = _pad_lanes(dt_bias[0]), _pad_lanes(a_log[0])
    dskip_cols = jnp.repeat(d_skip[0], SSM_HEAD_DIM).reshape(1, D_INNER)
    snw = ssm_norm_w[0].reshape(1, D_INNER)
    w_ssm = w_ssm_branch[0].astype(BF16)
    w_attn = w_attn_branch[0].astype(BF16)
    wo = w_out[0].astype(BF16)
    fw = final_norm_w.reshape(1, D_MODEL)
    sinks = attn_sinks[0]

    xp2 = x_prompt.reshape(t_p, D_MODEL)
    xs2 = x_sample.reshape(t_s, D_MODEL)

    band_bias_t = _band_bias_t(rel_bias)
    dec_bias = _decode_bias(rel_bias, sinks, dec_seq, w_buf)

    h, dt_raw = _norm_dt(xp2, xs2, nw, w_t)
    u = _in_proj(h, w_t)
    kv_t = _kv_t(h, w_t, t_p, t_s)

    g, st_p = _ssd_prompt(u, dt_raw, cw, cb, dtb, alog, dskip_cols, snw, batch, seq)
    og = _attn_prompt(u, sinks, band_bias_t, batch, seq)
    xc_s, nsc_t = _conv_sample(u, sc_t, cw, cb, t_p, dec_batch, dec_seq)
    g, st_s = _ssd_sample(u, dt_raw, xc_s, state_ssm[0].reshape(dec_batch, D_INNER, SSM_STATE),
                          dtb, alog, dskip_cols, snw, g, t_p, dec_batch, dec_seq)
    og, nk_t, nv_t = _attn_sample(u, kv_t, ck_t, cv_t, dec_bias, og, t_p, dec_batch, dec_seq)

    m = _merge(g, og, u, w_ssm, w_attn, 0, tm=1024)
    y_p, y_s = _out_proj(m, xp2, xs2, wo, fw, tm=512)

    def seq_tail(n_rows, col0, width):
        return jnp.stack([u[(b + 1) * seq - n_rows:(b + 1) * seq, col0:col0 + width] for b in range(batch)])

    k_p = seq_tail(WINDOW, _COL.K, KV_WIDTH).reshape(1, batch, WINDOW, KV_HEADS, HEAD_DIM)
    v_p = seq_tail(WINDOW, _COL.V, KV_WIDTH).reshape(1, batch, WINDOW, KV_HEADS, HEAD_DIM)
    conv_p = seq_tail(CONV_HALO, _COL.XS, CONV_DIM)[None]
    conv_s = jnp.transpose(nsc_t, (1, 0, 2))[None]
    ssm_p = st_p.reshape(1, batch, SSM_HEADS, SSM_HEAD_DIM, SSM_STATE)
    ssm_s = st_s.reshape(1, dec_batch, SSM_HEADS, SSM_HEAD_DIM, SSM_STATE)
    k_s = jnp.transpose(nk_t, (0, 3, 1, 2))[None]
    v_s = jnp.transpose(nv_t, (0, 3, 1, 2))[None]
    return (y_p.reshape(batch, seq, D_MODEL), y_s.reshape(dec_batch, dec_seq, D_MODEL),
            k_p, v_p, conv_p, ssm_p, k_s, v_s, conv_s, ssm_s)
```

```python
import functools
import math

import numpy as np
import jax
import jax.numpy as jnp
from jax import lax
from jax.experimental import pallas as pl
from jax.experimental.pallas import tpu as pltpu

F32 = jnp.float32
BF16 = jnp.bfloat16
HIGHEST = lax.Precision.HIGHEST

D_MODEL = 2048
D_INNER = 4096
SSM_HEAD_DIM = 64
SSM_HEADS = 64
SSM_GROUPS = 8
SSM_STATE = 128
CONV_WIDTH = 4
CONV_HALO = CONV_WIDTH - 1
GN = SSM_GROUPS * SSM_STATE
CONV_DIM = D_INNER + 2 * GN
SSD_CHUNK = 128
GROUP_W = D_INNER // SSM_GROUPS
HEAD_DIM = 64
Q_HEADS = 32
KV_HEADS = 8
REP = Q_HEADS // KV_HEADS
ATTN_WIDTH = Q_HEADS * HEAD_DIM
KV_WIDTH = KV_HEADS * HEAD_DIM
WINDOW = 128
N_BUCKETS = 32
MAX_EXACT = N_BUCKETS // 2
RMS_EPS = 1e-6
ATTN_SCALE = HEAD_DIM ** -0.5

_OFF_Z = 0
_OFF_XBC = _OFF_Z + D_INNER
_OFF_DT = _OFF_XBC + CONV_DIM
_OFF_Q = _OFF_DT + SSM_HEADS
_OFF_K = _OFF_Q + ATTN_WIDTH
_OFF_V = _OFF_K + KV_WIDTH
_OFF_GA = _OFF_V + KV_WIDTH
_OFF_MS = _OFF_GA + ATTN_WIDTH
_OFF_MA = _OFF_MS + D_MODEL
_IN_COLS = _OFF_MA + D_MODEL

LANES = 128
SUBLANES = 8
DT_PAD = LANES


class _COL:
    Z = 0
    XS = Z + D_INNER
    BC = XS + D_INNER
    Q = BC + 2 * GN
    GA = Q + ATTN_WIDTH
    MS = GA + ATTN_WIDTH
    MA = MS + D_MODEL
    K = MA + D_MODEL
    V = K + KV_WIDTH
    TOTAL = V + KV_WIDTH


VMEM_LIMIT = 52 * 1024 * 1024
BIG_VMEM_LIMIT = 57 * 1024 * 1024


def _cparams(n_grid, limit=VMEM_LIMIT):
    return pltpu.CompilerParams(dimension_semantics=("arbitrary",) * n_grid, vmem_limit_bytes=limit)


LOG2E = math.log2(math.e)
NEG_LOG2E = -LOG2E


def _sigmoid(v):
    return 1.0 / (1.0 + jnp.exp2(v * NEG_LOG2E))


def _silu(v):
    return v * _sigmoid(v)


def _dot_nt(a, b):
    return lax.dot_general(a, b, (((1,), (1,)), ((), ())), preferred_element_type=F32)


NORM_TM = 1024


def _norm_dt_kernel(xp_ref, xs_ref, nw_ref, wdt_ref, h_ref, dt_ref, n_prompt_tiles):
    i = pl.program_id(0)

    def emit(x_ref):
        xf = x_ref[...]
        ms = jnp.mean(xf * xf, axis=-1, keepdims=True)
        h = (xf * lax.rsqrt(ms + RMS_EPS) * nw_ref[...]).astype(BF16)
        h_ref[...] = h
        dt_ref[...] = _dot_nt(h, wdt_ref[...].astype(BF16))

    @pl.when(i < n_prompt_tiles)
    def _():
        emit(xp_ref)

    @pl.when(i >= n_prompt_tiles)
    def _():
        emit(xs_ref)


def _norm_dt(xp2, xs2, norm_w, w_t):
    tp, ts = xp2.shape[0], xs2.shape[0]
    n_p, n_s = tp // NORM_TM, ts // NORM_TM
    return pl.pallas_call(
        functools.partial(_norm_dt_kernel, n_prompt_tiles=n_p),
        grid=(n_p + n_s,),
        in_specs=[pl.BlockSpec((NORM_TM, D_MODEL), lambda i: (jnp.minimum(i, n_p - 1), 0)),
                  pl.BlockSpec((NORM_TM, D_MODEL), lambda i: (jnp.maximum(i - n_p, 0), 0)),
                  pl.BlockSpec((1, D_MODEL), lambda i: (0, 0)),
                  pl.BlockSpec((DT_PAD, D_MODEL), lambda i: (_OFF_DT // DT_PAD, 0))],
        out_specs=[pl.BlockSpec((NORM_TM, D_MODEL), lambda i: (i, 0)),
                   pl.BlockSpec((NORM_TM, DT_PAD), lambda i: (i, 0))],
        out_shape=[jax.ShapeDtypeStruct((tp + ts, D_MODEL), BF16),
                   jax.ShapeDtypeStruct((tp + ts, DT_PAD), F32)],
        compiler_params=_cparams(1),
        name="norm_dt",
    )(xp2, xs2, norm_w, w_t)


IN_TM = 1536
IN_TN = 1024
IN_CHUNK = 512
assert _COL.TOTAL % IN_TN == 0 and IN_TN == 2 * IN_CHUNK and _OFF_DT % IN_CHUNK == 0


def _src_row(c):
    straight = _OFF_DT // IN_CHUNK
    jj = c - straight
    q_t, kv_t = ATTN_WIDTH // IN_CHUNK, 2 * KV_WIDTH // IN_CHUNK
    gate_t = (ATTN_WIDTH + 2 * D_MODEL) // IN_CHUNK
    shifted = jnp.where(jj < q_t, jj, jnp.where(jj < q_t + gate_t, jj + kv_t, jj - gate_t))
    return pl.multiple_of(jnp.where(c < straight, c * IN_CHUNK, _OFF_Q + shifted * IN_CHUNK), SSM_HEADS)


IN_NSPLIT = 4
IN_MW = IN_TM // IN_NSPLIT
J_Z1 = _COL.XS // IN_TN
J_Q0, J_Q1 = _COL.Q // IN_TN, _COL.GA // IN_TN
J_GA0, J_GA1 = _COL.GA // IN_TN, _COL.MS // IN_TN
assert _COL.XS % IN_TN == 0 and _COL.Q % IN_TN == 0 and _COL.GA % IN_TN == 0 and _COL.MS % IN_TN == 0
assert math.log2(HEAD_DIM) % 2 == 0, "the folded attention scale must be a power of two"


def _in_proj_kernel(h_ref, w0_ref, w1_ref, o_ref, wbf_ref, *stage_refs):
    j = pl.program_id(0)

    @pl.when(pl.program_id(1) == 0)
    def _():
        scale = jnp.where((j >= J_Q0) & (j < J_Q1), ATTN_SCALE, 1.0)
        wbf_ref[0:IN_CHUNK, :] = (w0_ref[...] * scale).astype(BF16)
        wbf_ref[IN_CHUNK:IN_TN, :] = (w1_ref[...] * scale).astype(BF16)

    is_silu_tile = (j < J_Z1) | ((j >= J_GA0) & (j < J_GA1))

    @pl.when(is_silu_tile)
    def _():
        def silu_chunk(n):
            o_ref[n * IN_MW:(n + 1) * IN_MW, :] = _silu(stage_refs[n][...])

        for n in range(IN_NSPLIT):
            stage_refs[n][...] = _dot_nt(h_ref[n * IN_MW:(n + 1) * IN_MW, :], wbf_ref[...])
            if n > 0:
                silu_chunk(n - 1)
        silu_chunk(IN_NSPLIT - 1)

    @pl.when(jnp.logical_not(is_silu_tile))
    def _():
        o_ref[...] = _dot_nt(h_ref[...], wbf_ref[...])


def _in_proj(h, w_t):
    t = h.shape[0]
    return pl.pallas_call(
        _in_proj_kernel,
        grid=(_COL.TOTAL // IN_TN, t // IN_TM),
        in_specs=[pl.BlockSpec((IN_TM, D_MODEL), lambda j, i: (i, 0)),
                  pl.BlockSpec((pl.Element(IN_CHUNK), pl.Element(D_MODEL)), lambda j, i: (_src_row(2 * j), 0)),
                  pl.BlockSpec((pl.Element(IN_CHUNK), pl.Element(D_MODEL)), lambda j, i: (_src_row(2 * j + 1), 0))],
        out_specs=pl.BlockSpec((IN_TM, IN_TN), lambda j, i: (i, j)),
        out_shape=jax.ShapeDtypeStruct((t, _COL.TOTAL), F32),
        scratch_shapes=[pltpu.VMEM((IN_TN, D_MODEL), BF16)]
        + [pltpu.VMEM((IN_MW, IN_TN), F32) for _ in range(IN_NSPLIT)],
        compiler_params=_cparams(2, BIG_VMEM_LIMIT),
        name="in_proj",
    )(h, w_t, w_t)


def _kv_t_kernel(w_ref, h_ref, o_ref):
    o_ref[...] = _dot_nt(w_ref[...].astype(BF16), h_ref[...])


def _kv_t(h, w_t, row0, rows):
    assert row0 % rows == 0
    return pl.pallas_call(
        _kv_t_kernel,
        grid=(1,),
        in_specs=[pl.BlockSpec((pl.Element(2 * KV_WIDTH), pl.Element(D_MODEL)), lambda i: (_OFF_K, 0)),
                  pl.BlockSpec((rows, D_MODEL), lambda i: (row0 // rows, 0))],
        out_specs=pl.BlockSpec((2 * KV_WIDTH, rows), lambda i: (0, 0)),
        out_shape=jax.ShapeDtypeStruct((2 * KV_WIDTH, rows), F32),
        compiler_params=_cparams(1),
        name="kv_t",
    )(w_t, h)


def _bucket_of_dist(dist):
    n = np.maximum(dist, 0)
    nf = np.maximum(n, 1).astype(np.float32)
    large = MAX_EXACT + (np.log(nf / MAX_EXACT) / math.log(WINDOW / MAX_EXACT)
                         * (N_BUCKETS - MAX_EXACT)).astype(np.int32)
    large = np.minimum(large, N_BUCKETS - 1)
    bucket = np.where(n < MAX_EXACT, n, large)
    return np.where((dist >= 0) & (dist <= WINDOW), bucket, -1).astype(np.int32)


def _bias_of_bucket(bk, table_ref, h):
    acc = jnp.full(bk.shape, -jnp.inf, F32)
    for b in range(N_BUCKETS):
        acc = jnp.where(bk == b, table_ref[b, h], acc)
    return acc


BAND_BIAS_HEADS = 8


def _band_bias_kernel(table_ref, bucket_ref, o_ref):
    for hh in range(BAND_BIAS_HEADS):
        o_ref[hh] = _bias_of_bucket(bucket_ref[...], table_ref, pl.program_id(0) * BAND_BIAS_HEADS + hh)


def _band_bias_t(rel_bias):
    kj = np.arange(2 * WINDOW)[:, None]
    qi = np.arange(WINDOW)[None, :]
    bucket = jnp.asarray(_bucket_of_dist(qi + WINDOW - kj))
    return pl.pallas_call(
        _band_bias_kernel,
        grid=(Q_HEADS // BAND_BIAS_HEADS,),
        in_specs=[pl.BlockSpec(memory_space=pltpu.SMEM),
                  pl.BlockSpec((2 * WINDOW, WINDOW), lambda h: (0, 0))],
        out_specs=pl.BlockSpec((BAND_BIAS_HEADS, 2 * WINDOW, WINDOW), lambda h: (h, 0, 0)),
        out_shape=jax.ShapeDtypeStruct((Q_HEADS, 2 * WINDOW, WINDOW), F32),
        compiler_params=_cparams(1),
        name="band_bias",
    )(rel_bias, bucket)


def _decode_bias_kernel(table_ref, sink_ref, bucket_ref, o_ref, seq):
    g = pl.program_id(0)
    sink_lane = lax.broadcasted_iota(jnp.int32, bucket_ref.shape, 1) == seq
    for r in range(REP):
        h = g * REP + r
        bias = _bias_of_bucket(bucket_ref[...], table_ref, h)
        o_ref[r * seq:(r + 1) * seq, :] = jnp.where(sink_lane, sink_ref[h], bias)


def _decode_bias(rel_bias, sinks, seq, w_buf):
    l = np.arange(seq)[:, None]
    j = np.arange(w_buf)[None, :]
    dist_old = np.where((j < seq), l + w_buf - j, -1)
    dist_new = np.where(j < w_buf - seq, l + w_buf - (j + seq), l - (j - (w_buf - seq)))
    bucket = jnp.asarray(np.concatenate([_bucket_of_dist(dist_old), _bucket_of_dist(dist_new)], axis=1))
    return pl.pallas_call(
        functools.partial(_decode_bias_kernel, seq=seq),
        grid=(KV_HEADS,),
        in_specs=[pl.BlockSpec(memory_space=pltpu.SMEM), pl.BlockSpec(memory_space=pltpu.SMEM),
                  pl.BlockSpec(bucket.shape, lambda g: (0, 0))],
        out_specs=pl.BlockSpec((REP * seq, 2 * w_buf), lambda g: (g, 0)),
        out_shape=jax.ShapeDtypeStruct((Q_HEADS * seq, 2 * w_buf), F32),
        compiler_params=_cparams(1),
        name="decode_bias",
    )(rel_bias, sinks, bucket)


CONV_STRIP = 512


def _conv_silu(ext_ref, base, rows, cw_ref, cb_ref, xc_ref, out_row):
    n_tiles = rows // SUBLANES
    sub = lax.broadcasted_iota(jnp.int32, (SUBLANES, CONV_STRIP), 0)
    from_prev = [sub < k for k in range(CONV_WIDTH)]
    for c0 in range(0, CONV_DIM, CONV_STRIP):
        cs = slice(c0, c0 + CONV_STRIP)
        bias = jnp.broadcast_to(cb_ref[:, cs], (SUBLANES, CONV_STRIP))
        taps = [jnp.broadcast_to(cw_ref[CONV_HALO - k:CONV_WIDTH - k, cs], (SUBLANES, CONV_STRIP))
                for k in range(CONV_WIDTH)]
        prev = ext_ref[base - SUBLANES:base, cs]
        prev_rot = [None] + [pltpu.roll(prev, k, axis=0) for k in range(1, CONV_WIDTH)]
        for t in range(n_tiles):
            cur = ext_ref[base + t * SUBLANES:base + (t + 1) * SUBLANES, cs]
            acc = bias + taps[0] * cur
            cur_rot = [None]
            for k in range(1, CONV_WIDTH):
                cur_rot.append(pltpu.roll(cur, k, axis=0))
                acc = acc + taps[k] * jnp.where(from_prev[k], prev_rot[k], cur_rot[k])
            xc_ref[out_row + t * SUBLANES:out_row + (t + 1) * SUBLANES, cs] = _silu(acc)
            prev_rot = cur_rot


CONV_S_SEQS = 16


def _conv_sample_kernel(xs_ref, bc_ref, sc_ref, cw_ref, cb_ref, xc_ref, nsc_ref, ext_ref, seq_len):
    slab = SUBLANES + seq_len
    for b in range(CONV_S_SEQS):
        base = b * slab + SUBLANES
        for k in range(CONV_HALO):
            ext_ref[base - CONV_HALO + k:base - CONV_HALO + k + 1, :] = sc_ref[k, b:b + 1, :]
        ext_ref[base:base + seq_len, 0:D_INNER] = xs_ref[b * seq_len:(b + 1) * seq_len, :]
        ext_ref[base:base + seq_len, D_INNER:CONV_DIM] = bc_ref[b * seq_len:(b + 1) * seq_len, :]
        for k in range(CONV_HALO):
            row = base + seq_len - CONV_HALO + k
            nsc_ref[k, b:b + 1, :] = ext_ref[row:row + 1, :]
    sub = lax.broadcasted_iota(jnp.int32, (SUBLANES, CONV_STRIP), 0)
    for c0 in range(0, CONV_DIM, CONV_STRIP):
        cs = slice(c0, c0 + CONV_STRIP)
        bias = jnp.broadcast_to(cb_ref[:, cs], (seq_len, CONV_STRIP))
        taps = [jnp.broadcast_to(cw_ref[CONV_HALO - k:CONV_WIDTH - k, cs], (seq_len, CONV_STRIP))
                for k in range(CONV_WIDTH)]
        for b in range(CONV_S_SEQS):
            base = b * slab + SUBLANES
            prev = ext_ref[base - SUBLANES:base, cs]
            cur = ext_ref[base:base + seq_len, cs]
            acc = bias + taps[0] * cur
            for k in range(1, CONV_WIDTH):
                acc = acc + taps[k] * jnp.where(sub < k, pltpu.roll(prev, k, axis=0), pltpu.roll(cur, k, axis=0))
            xc_ref[b * seq_len:(b + 1) * seq_len, cs] = _silu(acc)


def _conv_sample(u, sc_t, conv_w, conv_b, row0, batch, seq):
    assert seq == SUBLANES and batch % CONV_S_SEQS == 0
    rows = CONV_S_SEQS * seq
    assert row0 % rows == 0
    blk0 = row0 // rows
    full = lambda shape: pl.BlockSpec(shape, lambda i: (0,) * len(shape))
    return pl.pallas_call(
        functools.partial(_conv_sample_kernel, seq_len=seq),
        grid=(batch // CONV_S_SEQS,),
        in_specs=[pl.BlockSpec((rows, D_INNER), lambda i: (blk0 + i, _COL.XS // D_INNER)),
                  pl.BlockSpec((rows, 2 * GN), lambda i: (blk0 + i, _COL.BC // (2 * GN))),
                  pl.BlockSpec((CONV_HALO, CONV_S_SEQS, CONV_DIM), lambda i: (0, i, 0)),
                  full((CONV_WIDTH, CONV_DIM)), full((1, CONV_DIM))],
        out_specs=[pl.BlockSpec((rows, CONV_DIM), lambda i: (i, 0)),
                   pl.BlockSpec((CONV_HALO, CONV_S_SEQS, CONV_DIM), lambda i: (0, i, 0))],
        out_shape=[jax.ShapeDtypeStruct((batch * seq, CONV_DIM), F32),
                   jax.ShapeDtypeStruct((CONV_HALO, batch, CONV_DIM), F32)],
        scratch_shapes=[pltpu.VMEM((CONV_S_SEQS * (SUBLANES + seq), CONV_DIM), F32)],
        compiler_params=_cparams(1),
        name="conv_sample",
    )(u, u, sc_t, conv_w, conv_b)


def _ssd_core(rows, same_seq, zs_ref, dt_ref, dtb_ref, alog_ref, dskip_ref, nw_ref, xs_ref, bc_ref, xd_ref, y_ref,
              yoff_fn, state_fn, g_ref):
    li = lax.broadcasted_iota(jnp.int32, (rows, rows), 0)
    si = lax.broadcasted_iota(jnp.int32, (rows, rows), 1)
    causal = same_seq & (si <= li)
    causal_bias = jnp.where(causal, 0.0, -jnp.inf)
    lane = lax.broadcasted_iota(jnp.int32, (rows, LANES), 1)
    first_half = lane < SSM_HEAD_DIM

    dt = jax.nn.softplus(dt_ref[...] + dtb_ref[...])
    a_neg = -jnp.exp(alog_ref[...])
    d_a = dt * a_neg
    acs = jnp.dot(causal.astype(F32), d_a, precision=HIGHEST, preferred_element_type=F32)
    a_end = jnp.dot(same_seq.astype(F32), d_a, precision=HIGHEST, preferred_element_type=F32)
    log_dt = jnp.log(dt)
    acs2 = acs * LOG2E
    seg_rows2 = ((acs - log_dt) * LOG2E).T
    log_w_state = log_dt + a_end - acs
    log_w_state2 = log_w_state * LOG2E
    cd_rows = jnp.exp(a_end)
    e_acs_tab, w_state_tab = jnp.exp(acs), jnp.exp(log_w_state)

    def head_cols(m, j):
        return (jnp.broadcast_to(m[:, 2 * j:2 * j + 1], (rows, LANES)),
                jnp.broadcast_to(m[:, 2 * j + 1:2 * j + 2], (rows, LANES)))

    pairs_per_group = SSM_HEADS // SSM_GROUPS // 2
    for g in range(SSM_GROUPS):
        b_g = bc_ref[:, g * SSM_STATE:(g + 1) * SSM_STATE].astype(BF16)
        c_g = bc_ref[:, GN + g * SSM_STATE:GN + (g + 1) * SSM_STATE].astype(BF16)
        cb = _dot_nt(c_g, b_g)
        yoff = yoff_fn(g, c_g)
        for jj in range(pairs_per_group):
            j = g * pairs_per_group + jj
            ps = slice(j * LANES, (j + 1) * LANES)
            xs_pair = xs_ref[:, ps]
            xs_bf16 = xs_pair.astype(BF16)
            acs_cols = head_cols(acs2, j)
            parts = []
            for hh in range(2):
                h = 2 * j + hh
                col = acs_cols[hh] if rows == LANES else acs2[:, h:h + 1]
                seg2 = col - seg_rows2[h:h + 1, :]
                decay_dt = jnp.exp2(seg2 + causal_bias)
                m = (cb * decay_dt).astype(BF16)
                parts.append(jnp.dot(m, xs_bf16, preferred_element_type=F32))
            yd = jnp.where(first_half, parts[0], parts[1])
            if rows == LANES:
                e_acs = jnp.exp2(jnp.where(first_half, *acs_cols))
                w_state = jnp.exp2(jnp.where(first_half, *head_cols(log_w_state2, j)))
            else:
                e_acs = jnp.where(first_half, *head_cols(e_acs_tab, j))
                w_state = jnp.where(first_half, *head_cols(w_state_tab, j))
            xd_ref[:, ps] = (xs_pair * w_state).astype(xd_ref.dtype)
            y_ref[:, ps] = yd + yoff[:, jj * LANES:(jj + 1) * LANES] * e_acs + dskip_ref[:, ps] * xs_pair
        state_fn(g, cd_rows)

    for g in range(SSM_GROUPS):
        gs = slice(g * GROUP_W, (g + 1) * GROUP_W)
        gg = y_ref[:, gs] * zs_ref[:, gs]
        ms = jnp.mean(gg * gg, axis=-1, keepdims=True)
        g_ref[:, gs] = (gg * lax.rsqrt(ms + RMS_EPS) * nw_ref[:, gs]).astype(BF16)


def _ssd_prompt_kernel(zs_ref, xs_raw_ref, bc_raw_ref, dt_ref, cw_ref, cb_ref, dtb_ref, alog_ref, dskip_ref, nw_ref,
                       g_ref, st_ref, ext_ref, xc_ref, xd_ref, y_ref, stt_ref):
    c = pl.program_id(1)
    rows = SSD_CHUNK

    @pl.when(c == 0)
    def _():
        ext_ref[0:SUBLANES, :] = jnp.zeros((SUBLANES, CONV_DIM), F32)
        stt_ref[...] = jnp.zeros(stt_ref.shape, F32)

    @pl.when(c != 0)
    def _():
        ext_ref[0:SUBLANES, :] = ext_ref[rows:rows + SUBLANES, :]

    ext_ref[SUBLANES:SUBLANES + rows, 0:D_INNER] = xs_raw_ref[...]
    ext_ref[SUBLANES:SUBLANES + rows, D_INNER:CONV_DIM] = bc_raw_ref[...]
    _conv_silu(ext_ref, SUBLANES, rows, cw_ref, cb_ref, xc_ref, 0)
    xs_ref = xc_ref.at[:, 0:D_INNER]
    bc_ref = xc_ref.at[:, D_INNER:CONV_DIM]

    def yoff_fn(g, c_g):
        return jnp.dot(c_g, stt_ref[:, g * GROUP_W:(g + 1) * GROUP_W].astype(BF16), preferred_element_type=F32)

    def state_fn(g, cd_rows):
        b_g = bc_ref[:, g * SSM_STATE:(g + 1) * SSM_STATE].astype(BF16)
        xd_g = xd_ref[:, g * GROUP_W:(g + 1) * GROUP_W]
        new = lax.dot_general(b_g, xd_g, (((0,), (0,)), ((), ())), preferred_element_type=F32)
        lane = lax.broadcasted_iota(jnp.int32, (SSM_STATE, LANES), 1)
        for jj in range(GROUP_W // LANES):
            j = g * (GROUP_W // LANES) + jj
            ps = slice(j * LANES, (j + 1) * LANES)
            cd = jnp.where(lane < SSM_HEAD_DIM,
                           jnp.broadcast_to(cd_rows[0:1, 2 * j:2 * j + 1], (SSM_STATE, LANES)),
                           jnp.broadcast_to(cd_rows[0:1, 2 * j + 1:2 * j + 2], (SSM_STATE, LANES)))
            stt_ref[:, ps] = stt_ref[:, ps] * cd + new[:, jj * LANES:(jj + 1) * LANES]

    same_seq = jnp.full((rows, rows), True)
    _ssd_core(rows, same_seq, zs_ref, dt_ref, dtb_ref, alog_ref, dskip_ref, nw_ref, xs_ref, bc_ref, xd_ref, y_ref,
              yoff_fn, state_fn, g_ref)

    @pl.when(c == pl.num_programs(1) - 1)
    def _():
        for j in range(D_INNER // LANES):
            st_ref[0, j * LANES:(j + 1) * LANES, :] = stt_ref[:, j * LANES:(j + 1) * LANES].T


def _ssd_prompt(u, dt_raw, conv_w, conv_b, dt_bias, a_log, dskip_cols, ssm_norm_w, batch, seq):
    nc = seq // SSD_CHUNK
    rows = SSD_CHUNK
    row_blk = lambda b, c: b * nc + c
    full = lambda shape: pl.BlockSpec(shape, lambda b, c: (0,) * len(shape))
    return pl.pallas_call(
        _ssd_prompt_kernel,
        grid=(batch, nc),
        in_specs=[pl.BlockSpec((rows, D_INNER), lambda b, c: (row_blk(b, c), _COL.Z // D_INNER)),
                  pl.BlockSpec((rows, D_INNER), lambda b, c: (row_blk(b, c), _COL.XS // D_INNER)),
                  pl.BlockSpec((rows, 2 * GN), lambda b, c: (row_blk(b, c), _COL.BC // (2 * GN))),
                  pl.BlockSpec((rows, DT_PAD), lambda b, c: (row_blk(b, c), 0)),
                  full((CONV_WIDTH, CONV_DIM)), full((1, CONV_DIM)), full((1, DT_PAD)), full((1, DT_PAD)),
                  full((1, D_INNER)), full((1, D_INNER))],
        out_specs=[pl.BlockSpec((rows, D_INNER), lambda b, c: (row_blk(b, c), 0)),
                   pl.BlockSpec((1, D_INNER, SSM_STATE), lambda b, c: (b, 0, 0))],
        out_shape=[jax.ShapeDtypeStruct((u.shape[0], D_INNER), BF16),
                   jax.ShapeDtypeStruct((batch, D_INNER, SSM_STATE), F32)],
        scratch_shapes=[pltpu.VMEM((rows + SUBLANES, CONV_DIM), F32),
                        pltpu.VMEM((rows, CONV_DIM), F32),
                        pltpu.VMEM((rows, D_INNER), BF16),
                        pltpu.VMEM((rows, D_INNER), F32),
                        pltpu.VMEM((SSM_STATE, D_INNER), F32)],
        compiler_params=_cparams(2),
        name="ssd_prompt",
    )(u, u, u, dt_raw, conv_w, conv_b, dt_bias, a_log, dskip_cols, ssm_norm_w)


SAMPLE_BT = 4


def _ssd_sample_kernel(zs_ref, xs_ref, bc_ref, dt_ref, st_ref, dtb_ref, alog_ref, dskip_ref, nw_ref, g_all_ref,
                       g_ref, nst_ref, xd_ref, y_ref, yoff_ref, seq_len):
    del g_all_ref
    bt = SAMPLE_BT
    rows = bt * seq_len
    heads_per_group = SSM_HEADS // SSM_GROUPS

    for b in range(bt):
        rs = slice(b * seq_len, (b + 1) * seq_len)
        for g in range(SSM_GROUPS):
            c_bg = bc_ref[rs, GN + g * SSM_STATE:GN + (g + 1) * SSM_STATE].astype(BF16)
            s_bg = st_ref[b, g * GROUP_W:(g + 1) * GROUP_W, :].astype(BF16)
            yoff_ref[rs, g * GROUP_W:(g + 1) * GROUP_W] = _dot_nt(c_bg, s_bg)

    def yoff_fn(g, c_g):
        return yoff_ref[:, g * GROUP_W:(g + 1) * GROUP_W]

    def state_fn(g, cd_rows):
        for b in range(bt):
            rs = slice(b * seq_len, (b + 1) * seq_len)
            b_bg = bc_ref[rs, g * SSM_STATE:(g + 1) * SSM_STATE].astype(BF16)
            xd_bg = xd_ref[rs, g * GROUP_W:(g + 1) * GROUP_W].astype(BF16)
            new = lax.dot_general(xd_bg, b_bg, (((0,), (0,)), ((), ())), preferred_element_type=F32)
            for hh in range(heads_per_group):
                h = g * heads_per_group + hh
                hs = slice(h * SSM_HEAD_DIM, (h + 1) * SSM_HEAD_DIM)
                cd = jnp.broadcast_to(cd_rows[b * seq_len:b * seq_len + 1, h:h + 1], (SSM_HEAD_DIM, SSM_STATE))
                nst_ref[b, hs, :] = st_ref[b, hs, :] * cd + new[hh * SSM_HEAD_DIM:(hh + 1) * SSM_HEAD_DIM, :]

    li = lax.broadcasted_iota(jnp.int32, (rows, rows), 0)
    si = lax.broadcasted_iota(jnp.int32, (rows, rows), 1)
    same_seq = (li // seq_len) == (si // seq_len)
    _ssd_core(rows, same_seq, zs_ref, dt_ref, dtb_ref, alog_ref, dskip_ref, nw_ref, xs_ref, bc_ref, xd_ref, y_ref,
              yoff_fn, state_fn, g_ref)


def _ssd_sample(u, dt_raw, xc, state, dt_bias, a_log, dskip_cols, ssm_norm_w, g_all, row0, batch, seq):
    bt = SAMPLE_BT
    rows = bt * seq
    assert row0 % rows == 0
    blk0 = row0 // rows
    full = lambda shape: pl.BlockSpec(shape, lambda i: (0,) * len(shape))
    return pl.pallas_call(
        functools.partial(_ssd_sample_kernel, seq_len=seq),
        grid=(batch // bt,),
        in_specs=[pl.BlockSpec((rows, D_INNER), lambda i: (blk0 + i, _COL.Z // D_INNER)),
                  pl.BlockSpec((rows, D_INNER), lambda i: (i, 0)),
                  pl.BlockSpec((rows, 2 * GN), lambda i: (i, D_INNER // (2 * GN))),
                  pl.BlockSpec((rows, DT_PAD), lambda i: (blk0 + i, 0)),
                  pl.BlockSpec((bt, D_INNER, SSM_STATE), lambda i: (i, 0, 0)),
                  full((1, DT_PAD)), full((1, DT_PAD)), full((1, D_INNER)), full((1, D_INNER)),
                  pl.BlockSpec(memory_space=pl.ANY)],
        out_specs=[pl.BlockSpec((rows, D_INNER), lambda i: (blk0 + i, 0)),
                   pl.BlockSpec((bt, D_INNER, SSM_STATE), lambda i: (i, 0, 0))],
        out_shape=[jax.ShapeDtypeStruct(g_all.shape, BF16),
                   jax.ShapeDtypeStruct((batch, D_INNER, SSM_STATE), F32)],
        scratch_shapes=[pltpu.VMEM((rows, D_INNER), F32),
                        pltpu.VMEM((rows, D_INNER), F32),
                        pltpu.VMEM((rows, D_INNER), F32)],
        input_output_aliases={9: 0},
        compiler_params=_cparams(1),
        name="ssd_sample",
    )(u, xc, xc, dt_raw, state, dt_bias, a_log, dskip_cols, ssm_norm_w, g_all)


ATTN_BLOCKS = 4
ONES_ROWS = 16


def _attn_prompt_kernel(sink_ref, q_ref, kp_ref, ko_ref, vp_ref, vo_ref, ga_ref, bias_ref, o_ref):
    n = pl.program_id(1)
    no_prev = [jnp.where(n == 0, jnp.inf, 0.0)] + [0.0] * (ATTN_BLOCKS - 1)

    def rows(s):
        return slice(s * WINDOW, (s + 1) * WINDOW)

    def scores(s, g):
        ks = slice(g * HEAD_DIM, (g + 1) * HEAD_DIM)
        k_prev = kp_ref[:, ks] if s == 0 else ko_ref[rows(s - 1), ks]
        v_prev = vp_ref[:, ks] if s == 0 else vo_ref[rows(s - 1), ks]
        kk = jnp.concatenate([k_prev, ko_ref[rows(s), ks]], axis=0).astype(BF16)
        vv_t = jnp.concatenate([jnp.concatenate([v_prev, vo_ref[rows(s), ks]], axis=0).T,
                                jnp.ones((ONES_ROWS, 2 * WINDOW), F32)], axis=0).astype(BF16)
        s_list = []
        for r in range(REP):
            h = g * REP + r
            q = q_ref[rows(s), h * HEAD_DIM:(h + 1) * HEAD_DIM].astype(BF16)
            s_list.append(_dot_nt(kk, q))
        return vv_t, s_list

    def finish(s, g, vv_t, s_list):
        probs = []
        for r in range(REP):
            h = g * REP + r
            s_prev = s_list[r][0:WINDOW] + bias_ref[h, 0:WINDOW, :]
            s_own = s_list[r][WINDOW:2 * WINDOW] + bias_ref[h, WINDOW:2 * WINDOW, :]
            sink = sink_ref[h]
            m = jnp.maximum(jnp.maximum(jnp.max(s_prev, axis=0, keepdims=True) - no_prev[s],
                                        jnp.max(s_own, axis=0, keepdims=True)), sink)
            p = jnp.concatenate([jnp.exp(s_prev - (m + no_prev[s])), jnp.exp(s_own - m)], axis=0)
            probs.append((p.astype(BF16), jnp.exp(sink - m)))
        o_t = []
        for p, p_sink in probs:
            pv = jnp.dot(vv_t, p, preferred_element_type=F32)
            o_t.append(pv[0:HEAD_DIM] / (pv[HEAD_DIM:HEAD_DIM + 1] + p_sink))
        for rp in range(REP // 2):
            o_pair = jnp.concatenate(o_t[2 * rp:2 * rp + 2], axis=0).T
            ps = slice((g * REP + 2 * rp) * HEAD_DIM, (g * REP + 2 * rp + 2) * HEAD_DIM)
            o_ref[rows(s), ps] = (o_pair * ga_ref[rows(s), ps]).astype(BF16)

    units = [(s, g) for s in range(ATTN_BLOCKS) for g in range(KV_HEADS)]
    pending = scores(*units[0])
    for idx, unit in enumerate(units):
        cur = pending
        if idx + 1 < len(units):
            pending = scores(*units[idx + 1])
        finish(*unit, *cur)


def _attn_prompt(u, sinks, bias, batch, seq):
    rows = ATTN_BLOCKS * WINDOW
    assert seq % rows == 0
    ns = seq // rows
    row = lambda b, n: b * ns + n
    prev = lambda b, n: jnp.maximum((b * ns + n) * ATTN_BLOCKS - 1, 0)
    return pl.pallas_call(
        _attn_prompt_kernel,
        grid=(batch, ns),
        in_specs=[pl.BlockSpec(memory_space=pltpu.SMEM),
                  pl.BlockSpec((rows, ATTN_WIDTH), lambda b, n: (row(b, n), _COL.Q // ATTN_WIDTH)),
                  pl.BlockSpec((WINDOW, KV_WIDTH), lambda b, n: (prev(b, n), _COL.K // KV_WIDTH)),
                  pl.BlockSpec((rows, KV_WIDTH), lambda b, n: (row(b, n), _COL.K // KV_WIDTH)),
                  pl.BlockSpec((WINDOW, KV_WIDTH), lambda b, n: (prev(b, n), _COL.V // KV_WIDTH)),
                  pl.BlockSpec((rows, KV_WIDTH), lambda b, n: (row(b, n), _COL.V // KV_WIDTH)),
                  pl.BlockSpec((rows, ATTN_WIDTH), lambda b, n: (row(b, n), _COL.GA // ATTN_WIDTH)),
                  pl.BlockSpec((Q_HEADS, 2 * WINDOW, WINDOW), lambda b, n: (0, 0, 0))],
        out_specs=pl.BlockSpec((rows, ATTN_WIDTH), lambda b, n: (row(b, n), 0)),
        out_shape=jax.ShapeDtypeStruct((u.shape[0], ATTN_WIDTH), BF16),
        compiler_params=_cparams(2),
        name="attn_prompt",
    )(sinks, u, u, u, u, u, u, bias)


def _attn_sample_kernel(q_ref, ga_ref, kvt_ref, ck_ref, cv_ref, bias_ref, og_all_ref, o_ref, nk_ref, nv_ref,
                        s_ref, p_ref, inv_ref, seq_len):
    del og_all_ref
    bt = SAMPLE_BT
    w_buf = ck_ref.shape[-1]
    keep = w_buf - seq_len
    unit_rows = REP * seq_len
    seq_rows = KV_HEADS * unit_rows
    lane0 = (pl.program_id(0) * bt * seq_len) % kvt_ref.shape[1]
    units = [(b, g) for b in range(bt) for g in range(KV_HEADS)]

    lane = lax.broadcasted_iota(jnp.int32, (KV_WIDTH, w_buf), 1)
    for c_ref, n_ref, row0 in ((ck_ref, nk_ref, 0), (cv_ref, nv_ref, KV_WIDTH)):
        fresh = kvt_ref[row0:row0 + KV_WIDTH, :]
        for b in range(bt):
            new_shift = (keep + w_buf - (lane0 + b * seq_len)) % w_buf
            old = c_ref[b].reshape(KV_WIDTH, w_buf)
            new = jnp.where(lane < keep, pltpu.roll(old, keep, axis=1), pltpu.roll(fresh, new_shift, axis=1))
            n_ref[b] = new.reshape(KV_HEADS, HEAD_DIM, w_buf)

    def both(c_ref, n_ref, b, g):
        return jnp.concatenate([c_ref[b, g], n_ref[b, g]], axis=1).astype(BF16)

    for u, (b, g) in enumerate(units):
        rs = slice(b * seq_len, (b + 1) * seq_len)
        qs = jnp.concatenate([q_ref[rs, (g * REP + r) * HEAD_DIM:(g * REP + r + 1) * HEAD_DIM] for r in range(REP)],
                             axis=0)
        s_ref[u * unit_rows:(u + 1) * unit_rows, :] = jnp.dot(
            qs.astype(BF16), both(ck_ref, nk_ref, b, g), preferred_element_type=F32)

    col = lax.broadcasted_iota(jnp.int32, (seq_rows, 2 * w_buf), 1)
    not_sink = (col != seq_len).astype(F32)
    ones = jnp.ones((2 * w_buf, HEAD_DIM), BF16)
    for b in range(bt):
        rows_b = slice(b * seq_rows, (b + 1) * seq_rows)
        s = s_ref[rows_b, :] * not_sink + bias_ref[...]
        m = jnp.max(s, axis=-1, keepdims=True)
        p = jnp.exp(s - m).astype(BF16)
        p_ref[rows_b, :] = p
        inv_ref[rows_b, :] = 1.0 / jnp.dot(p, ones, preferred_element_type=F32)

    sink_lane = lax.broadcasted_iota(jnp.int32, (HEAD_DIM, w_buf), 1) == seq_len
    for u, (b, g) in enumerate(units):
        v_both = jnp.concatenate([jnp.where(sink_lane, 0.0, cv_ref[b, g]), nv_ref[b, g]], axis=1).astype(BF16)
        us = slice(u * unit_rows, (u + 1) * unit_rows)
        o = _dot_nt(p_ref[us, :], v_both) * inv_ref[us, :]
        rs = slice(b * seq_len, (b + 1) * seq_len)
        for r in range(REP):
            hs = slice((g * REP + r) * HEAD_DIM, (g * REP + r + 1) * HEAD_DIM)
            o_ref[rs, hs] = (o[r * seq_len:(r + 1) * seq_len, :] * ga_ref[rs, hs]).astype(BF16)


def _attn_sample(u, kv_t, cache_k, cache_v, bias, og_all, row0, batch, seq):
    bt = SAMPLE_BT
    rows = bt * seq
    w_buf = cache_k.shape[-1]
    assert w_buf == LANES and row0 % rows == 0 and LANES % rows == 0
    blk0 = row0 // rows
    score_rows = bt * Q_HEADS * seq
    cache_spec = pl.BlockSpec((bt, KV_HEADS, HEAD_DIM, w_buf), lambda i: (i, 0, 0, 0))
    return pl.pallas_call(
        functools.partial(_attn_sample_kernel, seq_len=seq),
        grid=(batch // bt,),
        in_specs=[pl.BlockSpec((rows, ATTN_WIDTH), lambda i: (blk0 + i, _COL.Q // ATTN_WIDTH)),
                  pl.BlockSpec((rows, ATTN_WIDTH), lambda i: (blk0 + i, _COL.GA // ATTN_WIDTH)),
                  pl.BlockSpec((2 * KV_WIDTH, LANES), lambda i: (0, i * rows // LANES)),
                  cache_spec, cache_spec,
                  pl.BlockSpec(bias.shape, lambda i: (0, 0)),
                  pl.BlockSpec(memory_space=pl.ANY)],
        out_specs=[pl.BlockSpec((rows, ATTN_WIDTH), lambda i: (blk0 + i, 0)), cache_spec, cache_spec],
        out_shape=[jax.ShapeDtypeStruct(og_all.shape, BF16),
                   jax.ShapeDtypeStruct(cache_k.shape, F32),
                   jax.ShapeDtypeStruct(cache_v.shape, F32)],
        scratch_shapes=[pltpu.VMEM((score_rows, 2 * w_buf), F32),
                        pltpu.VMEM((score_rows, 2 * w_buf), BF16),
                        pltpu.VMEM((score_rows, HEAD_DIM), F32)],
        input_output_aliases={6: 0},
        compiler_params=_cparams(1),
        name="attn_sample",
    )(u, u, kv_t, cache_k, cache_v, bias, og_all)


MERGE_TN = 512
EPI_SPLIT = 4


def _merge_kernel(g_ref, og_ref, ms_ref, ma_ref, ws_ref, wa_ref, o_ref, *stage_refs):
    rows = o_ref.shape[0] // EPI_SPLIT
    ssm_refs, attn_refs = stage_refs[:EPI_SPLIT], stage_refs[EPI_SPLIT:]

    def gate_chunk(n):
        rs = slice(n * rows, (n + 1) * rows)
        o_ref[rs, :] = (_sigmoid(ms_ref[rs, :]) * ssm_refs[n][...]
                        + _sigmoid(ma_ref[rs, :]) * attn_refs[n][...]).astype(BF16)

    for n in range(EPI_SPLIT):
        rs = slice(n * rows, (n + 1) * rows)
        ssm_refs[n][...] = jnp.dot(g_ref[rs, :], ws_ref[...], preferred_element_type=F32)
        attn_refs[n][...] = jnp.dot(og_ref[rs, :], wa_ref[...], preferred_element_type=F32)
        if n > 0:
            gate_chunk(n - 1)
    gate_chunk(EPI_SPLIT - 1)


def _merge(g, og, u, w_ssm, w_attn, row0, tm):
    t = g.shape[0]
    tn = MERGE_TN
    assert row0 % tm == 0
    blk0 = row0 // tm
    return pl.pallas_call(
        _merge_kernel,
        grid=(t // tm, D_MODEL // tn),
        in_specs=[pl.BlockSpec((tm, D_INNER), lambda i, j: (i, 0)),
                  pl.BlockSpec((tm, ATTN_WIDTH), lambda i, j: (i, 0)),
                  pl.BlockSpec((tm, tn), lambda i, j: (blk0 + i, _COL.MS // tn + j)),
                  pl.BlockSpec((tm, tn), lambda i, j: (blk0 + i, _COL.MA // tn + j)),
                  pl.BlockSpec((D_INNER, tn), lambda i, j: (0, j)),
                  pl.BlockSpec((ATTN_WIDTH, tn), lambda i, j: (0, j))],
        out_specs=pl.BlockSpec((tm, tn), lambda i, j: (i, j)),
        out_shape=jax.ShapeDtypeStruct((t, D_MODEL), BF16),
        scratch_shapes=[pltpu.VMEM((tm // EPI_SPLIT, tn), F32) for _ in range(2 * EPI_SPLIT)],
        compiler_params=_cparams(2, BIG_VMEM_LIMIT),
        name="merge",
    )(g, og, u, u, w_ssm, w_attn)


def _out_proj_kernel(m_ref, xp_ref, xs_ref, wo_ref, fw_ref, op_ref, os_ref, n_prompt_tiles):
    def emit(x_ref, o_ref):
        xn = x_ref[...] + jnp.dot(m_ref[...], wo_ref[...], preferred_element_type=F32)
        ms = jnp.mean(xn * xn, axis=-1, keepdims=True)
        o_ref[...] = xn * lax.rsqrt(ms + RMS_EPS) * fw_ref[...]

    @pl.when(pl.program_id(0) < n_prompt_tiles)
    def _():
        emit(xp_ref, op_ref)

    @pl.when(pl.program_id(0) >= n_prompt_tiles)
    def _():
        emit(xs_ref, os_ref)


def _out_proj(m, xp2, xs2, w_out, final_w, tm):
    tp, ts = xp2.shape[0], xs2.shape[0]
    n_p, n_s = tp // tm, ts // tm
    p_blk = lambda i: (jnp.minimum(i, n_p - 1), 0)
    s_blk = lambda i: (jnp.maximum(i - n_p, 0), 0)
    return pl.pallas_call(
        functools.partial(_out_proj_kernel, n_prompt_tiles=n_p),
        grid=(n_p + n_s,),
        in_specs=[pl.BlockSpec((tm, D_MODEL), lambda i: (i, 0)),
                  pl.BlockSpec((tm, D_MODEL), p_blk),
                  pl.BlockSpec((tm, D_MODEL), s_blk),
                  pl.BlockSpec((D_MODEL, D_MODEL), lambda i: (0, 0), pipeline_mode=pl.Buffered(1)),
                  pl.BlockSpec((1, D_MODEL), lambda i: (0, 0))],
        out_specs=[pl.BlockSpec((tm, D_MODEL), p_blk), pl.BlockSpec((tm, D_MODEL), s_blk)],
        out_shape=[jax.ShapeDtypeStruct((tp, D_MODEL), F32), jax.ShapeDtypeStruct((ts, D_MODEL), F32)],
        compiler_params=_cparams(1, BIG_VMEM_LIMIT),
        name="out_proj",
    )(m, xp2, xs2, w_out, final_w)


def _pad_lanes(v):
    return jnp.pad(v.reshape(1, -1), ((0, 0), (0, DT_PAD - v.shape[-1])))


def kernel(x_prompt, x_sample, cache_k, cache_v, state_conv, state_ssm, norm_w, w_in, conv_w, conv_b, dt_bias, a_log,
           d_skip, ssm_norm_w, w_ssm_branch, attn_sinks, w_attn_branch, w_out, rel_bias, final_norm_w):
    assert w_in.shape[0] == 1, "single-layer kernel"
    batch, seq, _ = x_prompt.shape
    dec_batch, dec_seq, _ = x_sample.shape
    w_buf = cache_k.shape[2]
    t_p, t_s = batch * seq, dec_batch * dec_seq
    assert seq % SSD_CHUNK == 0 and seq % WINDOW == 0 and dec_seq == SUBLANES and w_buf == WINDOW
    assert dec_batch % SAMPLE_BT == 0 and (t_p + t_s) % IN_TM == 0

    w_t = jnp.transpose(w_in[0])
    sc_t = jnp.transpose(state_conv[0], (1, 0, 2))
    ck_t = jnp.transpose(cache_k[0], (0, 2, 3, 1))
    cv_t = jnp.transpose(cache_v[0], (0, 2, 3, 1))

    nw = norm_w[0].reshape(1, D_MODEL)
    cw, cb = conv_w[0], conv_b[0].reshape(1, CONV_DIM)
    dtb, alog = _pad_lanes(dt_bias[0]), _pad_lanes(a_log[0])
    dskip_cols = jnp.repeat(d_skip[0], SSM_HEAD_DIM).reshape(1, D_INNER)
    snw = ssm_norm_w[0].reshape(1, D_INNER)
    w_ssm = w_ssm_branch[0].astype(BF16)
    w_attn = w_attn_branch[0].astype(BF16)
    wo = w_out[0].astype(BF16)
    fw = final_norm_w.reshape(1, D_MODEL)
    sinks = attn_sinks[0]

    xp2 = x_prompt.reshape(t_p, D_MODEL)
    xs2 = x_sample.reshape(t_s, D_MODEL)

    band_bias_t = _band_bias_t(rel_bias)
    dec_bias = _decode_bias(rel_bias, sinks, dec_seq, w_buf)

    h, dt_raw = _norm_dt(xp2, xs2, nw, w_t)
    u = _in_proj(h, w_t)
    kv_t = _kv_t(h, w_t, t_p, t_s)

    g, st_p = _ssd_prompt(u, dt_raw, cw, cb, dtb, alog, dskip_cols, snw, batch, seq)
    og = _attn_prompt(u, sinks, band_bias_t, batch, seq)
    xc_s, nsc_t = _conv_sample(u, sc_t, cw, cb, t_p, dec_batch, dec_seq)
    g, st_s = _ssd_sample(u, dt_raw, xc_s, state_ssm[0].reshape(dec_batch, D_INNER, SSM_STATE),
                          dtb, alog, dskip_cols, snw, g, t_p, dec_batch, dec_seq)
    og, nk_t, nv_t = _attn_sample(u, kv_t, ck_t, cv_t, dec_bias, og, t_p, dec_batch, dec_seq)

    m = _merge(g, og, u, w_ssm, w_attn, 0, tm=1024)
    y_p, y_s = _out_proj(m, xp2, xs2, wo, fw, tm=512)

    def seq_tail(n_rows, col0, width):
        return jnp.stack([u[(b + 1) * seq - n_rows:(b + 1) * seq, col0:col0 + width] for b in range(batch)])

    k_p = seq_tail(WINDOW, _COL.K, KV_WIDTH).reshape(1, batch, WINDOW, KV_HEADS, HEAD_DIM)
    v_p = seq_tail(WINDOW, _COL.V, KV_WIDTH).reshape(1, batch, WINDOW, KV_HEADS, HEAD_DIM)
    conv_p = seq_tail(CONV_HALO, _COL.XS, CONV_DIM)[None]
    conv_s = jnp.transpose(nsc_t, (1, 0, 2))[None]
    ssm_p = st_p.reshape(1, batch, SSM_HEADS, SSM_HEAD_DIM, SSM_STATE)
    ssm_s = st_s.reshape(1, dec_batch, SSM_HEADS, SSM_HEAD_DIM, SSM_STATE)
    k_s = jnp.transpose(nk_t, (0, 3, 1, 2))[None]
    v_s = jnp.transpose(nv_t, (0, 3, 1, 2))[None]
    return (y_p.reshape(batch, seq, D_MODEL), y_s.reshape(dec_batch, dec_seq, D_MODEL),
            k_p, v_p, conv_p, ssm_p, k_s, v_s, conv_s, ssm_s)
```

```python
import functools
import math

import numpy as np
import jax
import jax.numpy as jnp
from jax import lax
from jax.experimental import pallas as pl
from jax.experimental.pallas import tpu as pltpu

F32 = jnp.float32
BF16 = jnp.bfloat16
HIGHEST = lax.Precision.HIGHEST

D_MODEL = 2048
D_INNER = 4096
SSM_HEAD_DIM = 64
SSM_HEADS = 64
SSM_GROUPS = 8
SSM_STATE = 128
CONV_WIDTH = 4
CONV_HALO = CONV_WIDTH - 1
GN = SSM_GROUPS * SSM_STATE
CONV_DIM = D_INNER + 2 * GN
SSD_CHUNK = 128
GROUP_W = D_INNER // SSM_GROUPS
HEAD_DIM = 64
Q_HEADS = 32
KV_HEADS = 8
REP = Q_HEADS // KV_HEADS
ATTN_WIDTH = Q_HEADS * HEAD_DIM
KV_WIDTH = KV_HEADS * HEAD_DIM
WINDOW = 128
N_BUCKETS = 32
MAX_EXACT = N_BUCKETS // 2
RMS_EPS = 1e-6
ATTN_SCALE = HEAD_DIM ** -0.5

_OFF_Z = 0
_OFF_XBC = _OFF_Z + D_INNER
_OFF_DT = _OFF_XBC + CONV_DIM
_OFF_Q = _OFF_DT + SSM_HEADS
_OFF_K = _OFF_Q + ATTN_WIDTH
_OFF_V = _OFF_K + KV_WIDTH
_OFF_GA = _OFF_V + KV_WIDTH
_OFF_MS = _OFF_GA + ATTN_WIDTH
_OFF_MA = _OFF_MS + D_MODEL
_IN_COLS = _OFF_MA + D_MODEL

LANES = 128
SUBLANES = 8
DT_PAD = LANES


class _COL:
    Z = 0
    XS = Z + D_INNER
    BC = XS + D_INNER
    Q = BC + 2 * GN
    GA = Q + ATTN_WIDTH
    MS = GA + ATTN_WIDTH
    MA = MS + D_MODEL
    K = MA + D_MODEL
    V = K + KV_WIDTH
    TOTAL = V + KV_WIDTH


VMEM_LIMIT = 52 * 1024 * 1024
BIG_VMEM_LIMIT = 57 * 1024 * 1024


def _cparams(n_grid, limit=VMEM_LIMIT):
    return pltpu.CompilerParams(dimension_semantics=("arbitrary",) * n_grid, vmem_limit_bytes=limit)


LOG2E = math.log2(math.e)
NEG_LOG2E = -LOG2E


def _sigmoid(v):
    return 1.0 / (1.0 + jnp.exp2(v * NEG_LOG2E))


def _silu(v):
    return v * _sigmoid(v)


def _dot_nt(a, b):
    return lax.dot_general(a, b, (((1,), (1,)), ((), ())), preferred_element_type=F32)


NORM_TM = 1024


def _norm_dt_kernel(xp_ref, xs_ref, nw_ref, wdt_ref, h_ref, dt_ref, n_prompt_tiles):
    i = pl.program_id(0)

    def emit(x_ref):
        xf = x_ref[...]
        ms = jnp.mean(xf * xf, axis=-1, keepdims=True)
        h = (xf * lax.rsqrt(ms + RMS_EPS) * nw_ref[...]).astype(BF16)
        h_ref[...] = h
        dt_ref[...] = _dot_nt(h, wdt_ref[...].astype(BF16))

    @pl.when(i < n_prompt_tiles)
    def _():
        emit(xp_ref)

    @pl.when(i >= n_prompt_tiles)
    def _():
        emit(xs_ref)


def _norm_dt(xp2, xs2, norm_w, w_t):
    tp, ts = xp2.shape[0], xs2.shape[0]
    n_p, n_s = tp // NORM_TM, ts // NORM_TM
    return pl.pallas_call(
        functools.partial(_norm_dt_kernel, n_prompt_tiles=n_p),
        grid=(n_p + n_s,),
        in_specs=[pl.BlockSpec((NORM_TM, D_MODEL), lambda i: (jnp.minimum(i, n_p - 1), 0)),
                  pl.BlockSpec((NORM_TM, D_MODEL), lambda i: (jnp.maximum(i - n_p, 0), 0)),
                  pl.BlockSpec((1, D_MODEL), lambda i: (0, 0)),
                  pl.BlockSpec((DT_PAD, D_MODEL), lambda i: (_OFF_DT // DT_PAD, 0))],
        out_specs=[pl.BlockSpec((NORM_TM, D_MODEL), lambda i: (i, 0)),
                   pl.BlockSpec((NORM_TM, DT_PAD), lambda i: (i, 0))],
        out_shape=[jax.ShapeDtypeStruct((tp + ts, D_MODEL), BF16),
                   jax.ShapeDtypeStruct((tp + ts, DT_PAD), F32)],
        compiler_params=_cparams(1),
        name="norm_dt",
    )(xp2, xs2, norm_w, w_t)


IN_TM = 1536
IN_TN = 1024
IN_CHUNK = 512
assert _COL.TOTAL % IN_TN == 0 and IN_TN == 2 * IN_CHUNK and _OFF_DT % IN_CHUNK == 0


def _src_row(c):
    straight = _OFF_DT // IN_CHUNK
    jj = c - straight
    q_t, kv_t = ATTN_WIDTH // IN_CHUNK, 2 * KV_WIDTH // IN_CHUNK
    gate_t = (ATTN_WIDTH + 2 * D_MODEL) // IN_CHUNK
    shifted = jnp.where(jj < q_t, jj, jnp.where(jj < q_t + gate_t, jj + kv_t, jj - gate_t))
    return pl.multiple_of(jnp.where(c < straight, c * IN_CHUNK, _OFF_Q + shifted * IN_CHUNK), SSM_HEADS)


IN_NSPLIT = 4
IN_MW = IN_TM // IN_NSPLIT
J_Z1 = _COL.XS // IN_TN
J_Q0, J_Q1 = _COL.Q // IN_TN, _COL.GA // IN_TN
J_GA0, J_GA1 = _COL.GA // IN_TN, _COL.MS // IN_TN
assert _COL.XS % IN_TN == 0 and _COL.Q % IN_TN == 0 and _COL.GA % IN_TN == 0 and _COL.MS % IN_TN == 0
assert math.log2(HEAD_DIM) % 2 == 0, "the folded attention scale must be a power of two"


def _in_proj_kernel(h_ref, w0_ref, w1_ref, o_ref, wbf_ref, *stage_refs):
    j = pl.program_id(0)

    @pl.when(pl.program_id(1) == 0)
    def _():
        scale = jnp.where((j >= J_Q0) & (j < J_Q1), ATTN_SCALE, 1.0)
        wbf_ref[0:IN_CHUNK, :] = (w0_ref[...] * scale).astype(BF16)
        wbf_ref[IN_CHUNK:IN_TN, :] = (w1_ref[...] * scale).astype(BF16)

    is_silu_tile = (j < J_Z1) | ((j >= J_GA0) & (j < J_GA1))

    @pl.when(is_silu_tile)
    def _():
        def silu_chunk(n):
            o_ref[n * IN_MW:(n + 1) * IN_MW, :] = _silu(stage_refs[n][...])

        for n in range(IN_NSPLIT):
            stage_refs[n][...] = _dot_nt(h_ref[n * IN_MW:(n + 1) * IN_MW, :], wbf_ref[...])
            if n > 0:
                silu_chunk(n - 1)
        silu_chunk(IN_NSPLIT - 1)

    @pl.when(jnp.logical_not(is_silu_tile))
    def _():
        o_ref[...] = _dot_nt(h_ref[...], wbf_ref[...])


def _in_proj(h, w_t):
    t = h.shape[0]
    return pl.pallas_call(
        _in_proj_kernel,
        grid=(_COL.TOTAL // IN_TN, t // IN_TM),
        in_specs=[pl.BlockSpec((IN_TM, D_MODEL), lambda j, i: (i, 0)),
                  pl.BlockSpec((pl.Element(IN_CHUNK), pl.Element(D_MODEL)), lambda j, i: (_src_row(2 * j), 0)),
                  pl.BlockSpec((pl.Element(IN_CHUNK), pl.Element(D_MODEL)), lambda j, i: (_src_row(2 * j + 1), 0))],
        out_specs=pl.BlockSpec((IN_TM, IN_TN), lambda j, i: (i, j)),
        out_shape=jax.ShapeDtypeStruct((t, _COL.TOTAL), F32),
        scratch_shapes=[pltpu.VMEM((IN_TN, D_MODEL), BF16)]
        + [pltpu.VMEM((IN_MW, IN_TN), F32) for _ in range(IN_NSPLIT)],
        compiler_params=_cparams(2, BIG_VMEM_LIMIT),
        name="in_proj",
    )(h, w_t, w_t)


def _kv_t_kernel(w_ref, h_ref, o_ref):
    o_ref[...] = _dot_nt(w_ref[...].astype(BF16), h_ref[...])


def _kv_t(h, w_t, row0, rows):
    assert row0 % rows == 0
    return pl.pallas_call(
        _kv_t_kernel,
        grid=(1,),
        in_specs=[pl.BlockSpec((pl.Element(2 * KV_WIDTH), pl.Element(D_MODEL)), lambda i: (_OFF_K, 0)),
                  pl.BlockSpec((rows, D_MODEL), lambda i: (row0 // rows, 0))],
        out_specs=pl.BlockSpec((2 * KV_WIDTH, rows), lambda i: (0, 0)),
        out_shape=jax.ShapeDtypeStruct((2 * KV_WIDTH, rows), F32),
        compiler_params=_cparams(1),
        name="kv_t",
    )(w_t, h)


def _bucket_of_dist(dist):
    n = np.maximum(dist, 0)
    nf = np.maximum(n, 1).astype(np.float32)
    large = MAX_EXACT + (np.log(nf / MAX_EXACT) / math.log(WINDOW / MAX_EXACT)
                         * (N_BUCKETS - MAX_EXACT)).astype(np.int32)
    large = np.minimum(large, N_BUCKETS - 1)
    bucket = np.where(n < MAX_EXACT, n, large)
    return np.where((dist >= 0) & (dist <= WINDOW), bucket, -1).astype(np.int32)


def _bias_of_bucket(bk, table_ref, h):
    acc = jnp.full(bk.shape, -jnp.inf, F32)
    for b in range(N_BUCKETS):
        acc = jnp.where(bk == b, table_ref[b, h], acc)
    return acc


BAND_BIAS_HEADS = 8


def _band_bias_kernel(table_ref, bucket_ref, o_ref):
    for hh in range(BAND_BIAS_HEADS):
        o_ref[hh] = _bias_of_bucket(bucket_ref[...], table_ref, pl.program_id(0) * BAND_BIAS_HEADS + hh)


def _band_bias_t(rel_bias):
    kj = np.arange(2 * WINDOW)[:, None]
    qi = np.arange(WINDOW)[None, :]
    bucket = jnp.asarray(_bucket_of_dist(qi + WINDOW - kj))
    return pl.pallas_call(
        _band_bias_kernel,
        grid=(Q_HEADS // BAND_BIAS_HEADS,),
        in_specs=[pl.BlockSpec(memory_space=pltpu.SMEM),
                  pl.BlockSpec((2 * WINDOW, WINDOW), lambda h: (0, 0))],
        out_specs=pl.BlockSpec((BAND_BIAS_HEADS, 2 * WINDOW, WINDOW), lambda h: (h, 0, 0)),
        out_shape=jax.ShapeDtypeStruct((Q_HEADS, 2 * WINDOW, WINDOW), F32),
        compiler_params=_cparams(1),
        name="band_bias",
    )(rel_bias, bucket)


def _decode_bias_kernel(table_ref, sink_ref, bucket_ref, o_ref, seq):
    g = pl.program_id(0)
    sink_lane = lax.broadcasted_iota(jnp.int32, bucket_ref.shape, 1) == seq
    for r in range(REP):
        h = g * REP + r
        bias = _bias_of_bucket(bucket_ref[...], table_ref, h)
        o_ref[r * seq:(r + 1) * seq, :] = jnp.where(sink_lane, sink_ref[h], bias)


def _decode_bias(rel_bias, sinks, seq, w_buf):
    l = np.arange(seq)[:, None]
    j = np.arange(w_buf)[None, :]
    dist_old = np.where((j < seq), l + w_buf - j, -1)
    dist_new = np.where(j < w_buf - seq, l + w_buf - (j + seq), l - (j - (w_buf - seq)))
    bucket = jnp.asarray(np.concatenate([_bucket_of_dist(dist_old), _bucket_of_dist(dist_new)], axis=1))
    return pl.pallas_call(
        functools.partial(_decode_bias_kernel, seq=seq),
        grid=(KV_HEADS,),
        in_specs=[pl.BlockSpec(memory_space=pltpu.SMEM), pl.BlockSpec(memory_space=pltpu.SMEM),
                  pl.BlockSpec(bucket.shape, lambda g: (0, 0))],
        out_specs=pl.BlockSpec((REP * seq, 2 * w_buf), lambda g: (g, 0)),
        out_shape=jax.ShapeDtypeStruct((Q_HEADS * seq, 2 * w_buf), F32),
        compiler_params=_cparams(1),
        name="decode_bias",
    )(rel_bias, sinks, bucket)


CONV_STRIP = 512


def _conv_silu(ext_ref, base, rows, cw_ref, cb_ref, xc_ref, out_row):
    n_tiles = rows // SUBLANES
    sub = lax.broadcasted_iota(jnp.int32, (SUBLANES, CONV_STRIP), 0)
    from_prev = [sub < k for k in range(CONV_WIDTH)]
    for c0 in range(0, CONV_DIM, CONV_STRIP):
        cs = slice(c0, c0 + CONV_STRIP)
        bias = jnp.broadcast_to(cb_ref[:, cs], (SUBLANES, CONV_STRIP))
        taps = [jnp.broadcast_to(cw_ref[CONV_HALO - k:CONV_WIDTH - k, cs], (SUBLANES, CONV_STRIP))
                for k in range(CONV_WIDTH)]
        prev = ext_ref[base - SUBLANES:base, cs]
        prev_rot = [None] + [pltpu.roll(prev, k, axis=0) for k in range(1, CONV_WIDTH)]
        for t in range(n_tiles):
            cur = ext_ref[base + t * SUBLANES:base + (t + 1) * SUBLANES, cs]
            acc = bias + taps[0] * cur
            cur_rot = [None]
            for k in range(1, CONV_WIDTH):
                cur_rot.append(pltpu.roll(cur, k, axis=0))
                acc = acc + taps[k] * jnp.where(from_prev[k], prev_rot[k], cur_rot[k])
            xc_ref[out_row + t * SUBLANES:out_row + (t + 1) * SUBLANES, cs] = _silu(acc)
            prev_rot = cur_rot


CONV_S_SEQS = 16


def _conv_sample_kernel(xs_ref, bc_ref, sc_ref, cw_ref, cb_ref, xc_ref, nsc_ref, ext_ref, seq_len):
    slab = SUBLANES + seq_len
    for b in range(CONV_S_SEQS):
        base = b * slab + SUBLANES
        for k in range(CONV_HALO):
            ext_ref[base - CONV_HALO + k:base - CONV_HALO + k + 1, :] = sc_ref[k, b:b + 1, :]
        ext_ref[base:base + seq_len, 0:D_INNER] = xs_ref[b * seq_len:(b + 1) * seq_len, :]
        ext_ref[base:base + seq_len, D_INNER:CONV_DIM] = bc_ref[b * seq_len:(b + 1) * seq_len, :]
        for k in range(CONV_HALO):
            row = base + seq_len - CONV_HALO + k
            nsc_ref[k, b:b + 1, :] = ext_ref[row:row + 1, :]
    sub = lax.broadcasted_iota(jnp.int32, (SUBLANES, CONV_STRIP), 0)
    for c0 in range(0, CONV_DIM, CONV_STRIP):
        cs = slice(c0, c0 + CONV_STRIP)
        bias = jnp.broadcast_to(cb_ref[:, cs], (seq_len, CONV_STRIP))
        taps = [jnp.broadcast_to(cw_ref[CONV_HALO - k:CONV_WIDTH - k, cs], (seq_len, CONV_STRIP))
                for k in range(CONV_WIDTH)]
        for b in range(CONV_S_SEQS):
            base = b * slab + SUBLANES
            prev = ext_ref[base - SUBLANES:base, cs]
            cur = ext_ref[base:base + seq_len, cs]
            acc = bias + taps[0] * cur
            for k in range(1, CONV_WIDTH):
                acc = acc + taps[k] * jnp.where(sub < k, pltpu.roll(prev, k, axis=0), pltpu.roll(cur, k, axis=0))
            xc_ref[b * seq_len:(b + 1) * seq_len, cs] = _silu(acc)


def _conv_sample(u, sc_t, conv_w, conv_b, row0, batch, seq):
    assert seq == SUBLANES and batch % CONV_S_SEQS == 0
    rows = CONV_S_SEQS * seq
    assert row0 % rows == 0
    blk0 = row0 // rows
    full = lambda shape: pl.BlockSpec(shape, lambda i: (0,) * len(shape))
    return pl.pallas_call(
        functools.partial(_conv_sample_kernel, seq_len=seq),
        grid=(batch // CONV_S_SEQS,),
        in_specs=[pl.BlockSpec((rows, D_INNER), lambda i: (blk0 + i, _COL.XS // D_INNER)),
                  pl.BlockSpec((rows, 2 * GN), lambda i: (blk0 + i, _COL.BC // (2 * GN))),
                  pl.BlockSpec((CONV_HALO, CONV_S_SEQS, CONV_DIM), lambda i: (0, i, 0)),
                  full((CONV_WIDTH, CONV_DIM)), full((1, CONV_DIM))],
        out_specs=[pl.BlockSpec((rows, CONV_DIM), lambda i: (i, 0)),
                   pl.BlockSpec((CONV_HALO, CONV_S_SEQS, CONV_DIM), lambda i: (0, i, 0))],
        out_shape=[jax.ShapeDtypeStruct((batch * seq, CONV_DIM), F32),
                   jax.ShapeDtypeStruct((CONV_HALO, batch, CONV_DIM), F32)],
        scratch_shapes=[pltpu.VMEM((CONV_S_SEQS * (SUBLANES + seq), CONV_DIM), F32)],
        compiler_params=_cparams(1),
        name="conv_sample",
    )(u, u, sc_t, conv_w, conv_b)


def _ssd_core(rows, same_seq, zs_ref, dt_ref, dtb_ref, alog_ref, dskip_ref, nw_ref, xs_ref, bc_ref, xd_ref, y_ref,
              yoff_fn, state_fn, g_ref):
    li = lax.broadcasted_iota(jnp.int32, (rows, rows), 0)
    si = lax.broadcasted_iota(jnp.int32, (rows, rows), 1)
    causal = same_seq & (si <= li)
    causal_bias = jnp.where(causal, 0.0, -jnp.inf)
    lane = lax.broadcasted_iota(jnp.int32, (rows, LANES), 1)
    first_half = lane < SSM_HEAD_DIM

    dt = jax.nn.softplus(dt_ref[...] + dtb_ref[...])
    a_neg = -jnp.exp(alog_ref[...])
    d_a = dt * a_neg
    acs = jnp.dot(causal.astype(F32), d_a, precision=HIGHEST, preferred_element_type=F32)
    a_end = jnp.dot(same_seq.astype(F32), d_a, precision=HIGHEST, preferred_element_type=F32)
    log_dt = jnp.log(dt)
    acs2 = acs * LOG2E
    seg_rows2 = ((acs - log_dt) * LOG2E).T
    log_w_state = log_dt + a_end - acs
    log_w_state2 = log_w_state * LOG2E
    cd_rows = jnp.exp(a_end)
    e_acs_tab, w_state_tab = jnp.exp(acs), jnp.exp(log_w_state)

    def head_cols(m, j):
        return (jnp.broadcast_to(m[:, 2 * j:2 * j + 1], (rows, LANES)),
                jnp.broadcast_to(m[:, 2 * j + 1:2 * j + 2], (rows, LANES)))

    pairs_per_group = SSM_HEADS // SSM_GROUPS // 2
    for g in range(SSM_GROUPS):
        b_g = bc_ref[:, g * SSM_STATE:(g + 1) * SSM_STATE].astype(BF16)
        c_g = bc_ref[:, GN + g * SSM_STATE:GN + (g + 1) * SSM_STATE].astype(BF16)
        cb = _dot_nt(c_g, b_g)
        yoff = yoff_fn(g, c_g)
        for jj in range(pairs_per_group):
            j = g * pairs_per_group + jj
            ps = slice(j * LANES, (j + 1) * LANES)
            xs_pair = xs_ref[:, ps]
            xs_bf16 = xs_pair.astype(BF16)
            acs_cols = head_cols(acs2, j)
            parts = []
            for hh in range(2):
                h = 2 * j + hh
                col = acs_cols[hh] if rows == LANES else acs2[:, h:h + 1]
                seg2 = col - seg_rows2[h:h + 1, :]
                decay_dt = jnp.exp2(seg2 + causal_bias)
                m = (cb * decay_dt).astype(BF16)
                parts.append(jnp.dot(m, xs_bf16, preferred_element_type=F32))
            yd = jnp.where(first_half, parts[0], parts[1])
            if rows == LANES:
                e_acs = jnp.exp2(jnp.where(first_half, *acs_cols))
                w_state = jnp.exp2(jnp.where(first_half, *head_cols(log_w_state2, j)))
            else:
                e_acs = jnp.where(first_half, *head_cols(e_acs_tab, j))
                w_state = jnp.where(first_half, *head_cols(w_state_tab, j))
            xd_ref[:, ps] = (xs_pair * w_state).astype(xd_ref.dtype)
            y_ref[:, ps] = yd + yoff[:, jj * LANES:(jj + 1) * LANES] * e_acs + dskip_ref[:, ps] * xs_pair
        state_fn(g, cd_rows)

    for g in range(SSM_GROUPS):
        gs = slice(g * GROUP_W, (g + 1) * GROUP_W)
        gg = y_ref[:, gs] * zs_ref[:, gs]
        ms = jnp.mean(gg * gg, axis=-1, keepdims=True)
        g_ref[:, gs] = (gg * lax.rsqrt(ms + RMS_EPS) * nw_ref[:, gs]).astype(BF16)


def _ssd_prompt_kernel(zs_ref, xs_raw_ref, bc_raw_ref, dt_ref, cw_ref, cb_ref, dtb_ref, alog_ref, dskip_ref, nw_ref,
                       g_ref, st_ref, ext_ref, xc_ref, xd_ref, y_ref, stt_ref):
    c = pl.program_id(1)
    rows = SSD_CHUNK

    @pl.when(c == 0)
    def _():
        ext_ref[0:SUBLANES, :] = jnp.zeros((SUBLANES, CONV_DIM), F32)
        stt_ref[...] = jnp.zeros(stt_ref.shape, F32)

    @pl.when(c != 0)
    def _():
        ext_ref[0:SUBLANES, :] = ext_ref[rows:rows + SUBLANES, :]

    ext_ref[SUBLANES:SUBLANES + rows, 0:D_INNER] = xs_raw_ref[...]
    ext_ref[SUBLANES:SUBLANES + rows, D_INNER:CONV_DIM] = bc_raw_ref[...]
    _conv_silu(ext_ref, SUBLANES, rows, cw_ref, cb_ref, xc_ref, 0)
    xs_ref = xc_ref.at[:, 0:D_INNER]
    bc_ref = xc_ref.at[:, D_INNER:CONV_DIM]

    def yoff_fn(g, c_g):
        return jnp.dot(c_g, stt_ref[:, g * GROUP_W:(g + 1) * GROUP_W].astype(BF16), preferred_element_type=F32)

    def state_fn(g, cd_rows):
        b_g = bc_ref[:, g * SSM_STATE:(g + 1) * SSM_STATE].astype(BF16)
        xd_g = xd_ref[:, g * GROUP_W:(g + 1) * GROUP_W]
        new = lax.dot_general(b_g, xd_g, (((0,), (0,)), ((), ())), preferred_element_type=F32)
        lane = lax.broadcasted_iota(jnp.int32, (SSM_STATE, LANES), 1)
        for jj in range(GROUP_W // LANES):
            j = g * (GROUP_W // LANES) + jj
            ps = slice(j * LANES, (j + 1) * LANES)
            cd = jnp.where(lane < SSM_HEAD_DIM,
                           jnp.broadcast_to(cd_rows[0:1, 2 * j:2 * j + 1], (SSM_STATE, LANES)),
                           jnp.broadcast_to(cd_rows[0:1, 2 * j + 1:2 * j + 2], (SSM_STATE, LANES)))
            stt_ref[:, ps] = stt_ref[:, ps] * cd + new[:, jj * LANES:(jj + 1) * LANES]

    same_seq = jnp.full((rows, rows), True)
    _ssd_core(rows, same_seq, zs_ref, dt_ref, dtb_ref, alog_ref, dskip_ref, nw_ref, xs_ref, bc_ref, xd_ref, y_ref,
              yoff_fn, state_fn, g_ref)

    @pl.when(c == pl.num_programs(1) - 1)
    def _():
        for j in range(D_INNER // LANES):
            st_ref[0, j * LANES:(j + 1) * LANES, :] = stt_ref[:, j * LANES:(j + 1) * LANES].T


def _ssd_prompt(u, dt_raw, conv_w, conv_b, dt_bias, a_log, dskip_cols, ssm_norm_w, batch, seq):
    nc = seq // SSD_CHUNK
    rows = SSD_CHUNK
    row_blk = lambda b, c: b * nc + c
    full = lambda shape: pl.BlockSpec(shape, lambda b, c: (0,) * len(shape))
    return pl.pallas_call(
        _ssd_prompt_kernel,
        grid=(batch, nc),
        in_specs=[pl.BlockSpec((rows, D_INNER), lambda b, c: (row_blk(b, c), _COL.Z // D_INNER)),
                  pl.BlockSpec((rows, D_INNER), lambda b, c: (row_blk(b, c), _COL.XS // D_INNER)),
                  pl.BlockSpec((rows, 2 * GN), lambda b, c: (row_blk(b, c), _COL.BC // (2 * GN))),
                  pl.BlockSpec((rows, DT_PAD), lambda b, c: (row_blk(b, c), 0)),
                  full((CONV_WIDTH, CONV_DIM)), full((1, CONV_DIM)), full((1, DT_PAD)), full((1, DT_PAD)),
                  full((1, D_INNER)), full((1, D_INNER))],
        out_specs=[pl.BlockSpec((rows, D_INNER), lambda b, c: (row_blk(b, c), 0)),
                   pl.BlockSpec((1, D_INNER, SSM_STATE), lambda b, c: (b, 0, 0))],
        out_shape=[jax.ShapeDtypeStruct((batch * seq, D_INNER), BF16),
                   jax.ShapeDtypeStruct((batch, D_INNER, SSM_STATE), F32)],
        scratch_shapes=[pltpu.VMEM((rows + SUBLANES, CONV_DIM), F32),
                        pltpu.VMEM((rows, CONV_DIM), F32),
                        pltpu.VMEM((rows, D_INNER), BF16),
                        pltpu.VMEM((rows, D_INNER), F32),
                        pltpu.VMEM((SSM_STATE, D_INNER), F32)],
        compiler_params=_cparams(2),
        name="ssd_prompt",
    )(u, u, u, dt_raw, conv_w, conv_b, dt_bias, a_log, dskip_cols, ssm_norm_w)


SAMPLE_BT = 4


def _ssd_sample_kernel(zs_ref, xs_ref, bc_ref, dt_ref, st_ref, dtb_ref, alog_ref, dskip_ref, nw_ref,
                       g_ref, nst_ref, xd_ref, y_ref, yoff_ref, seq_len):
    bt = SAMPLE_BT
    rows = bt * seq_len
    heads_per_group = SSM_HEADS // SSM_GROUPS

    for b in range(bt):
        rs = slice(b * seq_len, (b + 1) * seq_len)
        for g in range(SSM_GROUPS):
            c_bg = bc_ref[rs, GN + g * SSM_STATE:GN + (g + 1) * SSM_STATE].astype(BF16)
            s_bg = st_ref[b, g * GROUP_W:(g + 1) * GROUP_W, :].astype(BF16)
            yoff_ref[rs, g * GROUP_W:(g + 1) * GROUP_W] = _dot_nt(c_bg, s_bg)

    def yoff_fn(g, c_g):
        return yoff_ref[:, g * GROUP_W:(g + 1) * GROUP_W]

    def state_fn(g, cd_rows):
        for b in range(bt):
            rs = slice(b * seq_len, (b + 1) * seq_len)
            b_bg = bc_ref[rs, g * SSM_STATE:(g + 1) * SSM_STATE].astype(BF16)
            xd_bg = xd_ref[rs, g * GROUP_W:(g + 1) * GROUP_W].astype(BF16)
            new = lax.dot_general(xd_bg, b_bg, (((0,), (0,)), ((), ())), preferred_element_type=F32)
            for hh in range(heads_per_group):
                h = g * heads_per_group + hh
                hs = slice(h * SSM_HEAD_DIM, (h + 1) * SSM_HEAD_DIM)
                cd = jnp.broadcast_to(cd_rows[b * seq_len:b * seq_len + 1, h:h + 1], (SSM_HEAD_DIM, SSM_STATE))
                nst_ref[b, hs, :] = st_ref[b, hs, :] * cd + new[hh * SSM_HEAD_DIM:(hh + 1) * SSM_HEAD_DIM, :]

    li = lax.broadcasted_iota(jnp.int32, (rows, rows), 0)
    si = lax.broadcasted_iota(jnp.int32, (rows, rows), 1)
    same_seq = (li // seq_len) == (si // seq_len)
    _ssd_core(rows, same_seq, zs_ref, dt_ref, dtb_ref, alog_ref, dskip_ref, nw_ref, xs_ref, bc_ref, xd_ref, y_ref,
              yoff_fn, state_fn, g_ref)


def _ssd_sample(u, dt_raw, xc, state, dt_bias, a_log, dskip_cols, ssm_norm_w, row0, batch, seq):
    bt = SAMPLE_BT
    rows = bt * seq
    assert row0 % rows == 0
    blk0 = row0 // rows
    full = lambda shape: pl.BlockSpec(shape, lambda i: (0,) * len(shape))
    return pl.pallas_call(
        functools.partial(_ssd_sample_kernel, seq_len=seq),
        grid=(batch // bt,),
        in_specs=[pl.BlockSpec((rows, D_INNER), lambda i: (blk0 + i, _COL.Z // D_INNER)),
                  pl.BlockSpec((rows, D_INNER), lambda i: (i, 0)),
                  pl.BlockSpec((rows, 2 * GN), lambda i: (i, D_INNER // (2 * GN))),
                  pl.BlockSpec((rows, DT_PAD), lambda i: (blk0 + i, 0)),
                  pl.BlockSpec((bt, D_INNER, SSM_STATE), lambda i: (i, 0, 0)),
                  full((1, DT_PAD)), full((1, DT_PAD)), full((1, D_INNER)), full((1, D_INNER))],
        out_specs=[pl.BlockSpec((rows, D_INNER), lambda i: (i, 0)),
                   pl.BlockSpec((bt, D_INNER, SSM_STATE), lambda i: (i, 0, 0))],
        out_shape=[jax.ShapeDtypeStruct((batch * seq, D_INNER), BF16),
                   jax.ShapeDtypeStruct((batch, D_INNER, SSM_STATE), F32)],
        scratch_shapes=[pltpu.VMEM((rows, D_INNER), F32),
                        pltpu.VMEM((rows, D_INNER), F32),
                        pltpu.VMEM((rows, D_INNER), F32)],
        compiler_params=_cparams(1),
        name="ssd_sample",
    )(u, xc, xc, dt_raw, state, dt_bias, a_log, dskip_cols, ssm_norm_w)


ATTN_BLOCKS = 4
ONES_ROWS = 16


def _attn_prompt_kernel(sink_ref, q_ref, kp_ref, ko_ref, vp_ref, vo_ref, ga_ref, bias_ref, o_ref):
    n = pl.program_id(1)
    no_prev = [jnp.where(n == 0, jnp.inf, 0.0)] + [0.0] * (ATTN_BLOCKS - 1)

    def rows(s):
        return slice(s * WINDOW, (s + 1) * WINDOW)

    def scores(s, g):
        ks = slice(g * HEAD_DIM, (g + 1) * HEAD_DIM)
        k_prev = kp_ref[:, ks] if s == 0 else ko_ref[rows(s - 1), ks]
        v_prev = vp_ref[:, ks] if s == 0 else vo_ref[rows(s - 1), ks]
        kk = jnp.concatenate([k_prev, ko_ref[rows(s), ks]], axis=0).astype(BF16)
        vv_t = jnp.concatenate([jnp.concatenate([v_prev, vo_ref[rows(s), ks]], axis=0).T,
                                jnp.ones((ONES_ROWS, 2 * WINDOW), F32)], axis=0).astype(BF16)
        s_list = []
        for r in range(REP):
            h = g * REP + r
            q = q_ref[rows(s), h * HEAD_DIM:(h + 1) * HEAD_DIM].astype(BF16)
            s_list.append(_dot_nt(kk, q))
        return vv_t, s_list

    def finish(s, g, vv_t, s_list):
        probs = []
        for r in range(REP):
            h = g * REP + r
            s_prev = s_list[r][0:WINDOW] + bias_ref[h, 0:WINDOW, :]
            s_own = s_list[r][WINDOW:2 * WINDOW] + bias_ref[h, WINDOW:2 * WINDOW, :]
            sink = sink_ref[h]
            m = jnp.maximum(jnp.maximum(jnp.max(s_prev, axis=0, keepdims=True) - no_prev[s],
                                        jnp.max(s_own, axis=0, keepdims=True)), sink)
            p = jnp.concatenate([jnp.exp(s_prev - (m + no_prev[s])), jnp.exp(s_own - m)], axis=0)
            probs.append((p.astype(BF16), jnp.exp(sink - m)))
        o_t = []
        for p, p_sink in probs:
            pv = jnp.dot(vv_t, p, preferred_element_type=F32)
            o_t.append(pv[0:HEAD_DIM] / (pv[HEAD_DIM:HEAD_DIM + 1] + p_sink))
        for rp in range(REP // 2):
            o_pair = jnp.concatenate(o_t[2 * rp:2 * rp + 2], axis=0).T
            ps = slice((g * REP + 2 * rp) * HEAD_DIM, (g * REP + 2 * rp + 2) * HEAD_DIM)
            o_ref[rows(s), ps] = (o_pair * ga_ref[rows(s), ps]).astype(BF16)

    units = [(s, g) for s in range(ATTN_BLOCKS) for g in range(KV_HEADS)]
    pending = scores(*units[0])
    for idx, unit in enumerate(units):
        cur = pending
        if idx + 1 < len(units):
            pending = scores(*units[idx + 1])
        finish(*unit, *cur)


def _attn_prompt(u, sinks, bias, batch, seq):
    rows = ATTN_BLOCKS * WINDOW
    assert seq % rows == 0
    ns = seq // rows
    row = lambda b, n: b * ns + n
    prev = lambda b, n: jnp.maximum((b * ns + n) * ATTN_BLOCKS - 1, 0)
    return pl.pallas_call(
        _attn_prompt_kernel,
        grid=(batch, ns),
        in_specs=[pl.BlockSpec(memory_space=pltpu.SMEM),
                  pl.BlockSpec((rows, ATTN_WIDTH), lambda b, n: (row(b, n), _COL.Q // ATTN_WIDTH)),
                  pl.BlockSpec((WINDOW, KV_WIDTH), lambda b, n: (prev(b, n), _COL.K // KV_WIDTH)),
                  pl.BlockSpec((rows, KV_WIDTH), lambda b, n: (row(b, n), _COL.K // KV_WIDTH)),
                  pl.BlockSpec((WINDOW, KV_WIDTH), lambda b, n: (prev(b, n), _COL.V // KV_WIDTH)),
                  pl.BlockSpec((rows, KV_WIDTH), lambda b, n: (row(b, n), _COL.V // KV_WIDTH)),
                  pl.BlockSpec((rows, ATTN_WIDTH), lambda b, n: (row(b, n), _COL.GA // ATTN_WIDTH)),
                  pl.BlockSpec((Q_HEADS, 2 * WINDOW, WINDOW), lambda b, n: (0, 0, 0))],
        out_specs=pl.BlockSpec((rows, ATTN_WIDTH), lambda b, n: (row(b, n), 0)),
        out_shape=jax.ShapeDtypeStruct((batch * seq, ATTN_WIDTH), BF16),
        compiler_params=_cparams(2),
        name="attn_prompt",
    )(sinks, u, u, u, u, u, u, bias)


def _attn_sample_kernel(q_ref, ga_ref, kvt_ref, ck_ref, cv_ref, bias_ref, o_ref, nk_ref, nv_ref,
                        s_ref, p_ref, inv_ref, seq_len):
    bt = SAMPLE_BT
    w_buf = ck_ref.shape[-1]
    keep = w_buf - seq_len
    unit_rows = REP * seq_len
    seq_rows = KV_HEADS * unit_rows
    lane0 = (pl.program_id(0) * bt * seq_len) % kvt_ref.shape[1]
    units = [(b, g) for b in range(bt) for g in range(KV_HEADS)]

    lane = lax.broadcasted_iota(jnp.int32, (KV_WIDTH, w_buf), 1)
    for c_ref, n_ref, row0 in ((ck_ref, nk_ref, 0), (cv_ref, nv_ref, KV_WIDTH)):
        fresh = kvt_ref[row0:row0 + KV_WIDTH, :]
        for b in range(bt):
            new_shift = (keep + w_buf - (lane0 + b * seq_len)) % w_buf
            old = c_ref[b].reshape(KV_WIDTH, w_buf)
            new = jnp.where(lane < keep, pltpu.roll(old, keep, axis=1), pltpu.roll(fresh, new_shift, axis=1))
            n_ref[b] = new.reshape(KV_HEADS, HEAD_DIM, w_buf)

    def both(c_ref, n_ref, b, g):
        return jnp.concatenate([c_ref[b, g], n_ref[b, g]], axis=1).astype(BF16)

    for u, (b, g) in enumerate(units):
        rs = slice(b * seq_len, (b + 1) * seq_len)
        qs = jnp.concatenate([q_ref[rs, (g * REP + r) * HEAD_DIM:(g * REP + r + 1) * HEAD_DIM] for r in range(REP)],
                             axis=0)
        s_ref[u * unit_rows:(u + 1) * unit_rows, :] = jnp.dot(
            qs.astype(BF16), both(ck_ref, nk_ref, b, g), preferred_element_type=F32)

    col = lax.broadcasted_iota(jnp.int32, (seq_rows, 2 * w_buf), 1)
    not_sink = (col != seq_len).astype(F32)
    ones = jnp.ones((2 * w_buf, HEAD_DIM), BF16)
    for b in range(bt):
        rows_b = slice(b * seq_rows, (b + 1) * seq_rows)
        s = s_ref[rows_b, :] * not_sink + bias_ref[...]
        m = jnp.max(s, axis=-1, keepdims=True)
        p = jnp.exp(s - m).astype(BF16)
        p_ref[rows_b, :] = p
        inv_ref[rows_b, :] = 1.0 / jnp.dot(p, ones, preferred_element_type=F32)

    sink_lane = lax.broadcasted_iota(jnp.int32, (HEAD_DIM, w_buf), 1) == seq_len
    for u, (b, g) in enumerate(units):
        v_both = jnp.concatenate([jnp.where(sink_lane, 0.0, cv_ref[b, g]), nv_ref[b, g]], axis=1).astype(BF16)
        us = slice(u * unit_rows, (u + 1) * unit_rows)
        o = _dot_nt(p_ref[us, :], v_both) * inv_ref[us, :]
        rs = slice(b * seq_len, (b + 1) * seq_len)
        for r in range(REP):
            hs = slice((g * REP + r) * HEAD_DIM, (g * REP + r + 1) * HEAD_DIM)
            o_ref[rs, hs] = (o[r * seq_len:(r + 1) * seq_len, :] * ga_ref[rs, hs]).astype(BF16)


def _attn_sample(u, kv_t, cache_k, cache_v, bias, row0, batch, seq):
    bt = SAMPLE_BT
    rows = bt * seq
    w_buf = cache_k.shape[-1]
    assert w_buf == LANES and row0 % rows == 0 and LANES % rows == 0
    blk0 = row0 // rows
    score_rows = bt * Q_HEADS * seq
    cache_spec = pl.BlockSpec((bt, KV_HEADS, HEAD_DIM, w_buf), lambda i: (i, 0, 0, 0))
    return pl.pallas_call(
        functools.partial(_attn_sample_kernel, seq_len=seq),
        grid=(batch // bt,),
        in_specs=[pl.BlockSpec((rows, ATTN_WIDTH), lambda i: (blk0 + i, _COL.Q // ATTN_WIDTH)),
                  pl.BlockSpec((rows, ATTN_WIDTH), lambda i: (blk0 + i, _COL.GA // ATTN_WIDTH)),
                  pl.BlockSpec((2 * KV_WIDTH, LANES), lambda i: (0, i * rows // LANES)),
                  cache_spec, cache_spec,
                  pl.BlockSpec(bias.shape, lambda i: (0, 0))],
        out_specs=[pl.BlockSpec((rows, ATTN_WIDTH), lambda i: (i, 0)), cache_spec, cache_spec],
        out_shape=[jax.ShapeDtypeStruct((batch * seq, ATTN_WIDTH), BF16),
                   jax.ShapeDtypeStruct(cache_k.shape, F32),
                   jax.ShapeDtypeStruct(cache_v.shape, F32)],
        scratch_shapes=[pltpu.VMEM((score_rows, 2 * w_buf), F32),
                        pltpu.VMEM((score_rows, 2 * w_buf), BF16),
                        pltpu.VMEM((score_rows, HEAD_DIM), F32)],
        compiler_params=_cparams(1),
        name="attn_sample",
    )(u, u, kv_t, cache_k, cache_v, bias)


MERGE_TN = 512
EPI_SPLIT = 4


def _merge_kernel(g_ref, og_ref, ms_ref, ma_ref, ws_ref, wa_ref, o_ref, *stage_refs):
    rows = o_ref.shape[0] // EPI_SPLIT
    ssm_refs, attn_refs = stage_refs[:EPI_SPLIT], stage_refs[EPI_SPLIT:]

    def gate_chunk(n):
        rs = slice(n * rows, (n + 1) * rows)
        o_ref[rs, :] = (_sigmoid(ms_ref[rs, :]) * ssm_refs[n][...]
                        + _sigmoid(ma_ref[rs, :]) * attn_refs[n][...]).astype(BF16)

    for n in range(EPI_SPLIT):
        rs = slice(n * rows, (n + 1) * rows)
        ssm_refs[n][...] = jnp.dot(g_ref[rs, :], ws_ref[...], preferred_element_type=F32)
        attn_refs[n][...] = jnp.dot(og_ref[rs, :], wa_ref[...], preferred_element_type=F32)
        if n > 0:
            gate_chunk(n - 1)
    gate_chunk(EPI_SPLIT - 1)


def _merge(g, og, u, w_ssm, w_attn, row0, tm):
    t = g.shape[0]
    tn = MERGE_TN
    assert row0 % tm == 0
    blk0 = row0 // tm
    return pl.pallas_call(
        _merge_kernel,
        grid=(t // tm, D_MODEL // tn),
        in_specs=[pl.BlockSpec((tm, D_INNER), lambda i, j: (i, 0)),
                  pl.BlockSpec((tm, ATTN_WIDTH), lambda i, j: (i, 0)),
                  pl.BlockSpec((tm, tn), lambda i, j: (blk0 + i, _COL.MS // tn + j)),
                  pl.BlockSpec((tm, tn), lambda i, j: (blk0 + i, _COL.MA // tn + j)),
                  pl.BlockSpec((D_INNER, tn), lambda i, j: (0, j)),
                  pl.BlockSpec((ATTN_WIDTH, tn), lambda i, j: (0, j))],
        out_specs=pl.BlockSpec((tm, tn), lambda i, j: (i, j)),
        out_shape=jax.ShapeDtypeStruct((t, D_MODEL), BF16),
        scratch_shapes=[pltpu.VMEM((tm // EPI_SPLIT, tn), F32) for _ in range(2 * EPI_SPLIT)],
        compiler_params=_cparams(2, BIG_VMEM_LIMIT),
        name="merge",
    )(g, og, u, u, w_ssm, w_attn)


def _out_proj_kernel(mp_ref, ms_ref, xp_ref, xs_ref, wo_ref, fw_ref, op_ref, os_ref, n_prompt_tiles):
    def emit(m_ref, x_ref, o_ref):
        xn = x_ref[...] + jnp.dot(m_ref[...], wo_ref[...], preferred_element_type=F32)
        ms = jnp.mean(xn * xn, axis=-1, keepdims=True)
        o_ref[...] = xn * lax.rsqrt(ms + RMS_EPS) * fw_ref[...]

    @pl.when(pl.program_id(0) < n_prompt_tiles)
    def _():
        emit(mp_ref, xp_ref, op_ref)

    @pl.when(pl.program_id(0) >= n_prompt_tiles)
    def _():
        emit(ms_ref, xs_ref, os_ref)


def _out_proj(m_p, m_s, xp2, xs2, w_out, final_w, tm):
    tp, ts = xp2.shape[0], xs2.shape[0]
    n_p, n_s = tp // tm, ts // tm
    p_blk = lambda i: (jnp.minimum(i, n_p - 1), 0)
    s_blk = lambda i: (jnp.maximum(i - n_p, 0), 0)
    return pl.pallas_call(
        functools.partial(_out_proj_kernel, n_prompt_tiles=n_p),
        grid=(n_p + n_s,),
        in_specs=[pl.BlockSpec((tm, D_MODEL), p_blk),
                  pl.BlockSpec((tm, D_MODEL), s_blk),
                  pl.BlockSpec((tm, D_MODEL), p_blk),
                  pl.BlockSpec((tm, D_MODEL), s_blk),
                  pl.BlockSpec((D_MODEL, D_MODEL), lambda i: (0, 0), pipeline_mode=pl.Buffered(1)),
                  pl.BlockSpec((1, D_MODEL), lambda i: (0, 0))],
        out_specs=[pl.BlockSpec((tm, D_MODEL), p_blk), pl.BlockSpec((tm, D_MODEL), s_blk)],
        out_shape=[jax.ShapeDtypeStruct((tp, D_MODEL), F32), jax.ShapeDtypeStruct((ts, D_MODEL), F32)],
        compiler_params=_cparams(1, BIG_VMEM_LIMIT),
        name="out_proj",
    )(m_p, m_s, xp2, xs2, w_out, final_w)


def _pad_lanes(v):
    return jnp.pad(v.reshape(1, -1), ((0, 0), (0, DT_PAD - v.shape[-1])))


def kernel(x_prompt, x_sample, cache_k, cache_v, state_conv, state_ssm, norm_w, w_in, conv_w, conv_b, dt_bias, a_log,
           d_skip, ssm_norm_w, w_ssm_branch, attn_sinks, w_attn_branch, w_out, rel_bias, final_norm_w):
    assert w_in.shape[0] == 1, "single-layer kernel"
    batch, seq, _ = x_prompt.shape
    dec_batch, dec_seq, _ = x_sample.shape
    w_buf = cache_k.shape[2]
    t_p, t_s = batch * seq, dec_batch * dec_seq
    assert seq % SSD_CHUNK == 0 and seq % WINDOW == 0 and dec_seq == SUBLANES and w_buf == WINDOW
    assert dec_batch % SAMPLE_BT == 0 and (t_p + t_s) % IN_TM == 0

    w_t = jnp.transpose(w_in[0])
    sc_t = jnp.transpose(state_conv[0], (1, 0, 2))
    ck_t = jnp.transpose(cache_k[0], (0, 2, 3, 1))
    cv_t = jnp.transpose(cache_v[0], (0, 2, 3, 1))

    nw = norm_w[0].reshape(1, D_MODEL)
    cw, cb = conv_w[0], conv_b[0].reshape(1, CONV_DIM)
    dtb, alog = _pad_lanes(dt_bias[0]), _pad_lanes(a_log[0])
    dskip_cols = jnp.repeat(d_skip[0], SSM_HEAD_DIM).reshape(1, D_INNER)
    snw = ssm_norm_w[0].reshape(1, D_INNER)
    w_ssm = w_ssm_branch[0].astype(BF16)
    w_attn = w_attn_branch[0].astype(BF16)
    wo = w_out[0].astype(BF16)
    fw = final_norm_w.reshape(1, D_MODEL)
    sinks = attn_sinks[0]

    xp2 = x_prompt.reshape(t_p, D_MODEL)
    xs2 = x_sample.reshape(t_s, D_MODEL)

    band_bias_t = _band_bias_t(rel_bias)
    dec_bias = _decode_bias(rel_bias, sinks, dec_seq, w_buf)

    h, dt_raw = _norm_dt(xp2, xs2, nw, w_t)
    u = _in_proj(h, w_t)
    kv_t = _kv_t(h, w_t, t_p, t_s)

    g_p, st_p = _ssd_prompt(u, dt_raw, cw, cb, dtb, alog, dskip_cols, snw, batch, seq)
    og_p = _attn_prompt(u, sinks, band_bias_t, batch, seq)
    m_p = _merge(g_p, og_p, u, w_ssm, w_attn, 0, tm=1024)

    xc_s, nsc_t = _conv_sample(u, sc_t, cw, cb, t_p, dec_batch, dec_seq)
    g_s, st_s = _ssd_sample(u, dt_raw, xc_s, state_ssm[0].reshape(dec_batch, D_INNER, SSM_STATE),
                            dtb, alog, dskip_cols, snw, t_p, dec_batch, dec_seq)
    og_s, nk_t, nv_t = _attn_sample(u, kv_t, ck_t, cv_t, dec_bias, t_p, dec_batch, dec_seq)
    m_s = _merge(g_s, og_s, u, w_ssm, w_attn, t_p, tm=1024)

    y_p, y_s = _out_proj(m_p, m_s, xp2, xs2, wo, fw, tm=512)

    def seq_tail(n_rows, col0, width):
        return jnp.stack([u[(b + 1) * seq - n_rows:(b + 1) * seq, col0:col0 + width] for b in range(batch)])

    k_p = seq_tail(WINDOW, _COL.K, KV_WIDTH).reshape(1, batch, WINDOW, KV_HEADS, HEAD_DIM)
    v_p = seq_tail(WINDOW, _COL.V, KV_WIDTH).reshape(1, batch, WINDOW, KV_HEADS, HEAD_DIM)
    conv_p = seq_tail(CONV_HALO, _COL.XS, CONV_DIM)[None]
    conv_s = jnp.transpose(nsc_t, (1, 0, 2))[None]
    ssm_p = st_p.reshape(1, batch, SSM_HEADS, SSM_HEAD_DIM, SSM_STATE)
    ssm_s = st_s.reshape(1, dec_batch, SSM_HEADS, SSM_HEAD_DIM, SSM_STATE)
    k_s = jnp.transpose(nk_t, (0, 3, 1, 2))[None]
    v_s = jnp.transpose(nv_t, (0, 3, 1, 2))[None]
    return (y_p.reshape(batch, seq, D_MODEL), y_s.reshape(dec_batch, dec_seq, D_MODEL),
            k_p, v_p, conv_p, ssm_p, k_s, v_s, conv_s, ssm_s)
```
